```python
import math
import jax, jax.numpy as jnp
from jax import lax
import numpy as np

D_MODEL = 1024
BATCH = 4
SEQ = 4096
DEPTH = 4

N_MIXERS = 3
EPS = 1e-6
N_POOL_LAYERS = (DEPTH + 2) // 3
N_GDN_LAYERS = (DEPTH + 1) // 3
N_MLA_LAYERS = DEPTH // 3

POOL_WIDTH = 2 * D_MODEL
POOL_WINDOWS = (2, 4, 8, 16)
POOL_GROUP = POOL_WIDTH // len(POOL_WINDOWS)

GDN_HEADS = 8
GDN_DK = 128
GDN_DV = 256
GDN_CONV = 4
GDN_CHUNK = 64
GDN_QK = GDN_HEADS * GDN_DK
GDN_V = GDN_HEADS * GDN_DV
GDN_CONV_CH = 2 * GDN_QK + GDN_V
GDN_IN = 2 * GDN_QK + 2 * GDN_V + 2 * GDN_HEADS

MLA_HEADS = 16
MLA_NOPE = 128
MLA_ROPE = 64
MLA_V = 128
MLA_Q_LORA = 768
MLA_KV_LORA = 512
MLA_QK = MLA_NOPE + MLA_ROPE
MLA_WIDTH = MLA_HEADS * MLA_V
MLA_IN = MLA_Q_LORA + MLA_KV_LORA + MLA_ROPE + MLA_WIDTH
ROPE_THETA = 10000.0
Q_BLOCK = 128

kernel_name = "hybrid_pool_gdn_mla_trunk"


def rmsnorm(x, g):
    xf = x.astype(jnp.float32)
    y = xf * lax.rsqrt(jnp.mean(xf * xf, axis=-1, keepdims=True) + EPS)
    return (y * g.astype(jnp.float32)).astype(x.dtype)


def l2norm(x):
    return x * lax.rsqrt(jnp.sum(x * x, axis=-1, keepdims=True) + EPS)


def pool_mixer(h, w_in, w_grp, scale, w_out):
    B, S, _ = h.shape
    u, gate = jnp.split(h @ w_in, 2, axis=-1)
    c = jnp.pad(jnp.cumsum(u.astype(jnp.float32), axis=1), ((0, 0), (1, 0), (0, 0)))
    t = jnp.arange(S)
    groups = []
    for gi, w in enumerate(POOL_WINDOWS):
        sl = slice(gi * POOL_GROUP, (gi + 1) * POOL_GROUP)
        cg = c[:, :, sl]
        lo = jnp.pad(cg[:, :S - w + 1], ((0, 0), (w - 1, 0), (0, 0)))
        cnt = jnp.minimum(t + 1, w).astype(jnp.float32)[None, :, None]
        mean = (cg[:, 1:] - lo) / cnt
        groups.append(mean.astype(u.dtype) - u[..., sl])
    p = jnp.stack(groups, axis=2)
    p = jnp.einsum('bsgi,gio->bsgo', p, w_grp).reshape(B, S, POOL_WIDTH)
    y = p * scale * jax.nn.silu(gate)
    return y @ w_out


def causal_depthwise_conv(u, w):
    K, C = w.shape
    return lax.conv_general_dilated(u, w[:, None, :], window_strides=(1,),
                                    padding=[(K - 1, 0)],
                                    dimension_numbers=('NWC', 'WIO', 'NWC'),
                                    feature_group_count=C)


def chunk_gated_delta_rule(q, k, v, g, beta):
    B, S, H, dk = q.shape
    dv = v.shape[-1]
    C = GDN_CHUNK
    N = S // C

    def to_chunks(a):
        return a.reshape(B, N, C, H, -1).transpose(0, 3, 1, 2, 4)

    q, k, v = to_chunks(q), to_chunks(k), to_chunks(v)
    beta = beta.reshape(B, N, C, H).transpose(0, 3, 1, 2)
    g = jnp.cumsum(g.reshape(B, N, C, H).transpose(0, 3, 1, 2), axis=-1)
    k_beta = k * beta[..., None]
    v_beta = v * beta[..., None]
    causal = jnp.tril(jnp.ones((C, C), dtype=bool))
    strict = jnp.tril(jnp.ones((C, C), dtype=bool), -1)
    diff = g[..., :, None] - g[..., None, :]
    decay = jnp.exp(jnp.where(causal, diff, -jnp.inf))
    L = jnp.where(strict, jnp.einsum('bhnid,bhnjd->bhnij', k_beta, k) * decay, 0.0)
    A = jnp.eye(C, dtype=L.dtype) + L
    rhs = jnp.concatenate([v_beta, k_beta * jnp.exp(g)[..., None]], axis=-1)
    sol = lax.linalg.triangular_solve(A, rhs, left_side=True, lower=True)
    u_ps = sol[..., :dv]
    w_cd = sol[..., dv:]
    attn = jnp.where(causal, jnp.einsum('bhnid,bhnjd->bhnij', q, k) * decay, 0.0)

    def step(state, xs):
        q_c, k_c, u_c, w_c, g_c, a_c = xs
        v_new = u_c - jnp.einsum('bhck,bhkv->bhcv', w_c, state)
        o = (jnp.einsum('bhck,bhkv->bhcv', q_c * jnp.exp(g_c)[..., None], state)
             + jnp.einsum('bhij,bhjv->bhiv', a_c, v_new))
        g_last = g_c[..., -1]
        k_dec = k_c * jnp.exp(g_last[..., None] - g_c)[..., None]
        state = state * jnp.exp(g_last)[..., None, None] + jnp.einsum('bhck,bhcv->bhkv', k_dec, v_new)
        return state, o

    mv = lambda a: jnp.moveaxis(a, 2, 0)
    xs = (mv(q), mv(k), mv(u_ps), mv(w_cd), mv(g), mv(attn))
    state0 = jnp.zeros((B, H, dk, dv), jnp.float32)
    _, o = lax.scan(step, state0, xs)
    return o.transpose(1, 0, 3, 2, 4).reshape(B, S, H, dv)


def gdn_mixer(h, w_in, conv_w, a_log, dt_bias, norm_g, w_out):
    B, S, _ = h.shape
    f32 = jnp.float32
    proj = h @ w_in
    qkv, gate, b_raw, a_raw = jnp.split(
        proj, [GDN_CONV_CH, GDN_CONV_CH + GDN_V, GDN_CONV_CH + GDN_V + GDN_HEADS], axis=-1)
    qkv = jax.nn.silu(causal_depthwise_conv(qkv, conv_w))
    q, k, v = jnp.split(qkv, [GDN_QK, 2 * GDN_QK], axis=-1)
    q = l2norm(q.reshape(B, S, GDN_HEADS, GDN_DK).astype(f32)) * (GDN_DK ** -0.5)
    k = l2norm(k.reshape(B, S, GDN_HEADS, GDN_DK).astype(f32))
    v = v.reshape(B, S, GDN_HEADS, GDN_DV).astype(f32)
    beta = jax.nn.sigmoid(b_raw.astype(f32))
    g = -jnp.exp(a_log.astype(f32)) * jax.nn.softplus(a_raw.astype(f32) + dt_bias.astype(f32))
    o = chunk_gated_delta_rule(q, k, v, g, beta)
    o = rmsnorm(o, norm_g) * jax.nn.silu(gate.reshape(B, S, GDN_HEADS, GDN_DV).astype(f32))
    return o.reshape(B, S, GDN_V).astype(h.dtype) @ w_out


def rope(x, pos):
    half = x.shape[-1] // 2
    inv = ROPE_THETA ** (-jnp.arange(half, dtype=jnp.float32) / half)
    ang = pos.astype(jnp.float32)[..., None, None] * inv
    cos, sin = jnp.cos(ang), jnp.sin(ang)
    x1, x2 = x[..., :half], x[..., half:]
    return jnp.concatenate([x1 * cos - x2 * sin, x1 * sin + x2 * cos], axis=-1).astype(x.dtype)


def causal_block_attention(q, k, v):
    B, S, H, Dq = q.shape
    nb = S // Q_BLOCK
    scale = Dq ** -0.5
    qb = q.reshape(B, nb, Q_BLOCK, H, Dq).transpose(1, 0, 2, 3, 4)
    kpos = jnp.arange(S)

    def block(args):
        i, q_blk = args
        s = jnp.einsum('bqhd,bkhd->bhqk', q_blk, k).astype(jnp.float32) * scale
        qpos = i * Q_BLOCK + jnp.arange(Q_BLOCK)
        s = jnp.where(kpos[None, :] <= qpos[:, None], s, -jnp.inf)
        p = jax.nn.softmax(s, axis=-1)
        return jnp.einsum('bhqk,bkhd->bqhd', p.astype(v.dtype), v)

    o = lax.map(block, (jnp.arange(nb), qb))
    return o.transpose(1, 0, 2, 3, 4).reshape(B, S, H, v.shape[-1])


def mla_mixer(h, pos, w_in, q_norm_g, w_uq, kv_norm_g, w_ukv, w_out):
    B, S, _ = h.shape
    proj = h @ w_in
    cq, ckv, k_rope, gate = jnp.split(
        proj, [MLA_Q_LORA, MLA_Q_LORA + MLA_KV_LORA, MLA_Q_LORA + MLA_KV_LORA + MLA_ROPE], axis=-1)
    q = (rmsnorm(cq, q_norm_g) @ w_uq).reshape(B, S, MLA_HEADS, MLA_QK)
    kv = (rmsnorm(ckv, kv_norm_g) @ w_ukv).reshape(B, S, MLA_HEADS, MLA_NOPE + MLA_V)
    q_nope, q_rope = q[..., :MLA_NOPE], q[..., MLA_NOPE:]
    k_nope, v = kv[..., :MLA_NOPE], kv[..., MLA_NOPE:]
    q_rope = rope(q_rope, pos)
    k_rope = rope(k_rope[:, :, None, :], pos)
    q = jnp.concatenate([q_nope, q_rope], axis=-1)
    k = jnp.concatenate([k_nope, jnp.broadcast_to(k_rope, (B, S, MLA_HEADS, MLA_ROPE))], axis=-1)
    o = causal_block_attention(q, k, v).reshape(B, S, MLA_WIDTH)
    return (o * jax.nn.silu(gate)) @ w_out


def setup_inputs(seed: int = 0) -> dict:
    key = jax.random.key(seed)
    ks = jax.random.split(key, 24)
    nrm = lambda k, shape, fan: jax.random.normal(k, shape, jnp.float32) * (fan ** -0.5)
    gain = lambda k, shape: 1.0 + 0.02 * jax.random.normal(k, shape, jnp.float32)
    nA, nB, nC = N_POOL_LAYERS, N_GDN_LAYERS, N_MLA_LAYERS
    x = jax.random.normal(ks[0], (BATCH, SEQ, D_MODEL), jnp.float32)
    positions = jnp.broadcast_to(jnp.arange(SEQ, dtype=jnp.int32)[None, :], (BATCH, SEQ))
    dt = jnp.exp(jax.random.uniform(ks[10], (nB, GDN_HEADS), jnp.float32,
                                    math.log(1e-3), math.log(1e-1)))
    return {
        "x": x,
        "positions": positions,
        "norm_g": gain(ks[1], (DEPTH, D_MODEL)),
        "pool_w_in": nrm(ks[2], (nA, D_MODEL, 2 * POOL_WIDTH), D_MODEL),
        "pool_w_grp": nrm(ks[3], (nA, len(POOL_WINDOWS), POOL_GROUP, POOL_GROUP), POOL_GROUP),
        "pool_scale": gain(ks[4], (nA, POOL_WIDTH)),
        "pool_w_out": nrm(ks[5], (nA, POOL_WIDTH, D_MODEL), POOL_WIDTH),
        "gdn_w_in": nrm(ks[6], (nB, D_MODEL, GDN_IN), D_MODEL),
        "gdn_conv": nrm(ks[7], (nB, GDN_CONV, GDN_CONV_CH), GDN_CONV),
        "gdn_a_log": jnp.log(jax.random.uniform(ks[8], (nB, GDN_HEADS), jnp.float32, 1.0, 16.0)),
        "gdn_dt_bias": dt + jnp.log(-jnp.expm1(-dt)),
        "gdn_norm_g": gain(ks[9], (nB, GDN_DV)),
        "gdn_w_out": nrm(ks[11], (nB, GDN_V, D_MODEL), GDN_V),
        "mla_w_in": nrm(ks[12], (nC, D_MODEL, MLA_IN), D_MODEL),
        "mla_q_norm_g": gain(ks[13], (nC, MLA_Q_LORA)),
        "mla_w_uq": nrm(ks[14], (nC, MLA_Q_LORA, MLA_HEADS * MLA_QK), MLA_Q_LORA),
        "mla_kv_norm_g": gain(ks[15], (nC, MLA_KV_LORA)),
        "mla_w_ukv": nrm(ks[16], (nC, MLA_KV_LORA, MLA_HEADS * (MLA_NOPE + MLA_V)), MLA_KV_LORA),
        "mla_w_out": nrm(ks[17], (nC, MLA_WIDTH, D_MODEL), MLA_WIDTH),
        "final_g": gain(ks[18], (D_MODEL,)),
    }


def reference(x, positions, norm_g, pool_w_in, pool_w_grp, pool_scale, pool_w_out,
              gdn_w_in, gdn_conv, gdn_a_log, gdn_dt_bias, gdn_norm_g, gdn_w_out,
              mla_w_in, mla_q_norm_g, mla_w_uq, mla_kv_norm_g, mla_w_ukv, mla_w_out,
              final_g):
    for i in range(DEPTH):
        kind, j = i % N_MIXERS, i // N_MIXERS
        h = rmsnorm(x, norm_g[i])
        if kind == 0:
            y = pool_mixer(h, pool_w_in[j], pool_w_grp[j], pool_scale[j], pool_w_out[j])
        elif kind == 1:
            y = gdn_mixer(h, gdn_w_in[j], gdn_conv[j], gdn_a_log[j], gdn_dt_bias[j],
                          gdn_norm_g[j], gdn_w_out[j])
        else:
            y = mla_mixer(h, positions, mla_w_in[j], mla_q_norm_g[j], mla_w_uq[j],
                          mla_kv_norm_g[j], mla_w_ukv[j], mla_w_out[j])
        x = x + y.astype(x.dtype)
    return rmsnorm(x, final_g)
```

```python
import functools

import jax
import jax.numpy as jnp
from jax import lax
from jax.experimental import pallas as pl
from jax.experimental.pallas import tpu as pltpu

F32 = jnp.float32
BF16 = jnp.bfloat16
EPS = 1e-6
NEG = -1e30
VMEM_LIMIT_V7X = 56 * 2 ** 20
LANES = 128

N_MIXERS = 3

POOL_WINDOWS = (2, 4, 8, 16)
POOL_HALO = 16

GDN_HEADS = 8
GDN_DK = 128
GDN_DV = 256
GDN_CONV = 4
GDN_CHUNK = 64
GDN_QK = GDN_HEADS * GDN_DK
GDN_V = GDN_HEADS * GDN_DV
GDN_UNIT = 2 * GDN_CHUNK
CONV_HALO = 8

MLA_HEADS = 16
MLA_NOPE = 128
MLA_ROPE = 64
MLA_V = 128
MLA_Q_LORA = 768
MLA_KV_LORA = 512
MLA_QK = MLA_NOPE + MLA_ROPE
MLA_QPAD = 256
ROPE_THETA = 10000.0


def _const_spec(shape):
    zeros = (0,) * len(shape)
    return pl.BlockSpec(shape, lambda *_: zeros, pipeline_mode=pl.Buffered(1))


def _params(n_axes):
    return pltpu.CompilerParams(dimension_semantics=("arbitrary",) * n_axes,
                                vmem_limit_bytes=VMEM_LIMIT_V7X)


def _rms(x, g):
    return x * lax.rsqrt(jnp.mean(x * x, axis=-1, keepdims=True) + EPS) * g


def _silu(x):
    return x * jax.nn.sigmoid(x)


def _mm(a, b):
    return jnp.dot(a.astype(BF16), b.astype(BF16), preferred_element_type=F32)


def _mm_nt(a, b):
    return lax.dot_general(a.astype(BF16), b.astype(BF16), (((1,), (1,)), ((), ())),
                           preferred_element_type=F32)


def _pool_body(x_ref, ng_ref, win_ref, wgrp_ref, sc_ref, wout_ref, fg_ref, o_ref, ubuf_ref,
               *, ts, width, final):
    j = pl.program_id(1)
    grp = width // len(POOL_WINDOWS)

    @pl.when(j == 0)
    def _():
        ubuf_ref[0:POOL_HALO, :] = jnp.zeros((POOL_HALO, width), F32)

    x = x_ref[0]
    h = _rms(x, ng_ref[...]).astype(BF16)
    u = jnp.dot(h, win_ref[:, :width], preferred_element_type=F32)
    gate = jnp.dot(h, win_ref[:, width:], preferred_element_type=F32)
    ubuf_ref[POOL_HALO:POOL_HALO + ts, :] = u
    t = j * ts + lax.broadcasted_iota(jnp.int32, (ts, 1), 0)
    parts = []
    for gi, w in enumerate(POOL_WINDOWS):
        c0 = gi * grp
        ug = u[:, c0:c0 + grp]
        s = ug
        for d in range(1, w):
            s = s + ubuf_ref[POOL_HALO - d:POOL_HALO - d + ts, c0:c0 + grp]
        cnt = jnp.minimum(t + 1, w).astype(F32)
        p = s / cnt - ug
        parts.append(jnp.dot(p.astype(BF16), wgrp_ref[gi], preferred_element_type=F32))
    pg = jnp.concatenate(parts, axis=1)
    y = pg * sc_ref[...] * _silu(gate)
    out = x + jnp.dot(y.astype(BF16), wout_ref[...], preferred_element_type=F32)
    ubuf_ref[0:POOL_HALO, :] = u[ts - POOL_HALO:, :]
    if final:
        out = _rms(out, fg_ref[...])
    o_ref[0] = out


def _pool_layer(x, ng, w_in, w_grp, scale, w_out, final_g, *, ts=256):
    B, S, D = x.shape
    width = w_out.shape[0]
    final = final_g is not None
    fg = final_g if final else jnp.ones((D,), F32)
    body = functools.partial(_pool_body, ts=ts, width=width, final=final)
    return pl.pallas_call(
        body,
        grid=(B, S // ts),
        in_specs=[
            pl.BlockSpec((1, ts, D), lambda b, j: (b, j, 0)),
            _const_spec((1, D)),
            _const_spec(w_in.shape),
            _const_spec(w_grp.shape),
            _const_spec((1, width)),
            _const_spec(w_out.shape),
            _const_spec((1, D)),
        ],
        out_specs=pl.BlockSpec((1, ts, D), lambda b, j: (b, j, 0)),
        out_shape=jax.ShapeDtypeStruct((B, S, D), F32),
        scratch_shapes=[pltpu.VMEM((POOL_HALO + ts, width), F32)],
        compiler_params=_params(2),
        name="pool_layer",
    )(x, ng.reshape(1, D), w_in.astype(BF16), w_grp.astype(BF16), scale.reshape(1, width),
      w_out.astype(BF16), fg.reshape(1, D))


def _gdn_in_body(x_ref, ng_ref, wqkv_ref, wgate_ref, wba_ref, conv_ref, alog_ref, dtb_ref,
                 q_ref, k_ref, v_ref, gate_ref, gb_ref, cbuf_ref, *, ts):
    j = pl.program_id(1)
    nch = 2 * GDN_QK + GDN_V

    @pl.when(j == 0)
    def _():
        cbuf_ref[0:CONV_HALO, :] = jnp.zeros((CONV_HALO, nch), F32)

    x = x_ref[0]
    h = _rms(x, ng_ref[...]).astype(BF16)
    pre = jnp.dot(h, wqkv_ref[...], preferred_element_type=F32)
    cbuf_ref[CONV_HALO:CONV_HALO + ts, :] = pre
    acc = pre * conv_ref[GDN_CONV - 1:GDN_CONV, :]
    for kk in range(GDN_CONV - 1):
        r0 = CONV_HALO - (GDN_CONV - 1) + kk
        acc = acc + cbuf_ref[r0:r0 + ts, :] * conv_ref[kk:kk + 1, :]
    cbuf_ref[0:CONV_HALO, :] = pre[ts - CONV_HALO:, :]
    a = _silu(acc)
    for hh in range(GDN_HEADS):
        c = slice(hh * GDN_DK, (hh + 1) * GDN_DK)
        qh = a[:, c]
        qn = qh * lax.rsqrt(jnp.sum(qh * qh, axis=-1, keepdims=True) + EPS) * (GDN_DK ** -0.5)
        q_ref[0, :, c] = qn.astype(BF16)
        kh = a[:, GDN_QK + hh * GDN_DK:GDN_QK + (hh + 1) * GDN_DK]
        kn = kh * lax.rsqrt(jnp.sum(kh * kh, axis=-1, keepdims=True) + EPS)
        k_ref[0, :, c] = kn.astype(BF16)
    v_ref[0] = a[:, 2 * GDN_QK:].astype(BF16)
    gate_ref[0] = _silu(jnp.dot(h, wgate_ref[...], preferred_element_type=F32)).astype(BF16)

    ba = jnp.dot(h, wba_ref[...], preferred_element_type=F32)
    lane = lax.broadcasted_iota(jnp.int32, (1, LANES), 1)
    z = ba + dtb_ref[...]
    softplus = jnp.maximum(z, 0.0) + jnp.log1p(jnp.exp(-jnp.abs(z)))
    g = jnp.where((lane >= GDN_HEADS) & (lane < 2 * GDN_HEADS),
                  -jnp.exp(alog_ref[...]) * softplus, 0.0)
    r = lax.broadcasted_iota(jnp.int32, (ts, 1), 0)
    c = lax.broadcasted_iota(jnp.int32, (1, ts), 1)
    tri = jnp.where(((r // GDN_CHUNK) == (c // GDN_CHUNK)) & (c <= r), 1.0, 0.0).astype(BF16)
    g_hi = g.astype(BF16)
    r1 = g - g_hi.astype(F32)
    g_mid = r1.astype(BF16)
    g_lo = (r1 - g_mid.astype(F32)).astype(BF16)
    gc = (jnp.dot(tri, g_hi, preferred_element_type=F32)
          + jnp.dot(tri, g_mid, preferred_element_type=F32)
          + jnp.dot(tri, g_lo, preferred_element_type=F32))
    gb_ref[0] = jnp.where(lane < GDN_HEADS, jax.nn.sigmoid(ba), gc)


def _gdn_core_body(q_ref, k_ref, v_ref, gb_ref, gbt_ref, o_ref, s_ref):
    j = pl.program_id(1)

    @pl.when(j == 0)
    def _():
        s_ref[...] = jnp.zeros_like(s_ref)

    n = GDN_UNIT
    r = lax.broadcasted_iota(jnp.int32, (n, 1), 0)
    c = lax.broadcasted_iota(jnp.int32, (1, n), 1)
    same = (r // GDN_CHUNK) == (c // GDN_CHUNK)
    causal = same & (c <= r)
    strict = same & (c < r)
    eye = jnp.where(r == c, 1.0, 0.0)
    gb = gb_ref[0]
    gbt = gbt_ref[0]
    zeros_c = jnp.zeros((GDN_CHUNK, GDN_DV), F32)
    for h in range(GDN_HEADS):
        beta_c = gb[:, h:h + 1]
        gc_c = gb[:, GDN_HEADS + h:GDN_HEADS + h + 1]
        gc_r = gbt[GDN_HEADS + h:GDN_HEADS + h + 1, :]
        q = q_ref[0, :, h * GDN_DK:(h + 1) * GDN_DK]
        k = k_ref[0, :, h * GDN_DK:(h + 1) * GDN_DK]
        v = v_ref[0, :, h * GDN_DV:(h + 1) * GDN_DV]
        kf = k.astype(F32)
        kb = kf * beta_c
        decay = jnp.exp(jnp.where(causal, gc_c - gc_r, NEG))
        kkqk = _mm_nt(jnp.concatenate([kb.astype(BF16), q], axis=0), k)
        lmat = jnp.where(strict, kkqk[:n] * decay, 0.0)
        attn = kkqk[n:] * decay
        xm = eye - lmat
        pw = _mm(lmat, lmat)
        for _ in range(4):
            res = _mm(jnp.concatenate([xm, pw], axis=0), pw)
            xm = xm + res[:n]
            pw = res[n:]
        xm = xm + _mm(xm, pw)
        eg = jnp.exp(gc_c)
        rhs = jnp.concatenate([v.astype(F32) * beta_c, kb * eg], axis=1)
        sol = rhs + _mm(xm - eye, rhs)
        u_ps = sol[:, :GDN_DV]
        w_cd = sol[:, GDN_DV:]
        qg = q.astype(F32) * eg
        k_t = kf.T
        state = s_ref[h]
        for cc in range(2):
            rows = slice(cc * GDN_CHUNK, (cc + 1) * GDN_CHUNK)
            g_last = gc_r[:, (cc + 1) * GDN_CHUNK - 1:(cc + 1) * GDN_CHUNK]
            res = _mm(jnp.concatenate([w_cd[rows], qg[rows]], axis=0), state)
            v_new = u_ps[rows] - res[:GDN_CHUNK]
            v_all = jnp.concatenate([v_new, zeros_c] if cc == 0 else [zeros_c, v_new], axis=0)
            o_ref[0, rows, h * GDN_DV:(h + 1) * GDN_DV] = res[GDN_CHUNK:] + _mm(attn[rows], v_all)
            in_chunk = (c // GDN_CHUNK) == cc
            k_dec_t = k_t * jnp.exp(jnp.where(in_chunk, g_last - gc_r, NEG))
            state = state * jnp.exp(g_last) + _mm(k_dec_t, v_all)
        s_ref[h] = state


def _gdn_out_body(x_ref, o_ref, gate_ref, ng_ref, wout_ref, y_ref):
    o = o_ref[0]
    parts = []
    for h in range(GDN_HEADS):
        oh = o[:, h * GDN_DV:(h + 1) * GDN_DV]
        parts.append(_rms(oh, ng_ref[...]))
    on = jnp.concatenate(parts, axis=1) * gate_ref[0].astype(F32)
    y_ref[0] = x_ref[0] + jnp.dot(on.astype(BF16), wout_ref[...], preferred_element_type=F32)


def _gdn_layer(x, ng, w_in, conv_w, a_log, dt_bias, norm_g, w_out, *, ts=256):
    B, S, D = x.shape
    nch = 2 * GDN_QK + GDN_V
    w_qkv = w_in[:, :nch].astype(BF16)
    w_gate = w_in[:, nch:nch + GDN_V].astype(BF16)
    w_ba = jnp.pad(w_in[:, nch + GDN_V:], ((0, 0), (0, LANES - 2 * GDN_HEADS))).astype(BF16)
    pad_a = (GDN_HEADS, LANES - 2 * GDN_HEADS)
    alog = jnp.pad(a_log.astype(F32), pad_a).reshape(1, LANES)
    dtb = jnp.pad(dt_bias.astype(F32), pad_a).reshape(1, LANES)

    tok = lambda w: pl.BlockSpec((1, ts, w), lambda b, j: (b, j, 0))
    q, k, v, gate, gb = pl.pallas_call(
        functools.partial(_gdn_in_body, ts=ts),
        grid=(B, S // ts),
        in_specs=[tok(D), _const_spec((1, D)), _const_spec(w_qkv.shape), _const_spec(w_gate.shape),
                  _const_spec(w_ba.shape), _const_spec(conv_w.shape), _const_spec((1, LANES)),
                  _const_spec((1, LANES))],
        out_specs=[tok(GDN_QK), tok(GDN_QK), tok(GDN_V), tok(GDN_V), tok(LANES)],
        out_shape=[jax.ShapeDtypeStruct((B, S, GDN_QK), BF16),
                   jax.ShapeDtypeStruct((B, S, GDN_QK), BF16),
                   jax.ShapeDtypeStruct((B, S, GDN_V), BF16),
                   jax.ShapeDtypeStruct((B, S, GDN_V), BF16),
                   jax.ShapeDtypeStruct((B, S, LANES), F32)],
        scratch_shapes=[pltpu.VMEM((CONV_HALO + ts, nch), F32)],
        compiler_params=_params(2),
        name="gdn_in",
    )(x, ng.reshape(1, D), w_qkv, w_gate, w_ba, conv_w, alog, dtb)

    gbt = jnp.transpose(gb[:, :, :2 * GDN_HEADS], (0, 2, 1))
    n = GDN_UNIT
    utok = lambda w: pl.BlockSpec((1, n, w), lambda b, j: (b, j, 0))
    o = pl.pallas_call(
        _gdn_core_body,
        grid=(B, S // n),
        in_specs=[utok(GDN_QK), utok(GDN_QK), utok(GDN_V), utok(LANES),
                  pl.BlockSpec((1, 2 * GDN_HEADS, n), lambda b, j: (b, 0, j))],
        out_specs=utok(GDN_V),
        out_shape=jax.ShapeDtypeStruct((B, S, GDN_V), F32),
        scratch_shapes=[pltpu.VMEM((GDN_HEADS, GDN_DK, GDN_DV), F32)],
        compiler_params=_params(2),
        name="gdn_core",
    )(q, k, v, gb, gbt)

    return pl.pallas_call(
        _gdn_out_body,
        grid=(B, S // ts),
        in_specs=[tok(D), tok(GDN_V), tok(GDN_V), _const_spec((1, GDN_DV)),
                  _const_spec(w_out.shape)],
        out_specs=tok(D),
        out_shape=jax.ShapeDtypeStruct((B, S, D), F32),
        compiler_params=_params(2),
        name="gdn_out",
    )(x, o, gate, norm_g.reshape(1, GDN_DV), w_out.astype(BF16))


def _mla_in_body(x_ref, pos_ref, ng_ref, win_ref, qg_ref, wuq_ref, kvg_ref, wukv_ref, inv_ref,
                 q_ref, k_ref, v_ref, gate_ref):
    c1 = MLA_Q_LORA
    c2 = c1 + MLA_KV_LORA
    c3 = c2 + LANES
    x = x_ref[0]
    h = _rms(x, ng_ref[...]).astype(BF16)
    cq = jnp.dot(h, win_ref[:, :c1], preferred_element_type=F32)
    ckv = jnp.dot(h, win_ref[:, c1:c2], preferred_element_type=F32)
    kr = jnp.dot(h, win_ref[:, c2:c3], preferred_element_type=F32)
    gate = jnp.dot(h, win_ref[:, c3:], preferred_element_type=F32)
    gate_ref[0] = _silu(gate).astype(BF16)

    ang = pos_ref[0].astype(F32) * inv_ref[...]
    cos = jnp.cos(ang)
    sin = jnp.sin(ang)
    lane = lax.broadcasted_iota(jnp.int32, (1, LANES), 1)
    quarter = lane // (MLA_ROPE // 2)
    sin_s = sin * jnp.where((quarter == 0) | (quarter == 3), -1.0, 1.0)

    def rope(t):
        return t * cos + pltpu.roll(t, MLA_ROPE, 1) * sin_s

    k_rope = jnp.where(lane < MLA_ROPE, rope(kr), 0.0).astype(BF16)
    qf = jnp.dot(_rms(cq, qg_ref[...]).astype(BF16), wuq_ref[...],
                 preferred_element_type=F32) * (MLA_QK ** -0.5)
    kv = jnp.dot(_rms(ckv, kvg_ref[...]).astype(BF16), wukv_ref[...], preferred_element_type=F32)
    for hh in range(MLA_HEADS):
        q0 = hh * MLA_QPAD
        q_ref[0, hh, :, 0:MLA_NOPE] = qf[:, q0:q0 + MLA_NOPE].astype(BF16)
        q_ref[0, hh, :, MLA_NOPE:] = rope(qf[:, q0 + MLA_NOPE:q0 + MLA_QPAD]).astype(BF16)
        k0 = hh * (MLA_NOPE + MLA_V)
        k_ref[0, hh, :, 0:MLA_NOPE] = kv[:, k0:k0 + MLA_NOPE].astype(BF16)
        k_ref[0, hh, :, MLA_NOPE:] = k_rope
        v_ref[0, hh] = kv[:, k0 + MLA_NOPE:k0 + MLA_NOPE + MLA_V].astype(BF16)


def _attn_body(q_ref, k_ref, v_ref, o_ref, *, tq, tk):
    i = pl.program_id(2)
    q = q_ref[0, 0]

    def step(kb, carry, masked):
        m, l, acc = carry
        off = pl.multiple_of(kb * tk, tk)
        k = k_ref[0, 0, pl.ds(off, tk), :]
        v = v_ref[0, 0, pl.ds(off, tk), :]
        s = lax.dot_general(q, k, (((1,), (1,)), ((), ())), preferred_element_type=F32)
        if masked:
            row = i * tq + lax.broadcasted_iota(jnp.int32, (tq, 1), 0)
            col = kb * tk + lax.broadcasted_iota(jnp.int32, (1, tk), 1)
            s = jnp.where(col <= row, s, NEG)
        m_new = jnp.maximum(m, jnp.max(s, axis=-1, keepdims=True))
        p = jnp.exp(s - m_new)
        alpha = jnp.exp(m - m_new)
        l = alpha * l + jnp.sum(p, axis=-1, keepdims=True)
        acc = alpha * acc + jnp.dot(p.astype(BF16), v, preferred_element_type=F32)
        return m_new, l, acc

    n_full = i * (tq // tk)
    carry = (jnp.full((tq, 1), NEG, F32), jnp.zeros((tq, 1), F32), jnp.zeros((tq, MLA_V), F32))
    carry = lax.fori_loop(0, n_full, lambda kb, cr: step(kb, cr, False), carry)
    for d in range(tq // tk):
        carry = step(n_full + d, carry, True)
    _, l, acc = carry
    o_ref[0] = acc / l


def _gated_out_body(x_ref, o_ref, gate_ref, wout_ref, y_ref):
    y = o_ref[0] * gate_ref[0].astype(F32)
    y_ref[0] = x_ref[0] + jnp.dot(y.astype(BF16), wout_ref[...], preferred_element_type=F32)


def _mla_layer(x, pos, ng, w_in, q_norm_g, w_uq, kv_norm_g, w_ukv, w_out, *, ts=256, tq=512, tk=512):
    B, S, D = x.shape
    H = MLA_HEADS
    half = MLA_ROPE // 2
    c1 = MLA_Q_LORA
    c2 = c1 + MLA_KV_LORA
    c3 = c2 + MLA_ROPE
    w_kr = w_in[:, c2:c3]
    w_kr = jnp.concatenate([w_kr, w_kr[:, half:], w_kr[:, :half]], axis=1)
    w_in_p = jnp.concatenate([w_in[:, :c2], w_kr, w_in[:, c3:]], axis=1).astype(BF16)
    wq = w_uq.reshape(MLA_Q_LORA, H, MLA_QK)
    wq_r = wq[:, :, MLA_NOPE:]
    wq_p = jnp.concatenate([wq, wq_r[:, :, half:], wq_r[:, :, :half]], axis=2)
    wq_p = wq_p.reshape(MLA_Q_LORA, H * MLA_QPAD).astype(BF16)
    inv = ROPE_THETA ** (-jnp.arange(half, dtype=F32) / half)
    inv = jnp.tile(inv, LANES // half).reshape(1, LANES)
    width = H * MLA_V

    tok = lambda w: pl.BlockSpec((1, ts, w), lambda b, j: (b, j, 0))
    head = lambda w: pl.BlockSpec((1, H, ts, w), lambda b, j: (b, 0, j, 0))
    q, k, v, gate = pl.pallas_call(
        _mla_in_body,
        grid=(B, S // ts),
        in_specs=[tok(D), tok(1), _const_spec((1, D)), _const_spec(w_in_p.shape),
                  _const_spec((1, MLA_Q_LORA)), _const_spec(wq_p.shape),
                  _const_spec((1, MLA_KV_LORA)), _const_spec(w_ukv.shape), _const_spec((1, LANES))],
        out_specs=[head(MLA_QPAD), head(MLA_QPAD), head(MLA_V), tok(width)],
        out_shape=[jax.ShapeDtypeStruct((B, H, S, MLA_QPAD), BF16),
                   jax.ShapeDtypeStruct((B, H, S, MLA_QPAD), BF16),
                   jax.ShapeDtypeStruct((B, H, S, MLA_V), BF16),
                   jax.ShapeDtypeStruct((B, S, width), BF16)],
        compiler_params=_params(2),
        name="mla_in",
    )(x, pos.reshape(B, S, 1), ng.reshape(1, D), w_in_p, q_norm_g.reshape(1, MLA_Q_LORA), wq_p,
      kv_norm_g.reshape(1, MLA_KV_LORA), w_ukv.astype(BF16), inv)

    o = pl.pallas_call(
        functools.partial(_attn_body, tq=tq, tk=tk),
        grid=(B, H, S // tq),
        in_specs=[pl.BlockSpec((1, 1, tq, MLA_QPAD), lambda b, h, i: (b, h, i, 0)),
                  pl.BlockSpec((1, 1, S, MLA_QPAD), lambda b, h, i: (b, h, 0, 0)),
                  pl.BlockSpec((1, 1, S, MLA_V), lambda b, h, i: (b, h, 0, 0))],
        out_specs=pl.BlockSpec((1, tq, MLA_V), lambda b, h, i: (b, i, h)),
        out_shape=jax.ShapeDtypeStruct((B, S, width), F32),
        compiler_params=_params(3),
        name="mla_attn",
    )(q, k, v)

    return pl.pallas_call(
        _gated_out_body,
        grid=(B, S // ts),
        in_specs=[tok(D), tok(width), tok(width), _const_spec(w_out.shape)],
        out_specs=tok(D),
        out_shape=jax.ShapeDtypeStruct((B, S, D), F32),
        compiler_params=_params(2),
        name="mla_out",
    )(x, o, gate, w_out.astype(BF16))


def kernel(x, positions, norm_g, pool_w_in, pool_w_grp, pool_scale, pool_w_out, gdn_w_in, gdn_conv, gdn_a_log, gdn_dt_bias, gdn_norm_g, gdn_w_out, mla_w_in, mla_q_norm_g, mla_w_uq, mla_kv_norm_g, mla_w_ukv, mla_w_out, final_g):
    depth = norm_g.shape[0]
    for i in range(depth):
        kind, j = i % N_MIXERS, i // N_MIXERS
        last = i == depth - 1
        if kind == 0:
            x = _pool_layer(x, norm_g[i], pool_w_in[j], pool_w_grp[j], pool_scale[j], pool_w_out[j],
                            final_g if last else None)
        elif kind == 1:
            x = _gdn_layer(x, norm_g[i], gdn_w_in[j], gdn_conv[j], gdn_a_log[j], gdn_dt_bias[j],
                           gdn_norm_g[j], gdn_w_out[j])
        else:
            x = _mla_layer(x, positions, norm_g[i], mla_w_in[j], mla_q_norm_g[j], mla_w_uq[j],
                           mla_kv_norm_g[j], mla_w_ukv[j], mla_w_out[j])
        if last and kind != 0:
            x = _final_norm(x, final_g)
    return x


def _final_norm_body(x_ref, g_ref, o_ref):
    o_ref[0] = _rms(x_ref[0], g_ref[...])


def _final_norm(x, g, *, ts=512):
    B, S, D = x.shape
    tok = pl.BlockSpec((1, ts, D), lambda b, j: (b, j, 0))
    return pl.pallas_call(
        _final_norm_body, grid=(B, S // ts), in_specs=[tok, _const_spec((1, D))], out_specs=tok,
        out_shape=jax.ShapeDtypeStruct((B, S, D), F32), compiler_params=_params(2),
        name="final_norm",
    )(x, g.reshape(1, D))
```

```python
import functools

import jax
import jax.numpy as jnp
from jax import lax
from jax.experimental import pallas as pl
from jax.experimental.pallas import tpu as pltpu

F32 = jnp.float32
BF16 = jnp.bfloat16
EPS = 1e-6
NEG = -1e30
VMEM_LIMIT_V7X = 56 * 2 ** 20
LANES = 128

N_MIXERS = 3

POOL_WINDOWS = (2, 4, 8, 16)
POOL_HALO = 16

GDN_HEADS = 8
GDN_DK = 128
GDN_DV = 256
GDN_CONV = 4
GDN_CHUNK = 64
GDN_QK = GDN_HEADS * GDN_DK
GDN_V = GDN_HEADS * GDN_DV
GDN_UNIT = 2 * GDN_CHUNK
CONV_HALO = 8

MLA_HEADS = 16
MLA_NOPE = 128
MLA_ROPE = 64
MLA_V = 128
MLA_Q_LORA = 768
MLA_KV_LORA = 512
MLA_QK = MLA_NOPE + MLA_ROPE
MLA_QPAD = 256
ROPE_THETA = 10000.0
ATTN_STRIP = 64
LOG2E = 1.4426950408889634


def _const_spec(shape):
    zeros = (0,) * len(shape)
    return pl.BlockSpec(shape, lambda *_: zeros, pipeline_mode=pl.Buffered(1))


def _params(n_axes):
    return pltpu.CompilerParams(dimension_semantics=("arbitrary",) * n_axes,
                                vmem_limit_bytes=VMEM_LIMIT_V7X)


def _rms(x, g):
    return x * lax.rsqrt(jnp.mean(x * x, axis=-1, keepdims=True) + EPS) * g


def _silu(x):
    return x * jax.nn.sigmoid(x)


def _mm(a, b):
    return jnp.dot(a.astype(BF16), b.astype(BF16), preferred_element_type=F32)


def _mm_nt(a, b):
    return lax.dot_general(a.astype(BF16), b.astype(BF16), (((1,), (1,)), ((), ())),
                           preferred_element_type=F32)


def _pool_body(x_ref, ng_ref, win_ref, wgrp_ref, sc_ref, wout_ref, fg_ref, o_ref, ubuf_ref,
               *, ts, width, final):
    j = pl.program_id(1)
    grp = width // len(POOL_WINDOWS)

    @pl.when(j == 0)
    def _():
        ubuf_ref[0:POOL_HALO, :] = jnp.zeros((POOL_HALO, width), F32)

    x = x_ref[0]
    h = _rms(x, ng_ref[...]).astype(BF16)
    u = jnp.dot(h, win_ref[:, :width], preferred_element_type=F32)
    gate = jnp.dot(h, win_ref[:, width:], preferred_element_type=F32)
    ubuf_ref[POOL_HALO:POOL_HALO + ts, :] = u
    t = j * ts + lax.broadcasted_iota(jnp.int32, (ts, 1), 0)
    parts = []
    for gi, w in enumerate(POOL_WINDOWS):
        c0 = gi * grp
        ug = u[:, c0:c0 + grp]
        s = ug
        for d in range(1, w):
            s = s + ubuf_ref[POOL_HALO - d:POOL_HALO - d + ts, c0:c0 + grp]
        cnt = jnp.minimum(t + 1, w).astype(F32)
        p = s / cnt - ug
        parts.append(jnp.dot(p.astype(BF16), wgrp_ref[gi], preferred_element_type=F32))
    pg = jnp.concatenate(parts, axis=1)
    y = pg * sc_ref[...] * _silu(gate)
    out = x + jnp.dot(y.astype(BF16), wout_ref[...], preferred_element_type=F32)
    ubuf_ref[0:POOL_HALO, :] = u[ts - POOL_HALO:, :]
    if final:
        out = _rms(out, fg_ref[...])
    o_ref[0] = out


def _pool_layer(x, ng, w_in, w_grp, scale, w_out, final_g, *, ts=256):
    B, S, D = x.shape
    width = w_out.shape[0]
    final = final_g is not None
    fg = final_g if final else jnp.ones((D,), F32)
    body = functools.partial(_pool_body, ts=ts, width=width, final=final)
    return pl.pallas_call(
        body,
        grid=(B, S // ts),
        in_specs=[
            pl.BlockSpec((1, ts, D), lambda b, j: (b, j, 0)),
            _const_spec((1, D)),
            _const_spec(w_in.shape),
            _const_spec(w_grp.shape),
            _const_spec((1, width)),
            _const_spec(w_out.shape),
            _const_spec((1, D)),
        ],
        out_specs=pl.BlockSpec((1, ts, D), lambda b, j: (b, j, 0)),
        out_shape=jax.ShapeDtypeStruct((B, S, D), F32),
        scratch_shapes=[pltpu.VMEM((POOL_HALO + ts, width), F32)],
        compiler_params=_params(2),
        name="pool_layer",
    )(x, ng.reshape(1, D), w_in.astype(BF16), w_grp.astype(BF16), scale.reshape(1, width),
      w_out.astype(BF16), fg.reshape(1, D))


def _gdn_in_body(x_ref, ng_ref, wqkv_ref, wgate_ref, wba_ref, conv_ref, alog_ref, dtb_ref,
                 q_ref, k_ref, v_ref, gate_ref, gb_ref, cbuf_ref, *, ts):
    j = pl.program_id(1)
    nch = 2 * GDN_QK + GDN_V

    @pl.when(j == 0)
    def _():
        cbuf_ref[0:CONV_HALO, :] = jnp.zeros((CONV_HALO, nch), F32)

    x = x_ref[0]
    h = _rms(x, ng_ref[...]).astype(BF16)
    pre = jnp.dot(h, wqkv_ref[...], preferred_element_type=F32)
    cbuf_ref[CONV_HALO:CONV_HALO + ts, :] = pre
    acc = pre * conv_ref[GDN_CONV - 1:GDN_CONV, :]
    for kk in range(GDN_CONV - 1):
        r0 = CONV_HALO - (GDN_CONV - 1) + kk
        acc = acc + cbuf_ref[r0:r0 + ts, :] * conv_ref[kk:kk + 1, :]
    cbuf_ref[0:CONV_HALO, :] = pre[ts - CONV_HALO:, :]
    a = _silu(acc)
    for hh in range(GDN_HEADS):
        c = slice(hh * GDN_DK, (hh + 1) * GDN_DK)
        qh = a[:, c]
        qn = qh * lax.rsqrt(jnp.sum(qh * qh, axis=-1, keepdims=True) + EPS) * (GDN_DK ** -0.5)
        q_ref[0, :, c] = qn.astype(BF16)
        kh = a[:, GDN_QK + hh * GDN_DK:GDN_QK + (hh + 1) * GDN_DK]
        kn = kh * lax.rsqrt(jnp.sum(kh * kh, axis=-1, keepdims=True) + EPS)
        k_ref[0, :, c] = kn.astype(BF16)
    v_ref[0] = a[:, 2 * GDN_QK:].astype(BF16)
    gate_ref[0] = _silu(jnp.dot(h, wgate_ref[...], preferred_element_type=F32)).astype(BF16)

    ba = jnp.dot(h, wba_ref[...], preferred_element_type=F32)
    lane = lax.broadcasted_iota(jnp.int32, (1, LANES), 1)
    z = ba + dtb_ref[...]
    softplus = jnp.maximum(z, 0.0) + jnp.log1p(jnp.exp(-jnp.abs(z)))
    g = jnp.where((lane >= GDN_HEADS) & (lane < 2 * GDN_HEADS),
                  -jnp.exp(alog_ref[...]) * softplus, 0.0)
    r = lax.broadcasted_iota(jnp.int32, (ts, 1), 0)
    c = lax.broadcasted_iota(jnp.int32, (1, ts), 1)
    tri = jnp.where(((r // GDN_CHUNK) == (c // GDN_CHUNK)) & (c <= r), 1.0, 0.0).astype(BF16)
    g_hi = g.astype(BF16)
    r1 = g - g_hi.astype(F32)
    g_mid = r1.astype(BF16)
    g_lo = (r1 - g_mid.astype(F32)).astype(BF16)
    gc = (jnp.dot(tri, g_hi, preferred_element_type=F32)
          + jnp.dot(tri, g_mid, preferred_element_type=F32)
          + jnp.dot(tri, g_lo, preferred_element_type=F32))
    gb_ref[0] = jnp.where(lane < GDN_HEADS, jax.nn.sigmoid(ba), gc)


def _gdn_core_body(q_ref, k_ref, v_ref, gb_ref, gbt_ref, o_ref, s_ref):
    j = pl.program_id(1)

    @pl.when(j == 0)
    def _():
        s_ref[...] = jnp.zeros_like(s_ref)

    n = GDN_UNIT
    r = lax.broadcasted_iota(jnp.int32, (n, 1), 0)
    c = lax.broadcasted_iota(jnp.int32, (1, n), 1)
    same = (r // GDN_CHUNK) == (c // GDN_CHUNK)
    causal = same & (c <= r)
    strict = same & (c < r)
    eye = jnp.where(r == c, 1.0, 0.0)
    gb = gb_ref[0]
    gbt = gbt_ref[0]
    zeros_c = jnp.zeros((GDN_CHUNK, GDN_DV), F32)
    hs = range(GDN_HEADS)
    beta_c = [gb[:, h:h + 1] for h in hs]
    gc_c = [gb[:, GDN_HEADS + h:GDN_HEADS + h + 1] for h in hs]
    gc_r = [gbt[GDN_HEADS + h:GDN_HEADS + h + 1, :] for h in hs]
    q = [q_ref[0, :, h * GDN_DK:(h + 1) * GDN_DK] for h in hs]
    k = [k_ref[0, :, h * GDN_DK:(h + 1) * GDN_DK] for h in hs]
    kf = [k[h].astype(F32) for h in hs]
    kb = [kf[h] * beta_c[h] for h in hs]
    decay = [jnp.exp(jnp.where(causal, gc_c[h] - gc_r[h], NEG)) for h in hs]
    kkqk = [_mm_nt(jnp.concatenate([kb[h].astype(BF16), q[h]], axis=0), k[h]) for h in hs]
    lmat = [jnp.where(strict, kkqk[h][:n] * decay[h], 0.0) for h in hs]
    attn = [kkqk[h][n:] * decay[h] for h in hs]
    xm = [eye - lmat[h] for h in hs]
    pw = [_mm(lmat[h], lmat[h]) for h in hs]
    for _ in range(4):
        res = [_mm(jnp.concatenate([xm[h], pw[h]], axis=0), pw[h]) for h in hs]
        xm = [xm[h] + res[h][:n] for h in hs]
        pw = [res[h][n:] for h in hs]
    res = [_mm(xm[h], pw[h]) for h in hs]
    xm = [xm[h] + res[h] for h in hs]
    eg = [jnp.exp(gc_c[h]) for h in hs]
    rhs = [jnp.concatenate([v_ref[0, :, h * GDN_DV:(h + 1) * GDN_DV].astype(F32) * beta_c[h],
                            kb[h] * eg[h]], axis=1) for h in hs]
    sol = [rhs[h] + _mm(xm[h] - eye, rhs[h]) for h in hs]
    qg = [q[h].astype(F32) * eg[h] for h in hs]
    k_t = [kf[h].T for h in hs]
    state = [s_ref[h] for h in hs]
    for cc in range(2):
        rows = slice(cc * GDN_CHUNK, (cc + 1) * GDN_CHUNK)
        in_chunk = (c // GDN_CHUNK) == cc
        g_last = [gc_r[h][:, (cc + 1) * GDN_CHUNK - 1:(cc + 1) * GDN_CHUNK] for h in hs]
        res = [_mm(jnp.concatenate([sol[h][rows, GDN_DV:], qg[h][rows]], axis=0), state[h])
               for h in hs]
        v_new = [sol[h][rows, :GDN_DV] - res[h][:GDN_CHUNK] for h in hs]
        v_all = [jnp.concatenate([v_new[h], zeros_c] if cc == 0 else [zeros_c, v_new[h]], axis=0)
                 for h in hs]
        o_c = [res[h][GDN_CHUNK:] + _mm(attn[h][rows], v_all[h]) for h in hs]
        k_dec_t = [k_t[h] * jnp.exp(jnp.where(in_chunk, g_last[h] - gc_r[h], NEG)) for h in hs]
        state = [state[h] * jnp.exp(g_last[h]) + _mm(k_dec_t[h], v_all[h]) for h in hs]
        for h in hs:
            o_ref[0, rows, h * GDN_DV:(h + 1) * GDN_DV] = o_c[h]
    for h in hs:
        s_ref[h] = state[h]


def _gdn_out_body(x_ref, o_ref, gate_ref, ng_ref, wout_ref, y_ref):
    o = o_ref[0]
    parts = []
    for h in range(GDN_HEADS):
        oh = o[:, h * GDN_DV:(h + 1) * GDN_DV]
        parts.append(_rms(oh, ng_ref[...]))
    on = jnp.concatenate(parts, axis=1) * gate_ref[0].astype(F32)
    y_ref[0] = x_ref[0] + jnp.dot(on.astype(BF16), wout_ref[...], preferred_element_type=F32)


def _gdn_layer(x, ng, w_in, conv_w, a_log, dt_bias, norm_g, w_out, *, ts=256):
    B, S, D = x.shape
    nch = 2 * GDN_QK + GDN_V
    w_qkv = w_in[:, :nch].astype(BF16)
    w_gate = w_in[:, nch:nch + GDN_V].astype(BF16)
    w_ba = jnp.pad(w_in[:, nch + GDN_V:], ((0, 0), (0, LANES - 2 * GDN_HEADS))).astype(BF16)
    pad_a = (GDN_HEADS, LANES - 2 * GDN_HEADS)
    alog = jnp.pad(a_log.astype(F32), pad_a).reshape(1, LANES)
    dtb = jnp.pad(dt_bias.astype(F32), pad_a).reshape(1, LANES)

    tok = lambda w: pl.BlockSpec((1, ts, w), lambda b, j: (b, j, 0))
    q, k, v, gate, gb = pl.pallas_call(
        functools.partial(_gdn_in_body, ts=ts),
        grid=(B, S // ts),
        in_specs=[tok(D), _const_spec((1, D)), _const_spec(w_qkv.shape), _const_spec(w_gate.shape),
                  _const_spec(w_ba.shape), _const_spec(conv_w.shape), _const_spec((1, LANES)),
                  _const_spec((1, LANES))],
        out_specs=[tok(GDN_QK), tok(GDN_QK), tok(GDN_V), tok(GDN_V), tok(LANES)],
        out_shape=[jax.ShapeDtypeStruct((B, S, GDN_QK), BF16),
                   jax.ShapeDtypeStruct((B, S, GDN_QK), BF16),
                   jax.ShapeDtypeStruct((B, S, GDN_V), BF16),
                   jax.ShapeDtypeStruct((B, S, GDN_V), BF16),
                   jax.ShapeDtypeStruct((B, S, LANES), F32)],
        scratch_shapes=[pltpu.VMEM((CONV_HALO + ts, nch), F32)],
        compiler_params=_params(2),
        name="gdn_in",
    )(x, ng.reshape(1, D), w_qkv, w_gate, w_ba, conv_w, alog, dtb)

    gbt = jnp.transpose(gb[:, :, :2 * GDN_HEADS], (0, 2, 1))
    n = GDN_UNIT
    utok = lambda w: pl.BlockSpec((1, n, w), lambda b, j: (b, j, 0))
    o = pl.pallas_call(
        _gdn_core_body,
        grid=(B, S // n),
        in_specs=[utok(GDN_QK), utok(GDN_QK), utok(GDN_V), utok(LANES),
                  pl.BlockSpec((1, 2 * GDN_HEADS, n), lambda b, j: (b, 0, j))],
        out_specs=utok(GDN_V),
        out_shape=jax.ShapeDtypeStruct((B, S, GDN_V), F32),
        scratch_shapes=[pltpu.VMEM((GDN_HEADS, GDN_DK, GDN_DV), F32)],
        compiler_params=_params(2),
        name="gdn_core",
    )(q, k, v, gb, gbt)

    return pl.pallas_call(
        _gdn_out_body,
        grid=(B, S // ts),
        in_specs=[tok(D), tok(GDN_V), tok(GDN_V), _const_spec((1, GDN_DV)),
                  _const_spec(w_out.shape)],
        out_specs=tok(D),
        out_shape=jax.ShapeDtypeStruct((B, S, D), F32),
        compiler_params=_params(2),
        name="gdn_out",
    )(x, o, gate, norm_g.reshape(1, GDN_DV), w_out.astype(BF16))


def _mla_in_body(x_ref, pos_ref, ng_ref, win_ref, qg_ref, wuq_ref, kvg_ref, wukv_ref, inv_ref,
                 q_ref, k_ref, v_ref, gate_ref):
    c1 = MLA_Q_LORA
    c2 = c1 + MLA_KV_LORA
    c3 = c2 + LANES
    x = x_ref[0]
    h = _rms(x, ng_ref[...]).astype(BF16)
    cq = jnp.dot(h, win_ref[:, :c1], preferred_element_type=F32)
    ckv = jnp.dot(h, win_ref[:, c1:c2], preferred_element_type=F32)
    kr = jnp.dot(h, win_ref[:, c2:c3], preferred_element_type=F32)
    gate = jnp.dot(h, win_ref[:, c3:], preferred_element_type=F32)
    gate_ref[0] = _silu(gate).astype(BF16)

    ang = pos_ref[0].astype(F32) * inv_ref[...]
    cos = jnp.cos(ang)
    sin = jnp.sin(ang)
    lane = lax.broadcasted_iota(jnp.int32, (1, LANES), 1)
    quarter = lane // (MLA_ROPE // 2)
    sin_s = sin * jnp.where((quarter == 0) | (quarter == 3), -1.0, 1.0)

    def rope(t):
        return t * cos + pltpu.roll(t, MLA_ROPE, 1) * sin_s

    k_rope = jnp.where(lane < MLA_ROPE, rope(kr), 0.0).astype(BF16)
    qf = jnp.dot(_rms(cq, qg_ref[...]).astype(BF16), wuq_ref[...],
                 preferred_element_type=F32) * (MLA_QK ** -0.5 * LOG2E)
    kv = jnp.dot(_rms(ckv, kvg_ref[...]).astype(BF16), wukv_ref[...], preferred_element_type=F32)
    for hh in range(MLA_HEADS):
        q0 = hh * MLA_QPAD
        q_ref[0, hh, :, 0:MLA_NOPE] = qf[:, q0:q0 + MLA_NOPE].astype(BF16)
        q_ref[0, hh, :, MLA_NOPE:] = rope(qf[:, q0 + MLA_NOPE:q0 + MLA_QPAD]).astype(BF16)
        k0 = hh * (MLA_NOPE + MLA_V)
        k_ref[0, hh, :, 0:MLA_NOPE] = kv[:, k0:k0 + MLA_NOPE].astype(BF16)
        k_ref[0, hh, :, MLA_NOPE:] = k_rope
        v_ref[0, hh] = kv[:, k0 + MLA_NOPE:k0 + MLA_NOPE + MLA_V].astype(BF16)


def _attn_body(q_ref, k_ref, v_ref, o_ref, s_ref, p_ref, m_ref, l_ref, a_ref, acc_ref, *, tq, tk):
    i = pl.program_id(2)
    nh = q_ref.shape[1]
    m_ref[...] = jnp.full(m_ref.shape, NEG, F32)
    l_ref[...] = jnp.zeros(l_ref.shape, F32)
    acc_ref[...] = jnp.zeros(acc_ref.shape, F32)

    def step(kb, masked):
        off = pl.multiple_of(kb * tk, tk)
        for h in range(nh):
            s_ref[h] = lax.dot_general(q_ref[0, h], k_ref[0, h, pl.ds(off, tk), :],
                                       (((1,), (1,)), ((), ())), preferred_element_type=F32)
        for h in range(nh):
            for r0 in range(0, tq, ATTN_STRIP):
                rs = slice(r0, r0 + ATTN_STRIP)
                s = s_ref[h, rs, :]
                if masked:
                    row = i * tq + r0 + lax.broadcasted_iota(jnp.int32, (ATTN_STRIP, 1), 0)
                    col = kb * tk + lax.broadcasted_iota(jnp.int32, (1, tk), 1)
                    s = jnp.where(col <= row, s, NEG)
                m_old = m_ref[h, rs, :]
                m_new = jnp.maximum(m_old, jnp.max(s, axis=-1, keepdims=True))
                alpha = jnp.exp2(m_old - m_new)
                ps = [jnp.exp2(s[:, c0:c0 + LANES] - m_new) for c0 in range(0, tk, LANES)]
                psum = ps[0]
                for pc in ps[1:]:
                    psum = psum + pc
                l_ref[h, rs, :] = alpha * l_ref[h, rs, :] + jnp.sum(psum, axis=-1, keepdims=True)
                m_ref[h, rs, :] = m_new
                a_ref[h, rs, :] = alpha
                p_ref[h, rs, :] = jnp.concatenate(ps, axis=1).astype(BF16)
        for h in range(nh):
            acc_ref[h] = a_ref[h] * acc_ref[h] + jnp.dot(
                p_ref[h], v_ref[0, h, pl.ds(off, tk), :], preferred_element_type=F32)

    n_full = i * (tq // tk)

    def full_step(kb, carry):
        step(kb, False)
        return carry

    lax.fori_loop(0, n_full, full_step, 0)
    for d in range(tq // tk):
        step(n_full + d, True)
    for h in range(nh):
        o_ref[0, :, h * MLA_V:(h + 1) * MLA_V] = acc_ref[h] / l_ref[h]


def _gated_out_body(x_ref, o_ref, gate_ref, wout_ref, y_ref):
    y = o_ref[0] * gate_ref[0].astype(F32)
    y_ref[0] = x_ref[0] + jnp.dot(y.astype(BF16), wout_ref[...], preferred_element_type=F32)


def _mla_layer(x, pos, ng, w_in, q_norm_g, w_uq, kv_norm_g, w_ukv, w_out, *, ts=256, tq=512, tk=512, nh=2):
    B, S, D = x.shape
    H = MLA_HEADS
    half = MLA_ROPE // 2
    c1 = MLA_Q_LORA
    c2 = c1 + MLA_KV_LORA
    c3 = c2 + MLA_ROPE
    w_kr = w_in[:, c2:c3]
    w_kr = jnp.concatenate([w_kr, w_kr[:, half:], w_kr[:, :half]], axis=1)
    w_in_p = jnp.concatenate([w_in[:, :c2], w_kr, w_in[:, c3:]], axis=1).astype(BF16)
    wq = w_uq.reshape(MLA_Q_LORA, H, MLA_QK)
    wq_r = wq[:, :, MLA_NOPE:]
    wq_p = jnp.concatenate([wq, wq_r[:, :, half:], wq_r[:, :, :half]], axis=2)
    wq_p = wq_p.reshape(MLA_Q_LORA, H * MLA_QPAD).astype(BF16)
    inv = ROPE_THETA ** (-jnp.arange(half, dtype=F32) / half)
    inv = jnp.tile(inv, LANES // half).reshape(1, LANES)
    width = H * MLA_V

    tok = lambda w: pl.BlockSpec((1, ts, w), lambda b, j: (b, j, 0))
    head = lambda w: pl.BlockSpec((1, H, ts, w), lambda b, j: (b, 0, j, 0))
    q, k, v, gate = pl.pallas_call(
        _mla_in_body,
        grid=(B, S // ts),
        in_specs=[tok(D), tok(1), _const_spec((1, D)), _const_spec(w_in_p.shape),
                  _const_spec((1, MLA_Q_LORA)), _const_spec(wq_p.shape),
                  _const_spec((1, MLA_KV_LORA)), _const_spec(w_ukv.shape), _const_spec((1, LANES))],
        out_specs=[head(MLA_QPAD), head(MLA_QPAD), head(MLA_V), tok(width)],
        out_shape=[jax.ShapeDtypeStruct((B, H, S, MLA_QPAD), BF16),
                   jax.ShapeDtypeStruct((B, H, S, MLA_QPAD), BF16),
                   jax.ShapeDtypeStruct((B, H, S, MLA_V), BF16),
                   jax.ShapeDtypeStruct((B, S, width), BF16)],
        compiler_params=_params(2),
        name="mla_in",
    )(x, pos.reshape(B, S, 1), ng.reshape(1, D), w_in_p, q_norm_g.reshape(1, MLA_Q_LORA), wq_p,
      kv_norm_g.reshape(1, MLA_KV_LORA), w_ukv.astype(BF16), inv)

    o = pl.pallas_call(
        functools.partial(_attn_body, tq=tq, tk=tk),
        grid=(B, H // nh, S // tq),
        in_specs=[pl.BlockSpec((1, nh, tq, MLA_QPAD), lambda b, h, i: (b, h, i, 0)),
                  pl.BlockSpec((1, nh, S, MLA_QPAD), lambda b, h, i: (b, h, 0, 0)),
                  pl.BlockSpec((1, nh, S, MLA_V), lambda b, h, i: (b, h, 0, 0))],
        out_specs=pl.BlockSpec((1, tq, nh * MLA_V), lambda b, h, i: (b, i, h)),
        out_shape=jax.ShapeDtypeStruct((B, S, width), F32),
        scratch_shapes=[pltpu.VMEM((nh, tq, tk), F32), pltpu.VMEM((nh, tq, tk), BF16),
                        pltpu.VMEM((nh, tq, LANES), F32), pltpu.VMEM((nh, tq, LANES), F32),
                        pltpu.VMEM((nh, tq, LANES), F32), pltpu.VMEM((nh, tq, MLA_V), F32)],
        compiler_params=_params(3),
        name="mla_attn",
    )(q, k, v)

    return pl.pallas_call(
        _gated_out_body,
        grid=(B, S // ts),
        in_specs=[tok(D), tok(width), tok(width), _const_spec(w_out.shape)],
        out_specs=tok(D),
        out_shape=jax.ShapeDtypeStruct((B, S, D), F32),
        compiler_params=_params(2),
        name="mla_out",
    )(x, o, gate, w_out.astype(BF16))


def kernel(x, positions, norm_g, pool_w_in, pool_w_grp, pool_scale, pool_w_out, gdn_w_in, gdn_conv, gdn_a_log, gdn_dt_bias, gdn_norm_g, gdn_w_out, mla_w_in, mla_q_norm_g, mla_w_uq, mla_kv_norm_g, mla_w_ukv, mla_w_out, final_g):
    depth = norm_g.shape[0]
    for i in range(depth):
        kind, j = i % N_MIXERS, i // N_MIXERS
        last = i == depth - 1
        if kind == 0:
            x = _pool_layer(x, norm_g[i], pool_w_in[j], pool_w_grp[j], pool_scale[j], pool_w_out[j],
                            final_g if last else None)
        elif kind == 1:
            x = _gdn_layer(x, norm_g[i], gdn_w_in[j], gdn_conv[j], gdn_a_log[j], gdn_dt_bias[j],
                           gdn_norm_g[j], gdn_w_out[j])
        else:
            x = _mla_layer(x, positions, norm_g[i], mla_w_in[j], mla_q_norm_g[j], mla_w_uq[j],
                           mla_kv_norm_g[j], mla_w_ukv[j], mla_w_out[j])
        if last and kind != 0:
            x = _final_norm(x, final_g)
    return x


def _final_norm_body(x_ref, g_ref, o_ref):
    o_ref[0] = _rms(x_ref[0], g_ref[...])


def _final_norm(x, g, *, ts=512):
    B, S, D = x.shape
    tok = pl.BlockSpec((1, ts, D), lambda b, j: (b, j, 0))
    return pl.pallas_call(
        _final_norm_body, grid=(B, S // ts), in_specs=[tok, _const_spec((1, D))], out_specs=tok,
        out_shape=jax.ShapeDtypeStruct((B, S, D), F32), compiler_params=_params(2),
        name="final_norm",
    )(x, g.reshape(1, D))
```

```python
import functools

import jax
import jax.numpy as jnp
from jax import lax
from jax.experimental import pallas as pl
from jax.experimental.pallas import tpu as pltpu

F32 = jnp.float32
BF16 = jnp.bfloat16
EPS = 1e-6
NEG = -1e30
VMEM_LIMIT_V7X = 56 * 2 ** 20
LANES = 128

N_MIXERS = 3

POOL_WINDOWS = (2, 4, 8, 16)
POOL_HALO = 16

GDN_HEADS = 8
GDN_DK = 128
GDN_DV = 256
GDN_CONV = 4
GDN_CHUNK = 64
GDN_QK = GDN_HEADS * GDN_DK
GDN_V = GDN_HEADS * GDN_DV
GDN_UNIT = 2 * GDN_CHUNK
CONV_HALO = 8

MLA_HEADS = 16
MLA_NOPE = 128
MLA_ROPE = 64
MLA_V = 128
MLA_Q_LORA = 768
MLA_KV_LORA = 512
MLA_QK = MLA_NOPE + MLA_ROPE
MLA_QPAD = 256
ROPE_THETA = 10000.0
ATTN_STRIP = 32
LOG2E = 1.4426950408889634


def _const_spec(shape):
    zeros = (0,) * len(shape)
    return pl.BlockSpec(shape, lambda *_: zeros, pipeline_mode=pl.Buffered(1))


def _params(n_axes):
    return pltpu.CompilerParams(dimension_semantics=("arbitrary",) * n_axes,
                                vmem_limit_bytes=VMEM_LIMIT_V7X)


def _rms(x, g):
    return x * lax.rsqrt(jnp.mean(x * x, axis=-1, keepdims=True) + EPS) * g


def _silu(x):
    h = 0.5 * x
    return h + h * jnp.tanh(h)


def _mm(a, b):
    return jnp.dot(a.astype(BF16), b.astype(BF16), preferred_element_type=F32)


def _mm_nt(a, b):
    return lax.dot_general(a.astype(BF16), b.astype(BF16), (((1,), (1,)), ((), ())),
                           preferred_element_type=F32)


def _pool_body(x_ref, ng_ref, win_ref, wgrp_ref, sc_ref, wout_ref, fg_ref, o_ref, ubuf_ref,
               *, ts, width, final):
    j = pl.program_id(1)
    grp = width // len(POOL_WINDOWS)

    @pl.when(j == 0)
    def _():
        ubuf_ref[0:POOL_HALO, :] = jnp.zeros((POOL_HALO, width), F32)

    x = x_ref[0]
    h = _rms(x, ng_ref[...]).astype(BF16)
    u = jnp.dot(h, win_ref[:, :width], preferred_element_type=F32)
    gate = jnp.dot(h, win_ref[:, width:], preferred_element_type=F32)
    ubuf_ref[POOL_HALO:POOL_HALO + ts, :] = u
    t = j * ts + lax.broadcasted_iota(jnp.int32, (ts, 1), 0)
    parts = []
    for gi, w in enumerate(POOL_WINDOWS):
        c0 = gi * grp
        ug = u[:, c0:c0 + grp]
        s = ug
        for d in range(1, w):
            s = s + ubuf_ref[POOL_HALO - d:POOL_HALO - d + ts, c0:c0 + grp]
        cnt = jnp.minimum(t + 1, w).astype(F32)
        p = s / cnt - ug
        parts.append(jnp.dot(p.astype(BF16), wgrp_ref[gi], preferred_element_type=F32))
    pg = jnp.concatenate(parts, axis=1)
    y = pg * sc_ref[...] * _silu(gate)
    out = x + jnp.dot(y.astype(BF16), wout_ref[...], preferred_element_type=F32)
    ubuf_ref[0:POOL_HALO, :] = u[ts - POOL_HALO:, :]
    if final:
        out = _rms(out, fg_ref[...])
    o_ref[0] = out


def _pool_layer(x, ng, w_in, w_grp, scale, w_out, final_g, *, ts=512):
    B, S, D = x.shape
    width = w_out.shape[0]
    final = final_g is not None
    fg = final_g if final else jnp.ones((D,), F32)
    body = functools.partial(_pool_body, ts=ts, width=width, final=final)
    return pl.pallas_call(
        body,
        grid=(B, S // ts),
        in_specs=[
            pl.BlockSpec((1, ts, D), lambda b, j: (b, j, 0)),
            _const_spec((1, D)),
            _const_spec(w_in.shape),
            _const_spec(w_grp.shape),
            _const_spec((1, width)),
            _const_spec(w_out.shape),
            _const_spec((1, D)),
        ],
        out_specs=pl.BlockSpec((1, ts, D), lambda b, j: (b, j, 0)),
        out_shape=jax.ShapeDtypeStruct((B, S, D), F32),
        scratch_shapes=[pltpu.VMEM((POOL_HALO + ts, width), F32)],
        compiler_params=_params(2),
        name="pool_layer",
    )(x, ng.reshape(1, D), w_in.astype(BF16), w_grp.astype(BF16), scale.reshape(1, width),
      w_out.astype(BF16), fg.reshape(1, D))


def _gdn_in_body(x_ref, ng_ref, wqkv_ref, wgate_ref, wba_ref, conv_ref, alog_ref, dtb_ref,
                 q_ref, k_ref, v_ref, gate_ref, gb_ref, cbuf_ref, *, ts):
    j = pl.program_id(1)
    nch = 2 * GDN_QK + GDN_V

    @pl.when(j == 0)
    def _():
        cbuf_ref[0:CONV_HALO, :] = jnp.zeros((CONV_HALO, nch), F32)

    x = x_ref[0]
    h = _rms(x, ng_ref[...]).astype(BF16)
    pre = jnp.dot(h, wqkv_ref[...], preferred_element_type=F32)
    cbuf_ref[CONV_HALO:CONV_HALO + ts, :] = pre
    acc = pre * conv_ref[GDN_CONV - 1:GDN_CONV, :]
    for kk in range(GDN_CONV - 1):
        r0 = CONV_HALO - (GDN_CONV - 1) + kk
        acc = acc + cbuf_ref[r0:r0 + ts, :] * conv_ref[kk:kk + 1, :]
    cbuf_ref[0:CONV_HALO, :] = pre[ts - CONV_HALO:, :]
    a = _silu(acc)
    for hh in range(GDN_HEADS):
        c = slice(hh * GDN_DK, (hh + 1) * GDN_DK)
        qh = a[:, c]
        qn = qh * lax.rsqrt(jnp.sum(qh * qh, axis=-1, keepdims=True) + EPS) * (GDN_DK ** -0.5)
        q_ref[0, :, c] = qn.astype(BF16)
        kh = a[:, GDN_QK + hh * GDN_DK:GDN_QK + (hh + 1) * GDN_DK]
        kn = kh * lax.rsqrt(jnp.sum(kh * kh, axis=-1, keepdims=True) + EPS)
        k_ref[0, :, c] = kn.astype(BF16)
    v_ref[0] = a[:, 2 * GDN_QK:].astype(BF16)
    gate_ref[0] = _silu(jnp.dot(h, wgate_ref[...], preferred_element_type=F32)).astype(BF16)

    ba = jnp.dot(h, wba_ref[...], preferred_element_type=F32)
    lane = lax.broadcasted_iota(jnp.int32, (1, LANES), 1)
    z = ba + dtb_ref[...]
    softplus = jnp.maximum(z, 0.0) + jnp.log1p(jnp.exp(-jnp.abs(z)))
    g = jnp.where((lane >= GDN_HEADS) & (lane < 2 * GDN_HEADS),
                  -jnp.exp(alog_ref[...]) * softplus, 0.0)
    r = lax.broadcasted_iota(jnp.int32, (ts, 1), 0)
    c = lax.broadcasted_iota(jnp.int32, (1, ts), 1)
    tri = jnp.where(((r // GDN_CHUNK) == (c // GDN_CHUNK)) & (c <= r), 1.0, 0.0).astype(BF16)
    g_hi = g.astype(BF16)
    r1 = g - g_hi.astype(F32)
    g_mid = r1.astype(BF16)
    g_lo = (r1 - g_mid.astype(F32)).astype(BF16)
    gc = (jnp.dot(tri, g_hi, preferred_element_type=F32)
          + jnp.dot(tri, g_mid, preferred_element_type=F32)
          + jnp.dot(tri, g_lo, preferred_element_type=F32))
    gb_ref[0] = jnp.where(lane < GDN_HEADS, jax.nn.sigmoid(ba), gc)


def _gdn_core_body(q_ref, k_ref, v_ref, gb_ref, gbt_ref, o_ref, s_ref):
    j = pl.program_id(1)

    @pl.when(j == 0)
    def _():
        s_ref[...] = jnp.zeros_like(s_ref)

    n = GDN_UNIT
    r = lax.broadcasted_iota(jnp.int32, (n, 1), 0)
    c = lax.broadcasted_iota(jnp.int32, (1, n), 1)
    same = (r // GDN_CHUNK) == (c // GDN_CHUNK)
    causal = same & (c <= r)
    strict = same & (c < r)
    eye = jnp.where(r == c, 1.0, 0.0)
    gb = gb_ref[0]
    gbt = gbt_ref[0]
    zeros_c = jnp.zeros((GDN_CHUNK, GDN_DV), F32)
    hs = range(GDN_HEADS)
    beta_c = [gb[:, h:h + 1] for h in hs]
    gc_c = [gb[:, GDN_HEADS + h:GDN_HEADS + h + 1] for h in hs]
    gc_r = [gbt[GDN_HEADS + h:GDN_HEADS + h + 1, :] for h in hs]
    q = [q_ref[0, :, h * GDN_DK:(h + 1) * GDN_DK] for h in hs]
    k = [k_ref[0, :, h * GDN_DK:(h + 1) * GDN_DK] for h in hs]
    kf = [k[h].astype(F32) for h in hs]
    kb = [kf[h] * beta_c[h] for h in hs]
    decay = [jnp.exp(jnp.where(causal, gc_c[h] - gc_r[h], NEG)) for h in hs]
    kkqk = [_mm_nt(jnp.concatenate([kb[h].astype(BF16), q[h]], axis=0), k[h]) for h in hs]
    lmat = [jnp.where(strict, kkqk[h][:n] * decay[h], 0.0) for h in hs]
    attn = [kkqk[h][n:] * decay[h] for h in hs]
    xm = [eye - lmat[h] for h in hs]
    pw = [_mm(lmat[h], lmat[h]) for h in hs]
    for _ in range(4):
        res = [_mm(jnp.concatenate([xm[h], pw[h]], axis=0), pw[h]) for h in hs]
        xm = [xm[h] + res[h][:n] for h in hs]
        pw = [res[h][n:] for h in hs]
    res = [_mm(xm[h], pw[h]) for h in hs]
    xm = [xm[h] + res[h] for h in hs]
    eg = [jnp.exp(gc_c[h]) for h in hs]
    rhs = [jnp.concatenate([v_ref[0, :, h * GDN_DV:(h + 1) * GDN_DV].astype(F32) * beta_c[h],
                            kb[h] * eg[h]], axis=1) for h in hs]
    sol = [rhs[h] + _mm(xm[h] - eye, rhs[h]) for h in hs]
    qg = [q[h].astype(F32) * eg[h] for h in hs]
    k_t = [kf[h].T for h in hs]
    state = [s_ref[h] for h in hs]
    for cc in range(2):
        rows = slice(cc * GDN_CHUNK, (cc + 1) * GDN_CHUNK)
        in_chunk = (c // GDN_CHUNK) == cc
        g_last = [gc_r[h][:, (cc + 1) * GDN_CHUNK - 1:(cc + 1) * GDN_CHUNK] for h in hs]
        res = [_mm(jnp.concatenate([sol[h][rows, GDN_DV:], qg[h][rows]], axis=0), state[h])
               for h in hs]
        v_new = [sol[h][rows, :GDN_DV] - res[h][:GDN_CHUNK] for h in hs]
        v_all = [jnp.concatenate([v_new[h], zeros_c] if cc == 0 else [zeros_c, v_new[h]], axis=0)
                 for h in hs]
        o_c = [res[h][GDN_CHUNK:] + _mm(attn[h][rows], v_all[h]) for h in hs]
        k_dec_t = [k_t[h] * jnp.exp(jnp.where(in_chunk, g_last[h] - gc_r[h], NEG)) for h in hs]
        state = [state[h] * jnp.exp(g_last[h]) + _mm(k_dec_t[h], v_all[h]) for h in hs]
        for h in hs:
            o_ref[0, rows, h * GDN_DV:(h + 1) * GDN_DV] = o_c[h].astype(o_ref.dtype)
    for h in hs:
        s_ref[h] = state[h]


def _gdn_out_body(x_ref, o_ref, gate_ref, ng_ref, wout_ref, y_ref):
    o = o_ref[0].astype(F32)
    parts = []
    for h in range(GDN_HEADS):
        oh = o[:, h * GDN_DV:(h + 1) * GDN_DV]
        parts.append(_rms(oh, ng_ref[...]))
    on = jnp.concatenate(parts, axis=1) * gate_ref[0].astype(F32)
    y_ref[0] = x_ref[0] + jnp.dot(on.astype(BF16), wout_ref[...], preferred_element_type=F32)


def _gdn_layer(x, ng, w_in, conv_w, a_log, dt_bias, norm_g, w_out, *, ts=256):
    B, S, D = x.shape
    nch = 2 * GDN_QK + GDN_V
    w_qkv = w_in[:, :nch].astype(BF16)
    w_gate = w_in[:, nch:nch + GDN_V].astype(BF16)
    w_ba = jnp.pad(w_in[:, nch + GDN_V:], ((0, 0), (0, LANES - 2 * GDN_HEADS))).astype(BF16)
    pad_a = (GDN_HEADS, LANES - 2 * GDN_HEADS)
    alog = jnp.pad(a_log.astype(F32), pad_a).reshape(1, LANES)
    dtb = jnp.pad(dt_bias.astype(F32), pad_a).reshape(1, LANES)

    tok = lambda w: pl.BlockSpec((1, ts, w), lambda b, j: (b, j, 0))
    q, k, v, gate, gb = pl.pallas_call(
        functools.partial(_gdn_in_body, ts=ts),
        grid=(B, S // ts),
        in_specs=[tok(D), _const_spec((1, D)), _const_spec(w_qkv.shape), _const_spec(w_gate.shape),
                  _const_spec(w_ba.shape), _const_spec(conv_w.shape), _const_spec((1, LANES)),
                  _const_spec((1, LANES))],
        out_specs=[tok(GDN_QK), tok(GDN_QK), tok(GDN_V), tok(GDN_V), tok(LANES)],
        out_shape=[jax.ShapeDtypeStruct((B, S, GDN_QK), BF16),
                   jax.ShapeDtypeStruct((B, S, GDN_QK), BF16),
                   jax.ShapeDtypeStruct((B, S, GDN_V), BF16),
                   jax.ShapeDtypeStruct((B, S, GDN_V), BF16),
                   jax.ShapeDtypeStruct((B, S, LANES), F32)],
        scratch_shapes=[pltpu.VMEM((CONV_HALO + ts, nch), F32)],
        compiler_params=_params(2),
        name="gdn_in",
    )(x, ng.reshape(1, D), w_qkv, w_gate, w_ba, conv_w, alog, dtb)

    gbt = jnp.transpose(gb[:, :, :2 * GDN_HEADS], (0, 2, 1))
    n = GDN_UNIT
    utok = lambda w: pl.BlockSpec((1, n, w), lambda b, j: (b, j, 0))
    o = pl.pallas_call(
        _gdn_core_body,
        grid=(B, S // n),
        in_specs=[utok(GDN_QK), utok(GDN_QK), utok(GDN_V), utok(LANES),
                  pl.BlockSpec((1, 2 * GDN_HEADS, n), lambda b, j: (b, 0, j))],
        out_specs=utok(GDN_V),
        out_shape=jax.ShapeDtypeStruct((B, S, GDN_V), BF16),
        scratch_shapes=[pltpu.VMEM((GDN_HEADS, GDN_DK, GDN_DV), F32)],
        compiler_params=_params(2),
        name="gdn_core",
    )(q, k, v, gb, gbt)

    return pl.pallas_call(
        _gdn_out_body,
        grid=(B, S // ts),
        in_specs=[tok(D), tok(GDN_V), tok(GDN_V), _const_spec((1, GDN_DV)),
                  _const_spec(w_out.shape)],
        out_specs=tok(D),
        out_shape=jax.ShapeDtypeStruct((B, S, D), F32),
        compiler_params=_params(2),
        name="gdn_out",
    )(x, o, gate, norm_g.reshape(1, GDN_DV), w_out.astype(BF16))


def _mla_in_body(x_ref, pos_ref, ng_ref, win_ref, qg_ref, wuq_ref, kvg_ref, wukv_ref, inv_ref,
                 q_ref, k_ref, v_ref, gate_ref):
    c1 = MLA_Q_LORA
    c2 = c1 + MLA_KV_LORA
    c3 = c2 + LANES
    x = x_ref[0]
    h = _rms(x, ng_ref[...]).astype(BF16)
    cq = jnp.dot(h, win_ref[:, :c1], preferred_element_type=F32)
    ckv = jnp.dot(h, win_ref[:, c1:c2], preferred_element_type=F32)
    kr = jnp.dot(h, win_ref[:, c2:c3], preferred_element_type=F32)
    gate = jnp.dot(h, win_ref[:, c3:], preferred_element_type=F32)
    gate_ref[0] = _silu(gate).astype(BF16)

    ang = pos_ref[0].astype(F32) * inv_ref[...]
    cos = jnp.cos(ang)
    sin = jnp.sin(ang)
    lane = lax.broadcasted_iota(jnp.int32, (1, LANES), 1)
    quarter = lane // (MLA_ROPE // 2)
    sin_s = sin * jnp.where((quarter == 0) | (quarter == 3), -1.0, 1.0)

    def rope(t):
        return t * cos + pltpu.roll(t, MLA_ROPE, 1) * sin_s

    k_rope = jnp.where(lane < MLA_ROPE, rope(kr), 0.0).astype(BF16)
    qf = jnp.dot(_rms(cq, qg_ref[...]).astype(BF16), wuq_ref[...],
                 preferred_element_type=F32) * (MLA_QK ** -0.5 * LOG2E)
    kv = jnp.dot(_rms(ckv, kvg_ref[...]).astype(BF16), wukv_ref[...], preferred_element_type=F32)
    for hh in range(MLA_HEADS):
        q0 = hh * MLA_QPAD
        q_ref[0, hh, :, 0:MLA_NOPE] = qf[:, q0:q0 + MLA_NOPE].astype(BF16)
        q_ref[0, hh, :, MLA_NOPE:] = rope(qf[:, q0 + MLA_NOPE:q0 + MLA_QPAD]).astype(BF16)
        k0 = hh * (MLA_NOPE + MLA_V)
        k_ref[0, hh, :, 0:MLA_NOPE] = kv[:, k0:k0 + MLA_NOPE].astype(BF16)
        k_ref[0, hh, :, MLA_NOPE:] = k_rope
        v_ref[0, hh] = kv[:, k0 + MLA_NOPE:k0 + MLA_NOPE + MLA_V].astype(BF16)


def _attn_body(q_ref, k_ref, v_ref, o_ref, s_ref, p_ref, m_ref, a_ref, acc_ref, vx_ref, *, tq, tk):
    i = pl.program_id(2)
    nh = q_ref.shape[1]

    @pl.when(i == 0)
    def _():
        for h in range(nh):
            vx_ref[h, :, :MLA_V] = v_ref[0, h]
            vx_ref[h, :, MLA_V:] = jnp.ones((vx_ref.shape[1], LANES), BF16)

    m_ref[...] = jnp.full(m_ref.shape, NEG, F32)
    acc_ref[...] = jnp.zeros(acc_ref.shape, F32)

    def step(c0, width, masked):
        for h in range(nh):
            s_ref[h, :, :width] = lax.dot_general(
                q_ref[0, h], k_ref[0, h, pl.ds(c0, width), :], (((1,), (1,)), ((), ())),
                preferred_element_type=F32)
        for h in range(nh):
            for r0 in range(0, tq, ATTN_STRIP):
                rs = slice(r0, r0 + ATTN_STRIP)
                s = s_ref[h, rs, :width]
                if masked:
                    row = i * tq + r0 + lax.broadcasted_iota(jnp.int32, (ATTN_STRIP, 1), 0)
                    col = c0 + lax.broadcasted_iota(jnp.int32, (1, width), 1)
                    s = jnp.where(col <= row, s, NEG)
                m_old = m_ref[h, rs, :]
                m_new = jnp.maximum(m_old, jnp.max(s, axis=-1, keepdims=True))
                a_ref[h, rs, :] = jnp.exp2(m_old - m_new)
                m_ref[h, rs, :] = m_new
                p_ref[h, rs, :width] = jnp.concatenate(
                    [jnp.exp2(s[:, c:c + LANES] - m_new) for c in range(0, width, LANES)],
                    axis=1).astype(BF16)
        for h in range(nh):
            pv = jnp.dot(p_ref[h, :, :width], vx_ref[h, pl.ds(c0, width), :],
                         preferred_element_type=F32)
            a = a_ref[h]
            acc_ref[h, :, :MLA_V] = a * acc_ref[h, :, :MLA_V] + pv[:, :MLA_V]
            acc_ref[h, :, MLA_V:] = a * acc_ref[h, :, MLA_V:] + pv[:, MLA_V:]

    n_wide = (i * tq) // tk

    def wide_step(kb, carry):
        step(pl.multiple_of(kb * tk, tk), tk, False)
        return carry

    lax.fori_loop(0, n_wide, wide_step, 0)
    for d in range(tk // tq - 1):
        @pl.when(n_wide * tk + d * tq < i * tq)
        def _():
            step(pl.multiple_of(n_wide * tk + d * tq, tq), tq, False)
    step(pl.multiple_of(i * tq, tq), tq, True)
    for h in range(nh):
        o_ref[0, :, h * MLA_V:(h + 1) * MLA_V] = (
            acc_ref[h, :, :MLA_V] / acc_ref[h, :, MLA_V:]).astype(o_ref.dtype)


def _gated_out_body(x_ref, o_ref, gate_ref, wout_ref, y_ref):
    y = o_ref[0].astype(F32) * gate_ref[0].astype(F32)
    y_ref[0] = x_ref[0] + jnp.dot(y.astype(BF16), wout_ref[...], preferred_element_type=F32)


def _mla_layer(x, pos, ng, w_in, q_norm_g, w_uq, kv_norm_g, w_ukv, w_out, *, ts=256, tq=512, tk=1024, nh=4):
    B, S, D = x.shape
    H = MLA_HEADS
    half = MLA_ROPE // 2
    c1 = MLA_Q_LORA
    c2 = c1 + MLA_KV_LORA
    c3 = c2 + MLA_ROPE
    w_kr = w_in[:, c2:c3]
    w_kr = jnp.concatenate([w_kr, w_kr[:, half:], w_kr[:, :half]], axis=1)
    w_in_p = jnp.concatenate([w_in[:, :c2], w_kr, w_in[:, c3:]], axis=1).astype(BF16)
    wq = w_uq.reshape(MLA_Q_LORA, H, MLA_QK)
    wq_r = wq[:, :, MLA_NOPE:]
    wq_p = jnp.concatenate([wq, wq_r[:, :, half:], wq_r[:, :, :half]], axis=2)
    wq_p = wq_p.reshape(MLA_Q_LORA, H * MLA_QPAD).astype(BF16)
    inv = ROPE_THETA ** (-jnp.arange(half, dtype=F32) / half)
    inv = jnp.tile(inv, LANES // half).reshape(1, LANES)
    width = H * MLA_V

    tok = lambda w: pl.BlockSpec((1, ts, w), lambda b, j: (b, j, 0))
    head = lambda w: pl.BlockSpec((1, H, ts, w), lambda b, j: (b, 0, j, 0))
    q, k, v, gate = pl.pallas_call(
        _mla_in_body,
        grid=(B, S // ts),
        in_specs=[tok(D), tok(1), _const_spec((1, D)), _const_spec(w_in_p.shape),
                  _const_spec((1, MLA_Q_LORA)), _const_spec(wq_p.shape),
                  _const_spec((1, MLA_KV_LORA)), _const_spec(w_ukv.shape), _const_spec((1, LANES))],
        out_specs=[head(MLA_QPAD), head(MLA_QPAD), head(MLA_V), tok(width)],
        out_shape=[jax.ShapeDtypeStruct((B, H, S, MLA_QPAD), BF16),
                   jax.ShapeDtypeStruct((B, H, S, MLA_QPAD), BF16),
                   jax.ShapeDtypeStruct((B, H, S, MLA_V), BF16),
                   jax.ShapeDtypeStruct((B, S, width), BF16)],
        compiler_params=_params(2),
        name="mla_in",
    )(x, pos.reshape(B, S, 1), ng.reshape(1, D), w_in_p, q_norm_g.reshape(1, MLA_Q_LORA), wq_p,
      kv_norm_g.reshape(1, MLA_KV_LORA), w_ukv.astype(BF16), inv)

    o = pl.pallas_call(
        functools.partial(_attn_body, tq=tq, tk=tk),
        grid=(B, H // nh, S // tq),
        in_specs=[pl.BlockSpec((1, nh, tq, MLA_QPAD), lambda b, h, i: (b, h, i, 0)),
                  pl.BlockSpec((1, nh, S, MLA_QPAD), lambda b, h, i: (b, h, 0, 0),
                               pipeline_mode=pl.Buffered(1)),
                  pl.BlockSpec((1, nh, S, MLA_V), lambda b, h, i: (b, h, 0, 0),
                               pipeline_mode=pl.Buffered(1))],
        out_specs=pl.BlockSpec((1, tq, nh * MLA_V), lambda b, h, i: (b, i, h)),
        out_shape=jax.ShapeDtypeStruct((B, S, width), BF16),
        scratch_shapes=[pltpu.VMEM((nh, tq, tk), F32), pltpu.VMEM((nh, tq, tk), BF16),
                        pltpu.VMEM((nh, tq, LANES), F32), pltpu.VMEM((nh, tq, LANES), F32),
                        pltpu.VMEM((nh, tq, MLA_V + LANES), F32),
                        pltpu.VMEM((nh, S, MLA_V + LANES), BF16)],
        compiler_params=_params(3),
        name="mla_attn",
    )(q, k, v)

    return pl.pallas_call(
        _gated_out_body,
        grid=(B, S // ts),
        in_specs=[tok(D), tok(width), tok(width), _const_spec(w_out.shape)],
        out_specs=tok(D),
        out_shape=jax.ShapeDtypeStruct((B, S, D), F32),
        compiler_params=_params(2),
        name="mla_out",
    )(x, o, gate, w_out.astype(BF16))


def kernel(x, positions, norm_g, pool_w_in, pool_w_grp, pool_scale, pool_w_out, gdn_w_in, gdn_conv, gdn_a_log, gdn_dt_bias, gdn_norm_g, gdn_w_out, mla_w_in, mla_q_norm_g, mla_w_uq, mla_kv_norm_g, mla_w_ukv, mla_w_out, final_g):
    depth = norm_g.shape[0]
    for i in range(depth):
        kind, j = i % N_MIXERS, i // N_MIXERS
        last = i == depth - 1
        if kind == 0:
            x = _pool_layer(x, norm_g[i], pool_w_in[j], pool_w_grp[j], pool_scale[j], pool_w_out[j],
                            final_g if last else None)
        elif kind == 1:
            x = _gdn_layer(x, norm_g[i], gdn_w_in[j], gdn_conv[j], gdn_a_log[j], gdn_dt_bias[j],
                           gdn_norm_g[j], gdn_w_out[j])
        else:
            x = _mla_layer(x, positions, norm_g[i], mla_w_in[j], mla_q_norm_g[j], mla_w_uq[j],
                           mla_kv_norm_g[j], mla_w_ukv[j], mla_w_out[j])
        if last and kind != 0:
            x = _final_norm(x, final_g)
    return x


def _final_norm_body(x_ref, g_ref, o_ref):
    o_ref[0] = _rms(x_ref[0], g_ref[...])


def _final_norm(x, g, *, ts=512):
    B, S, D = x.shape
    tok = pl.BlockSpec((1, ts, D), lambda b, j: (b, j, 0))
    return pl.pallas_call(
        _final_norm_body, grid=(B, S // ts), in_specs=[tok, _const_spec((1, D))], out_specs=tok,
        out_shape=jax.ShapeDtypeStruct((B, S, D), F32), compiler_params=_params(2),
        name="final_norm",
    )(x, g.reshape(1, D))
```

```python
import functools

import jax
import jax.numpy as jnp
from jax import lax
from jax.experimental import pallas as pl
from jax.experimental.pallas import tpu as pltpu

F32 = jnp.float32
BF16 = jnp.bfloat16
EPS = 1e-6
NEG = -1e30
VMEM_LIMIT_V7X = 56 * 2 ** 20
LANES = 128
SUBLANES = 8

N_MIXERS = 3

POOL_WINDOWS = (2, 4, 8, 16)
POOL_HALO = SUBLANES * (max(POOL_WINDOWS).bit_length() - 1)

GDN_HEADS = 8
GDN_DK = 128
GDN_DV = 256
GDN_CONV = 4
GDN_CHUNK = 64
GDN_QK = GDN_HEADS * GDN_DK
GDN_V = GDN_HEADS * GDN_DV
GDN_UNIT = 2 * GDN_CHUNK
CONV_HALO = 8

MLA_HEADS = 16
MLA_NOPE = 128
MLA_ROPE = 64
MLA_V = 128
MLA_Q_LORA = 768
MLA_KV_LORA = 512
MLA_QK = MLA_NOPE + MLA_ROPE
MLA_QPAD = 256
ROPE_THETA = 10000.0
ATTN_STRIP = 32
LOG2E = 1.4426950408889634


def _const_spec(shape):
    zeros = (0,) * len(shape)
    return pl.BlockSpec(shape, lambda *_: zeros, pipeline_mode=pl.Buffered(1))


def _params(n_axes):
    return pltpu.CompilerParams(dimension_semantics=("arbitrary",) * n_axes,
                                vmem_limit_bytes=VMEM_LIMIT_V7X)


def _rms(x, g):
    return x * lax.rsqrt(jnp.mean(x * x, axis=-1, keepdims=True) + EPS) * g


def _silu(x):
    h = 0.5 * x
    return h + h * jnp.tanh(h)


def _mm(a, b):
    return jnp.dot(a.astype(BF16), b.astype(BF16), preferred_element_type=F32)


def _mm_nt(a, b):
    return lax.dot_general(a.astype(BF16), b.astype(BF16), (((1,), (1,)), ((), ())),
                           preferred_element_type=F32)


def _pool_body(x_ref, ng_ref, win_ref, wgrp_ref, sc_ref, wout_ref, fg_ref, o_ref, ubuf_ref,
               sa_ref, sb_ref, *, ts, width, final):
    j = pl.program_id(1)
    grp = width // len(POOL_WINDOWS)

    def window_sum(w, c0):
        levels = w.bit_length() - 1
        src, cols, shift = ubuf_ref, slice(c0, c0 + grp), 1
        for lv in range(levels):
            r0 = POOL_HALO - SUBLANES * (levels - 1 - lv)
            n = POOL_HALO + ts - r0
            val = src[r0:r0 + n, cols] + src[r0 - shift:r0 - shift + n, cols]
            if lv == levels - 1:
                return val
            dst = (sa_ref, sb_ref)[lv % 2]
            dst[r0:r0 + n, :] = val
            src, cols, shift = dst, slice(0, grp), 2 * shift

    @pl.when(j == 0)
    def _():
        ubuf_ref[0:POOL_HALO, :] = jnp.zeros((POOL_HALO, width), F32)

    x = x_ref[0]
    h = _rms(x, ng_ref[...]).astype(BF16)
    u = jnp.dot(h, win_ref[:, :width], preferred_element_type=F32)
    gate = jnp.dot(h, win_ref[:, width:], preferred_element_type=F32)
    ubuf_ref[POOL_HALO:POOL_HALO + ts, :] = u
    t = j * ts + lax.broadcasted_iota(jnp.int32, (ts, 1), 0)
    parts = []
    for gi, w in enumerate(POOL_WINDOWS):
        c0 = gi * grp
        cnt = jnp.minimum(t + 1, w).astype(F32)
        p = window_sum(w, c0) / cnt - u[:, c0:c0 + grp]
        parts.append(jnp.dot(p.astype(BF16), wgrp_ref[gi], preferred_element_type=F32))
    pg = jnp.concatenate(parts, axis=1)
    y = pg * sc_ref[...] * _silu(gate)
    out = x + jnp.dot(y.astype(BF16), wout_ref[...], preferred_element_type=F32)
    ubuf_ref[0:POOL_HALO, :] = u[ts - POOL_HALO:, :]
    if final:
        out = _rms(out, fg_ref[...])
    o_ref[0] = out


def _pool_layer(x, ng, w_in, w_grp, scale, w_out, final_g, *, ts=512):
    B, S, D = x.shape
    width = w_out.shape[0]
    final = final_g is not None
    fg = final_g if final else jnp.ones((D,), F32)
    body = functools.partial(_pool_body, ts=ts, width=width, final=final)
    return pl.pallas_call(
        body,
        grid=(B, S // ts),
        in_specs=[
            pl.BlockSpec((1, ts, D), lambda b, j: (b, j, 0)),
            _const_spec((1, D)),
            _const_spec(w_in.shape),
            _const_spec(w_grp.shape),
            _const_spec((1, width)),
            _const_spec(w_out.shape),
            _const_spec((1, D)),
        ],
        out_specs=pl.BlockSpec((1, ts, D), lambda b, j: (b, j, 0)),
        out_shape=jax.ShapeDtypeStruct((B, S, D), F32),
        scratch_shapes=[pltpu.VMEM((POOL_HALO + ts, width), F32),
                        pltpu.VMEM((POOL_HALO + ts, width // len(POOL_WINDOWS)), F32),
                        pltpu.VMEM((POOL_HALO + ts, width // len(POOL_WINDOWS)), F32)],
        compiler_params=_params(2),
        name="pool_layer",
    )(x, ng.reshape(1, D), w_in.astype(BF16), w_grp.astype(BF16), scale.reshape(1, width),
      w_out.astype(BF16), fg.reshape(1, D))


def _gdn_in_body(x_ref, ng_ref, wqkv_ref, wgate_ref, wba_ref, conv_ref, alog_ref, dtb_ref,
                 q_ref, k_ref, v_ref, gate_ref, gb_ref, cbuf_ref, *, ts):
    j = pl.program_id(1)
    nch = 2 * GDN_QK + GDN_V

    @pl.when(j == 0)
    def _():
        cbuf_ref[0:CONV_HALO, :] = jnp.zeros((CONV_HALO, nch), F32)

    x = x_ref[0]
    h = _rms(x, ng_ref[...]).astype(BF16)
    pre = jnp.dot(h, wqkv_ref[...], preferred_element_type=F32)
    cbuf_ref[CONV_HALO:CONV_HALO + ts, :] = pre
    acc = pre * conv_ref[GDN_CONV - 1:GDN_CONV, :]
    for kk in range(GDN_CONV - 1):
        r0 = CONV_HALO - (GDN_CONV - 1) + kk
        acc = acc + cbuf_ref[r0:r0 + ts, :] * conv_ref[kk:kk + 1, :]
    cbuf_ref[0:CONV_HALO, :] = pre[ts - CONV_HALO:, :]
    a = _silu(acc)
    for hh in range(GDN_HEADS):
        c = slice(hh * GDN_DK, (hh + 1) * GDN_DK)
        qh = a[:, c]
        qn = qh * lax.rsqrt(jnp.sum(qh * qh, axis=-1, keepdims=True) + EPS) * (GDN_DK ** -0.5)
        q_ref[0, :, c] = qn.astype(BF16)
        kh = a[:, GDN_QK + hh * GDN_DK:GDN_QK + (hh + 1) * GDN_DK]
        kn = kh * lax.rsqrt(jnp.sum(kh * kh, axis=-1, keepdims=True) + EPS)
        k_ref[0, :, c] = kn.astype(BF16)
    v_ref[0] = a[:, 2 * GDN_QK:].astype(BF16)
    gate_ref[0] = _silu(jnp.dot(h, wgate_ref[...], preferred_element_type=F32)).astype(BF16)

    ba = jnp.dot(h, wba_ref[...], preferred_element_type=F32)
    lane = lax.broadcasted_iota(jnp.int32, (1, LANES), 1)
    z = ba + dtb_ref[...]
    softplus = jnp.maximum(z, 0.0) + jnp.log1p(jnp.exp(-jnp.abs(z)))
    g = jnp.where((lane >= GDN_HEADS) & (lane < 2 * GDN_HEADS),
                  -jnp.exp(alog_ref[...]) * softplus, 0.0)
    r = lax.broadcasted_iota(jnp.int32, (ts, 1), 0)
    c = lax.broadcasted_iota(jnp.int32, (1, ts), 1)
    tri = jnp.where(((r // GDN_CHUNK) == (c // GDN_CHUNK)) & (c <= r), 1.0, 0.0).astype(BF16)
    g_hi = g.astype(BF16)
    r1 = g - g_hi.astype(F32)
    g_mid = r1.astype(BF16)
    g_lo = (r1 - g_mid.astype(F32)).astype(BF16)
    gc = (jnp.dot(tri, g_hi, preferred_element_type=F32)
          + jnp.dot(tri, g_mid, preferred_element_type=F32)
          + jnp.dot(tri, g_lo, preferred_element_type=F32))
    gb_ref[0] = jnp.where(lane < GDN_HEADS, jax.nn.sigmoid(ba), gc)


def _gdn_core_body(q_ref, k_ref, v_ref, gb_ref, gbt_ref, o_ref, s_ref):
    j = pl.program_id(1)

    @pl.when(j == 0)
    def _():
        s_ref[...] = jnp.zeros_like(s_ref)

    n = GDN_UNIT
    r = lax.broadcasted_iota(jnp.int32, (n, 1), 0)
    c = lax.broadcasted_iota(jnp.int32, (1, n), 1)
    same = (r // GDN_CHUNK) == (c // GDN_CHUNK)
    causal = same & (c <= r)
    strict = same & (c < r)
    eye = jnp.where(r == c, 1.0, 0.0)
    zeros_c = jnp.zeros((GDN_CHUNK, GDN_DV), F32)
    nb = q_ref.shape[0]
    hs = range(nb * GDN_HEADS)
    bi = [ci // GDN_HEADS for ci in hs]
    hi = [ci % GDN_HEADS for ci in hs]
    gb = [gb_ref[b] for b in range(nb)]
    gbt = [gbt_ref[b] for b in range(nb)]
    beta_c = [gb[bi[h]][:, hi[h]:hi[h] + 1] for h in hs]
    gc_c = [gb[bi[h]][:, GDN_HEADS + hi[h]:GDN_HEADS + hi[h] + 1] for h in hs]
    gc_r = [gbt[bi[h]][GDN_HEADS + hi[h]:GDN_HEADS + hi[h] + 1, :] for h in hs]
    q = [q_ref[bi[h], :, hi[h] * GDN_DK:(hi[h] + 1) * GDN_DK] for h in hs]
    k = [k_ref[bi[h], :, hi[h] * GDN_DK:(hi[h] + 1) * GDN_DK] for h in hs]
    kf = [k[h].astype(F32) for h in hs]
    kb = [kf[h] * beta_c[h] for h in hs]
    decay = [jnp.exp(jnp.where(causal, gc_c[h] - gc_r[h], NEG)) for h in hs]
    kkqk = [_mm_nt(jnp.concatenate([kb[h].astype(BF16), q[h]], axis=0), k[h]) for h in hs]
    lmat = [jnp.where(strict, kkqk[h][:n] * decay[h], 0.0) for h in hs]
    attn = [kkqk[h][n:] * decay[h] for h in hs]
    ch = GDN_CHUNK
    eye_s = eye[:ch] + eye[ch:]

    def block_diag(side):
        return jnp.where(same, jnp.concatenate([side, side], axis=0), 0.0)

    l_s = [lmat[h][:ch] + lmat[h][ch:] for h in hs]
    xs = [eye_s - l_s[h] for h in hs]
    ps = [_mm(l_s[h], lmat[h]) for h in hs]
    for _ in range(4):
        res = [_mm(jnp.concatenate([xs[h], ps[h]], axis=0), block_diag(ps[h])) for h in hs]
        xs = [xs[h] + res[h][:ch] for h in hs]
        ps = [res[h][ch:] for h in hs]
    res = [_mm(xs[h], block_diag(ps[h])) for h in hs]
    xm = [block_diag(xs[h] + res[h]) for h in hs]
    eg = [jnp.exp(gc_c[h]) for h in hs]
    rhs = [jnp.concatenate(
        [v_ref[bi[h], :, hi[h] * GDN_DV:(hi[h] + 1) * GDN_DV].astype(F32) * beta_c[h],
         kb[h] * eg[h]], axis=1) for h in hs]
    sol = [rhs[h] + _mm(xm[h] - eye, rhs[h]) for h in hs]
    qg = [q[h].astype(F32) * eg[h] for h in hs]
    k_t = [kf[h].T for h in hs]
    state = [s_ref[h] for h in hs]
    for cc in range(2):
        rows = slice(cc * GDN_CHUNK, (cc + 1) * GDN_CHUNK)
        in_chunk = (c // GDN_CHUNK) == cc
        g_last = [gc_r[h][:, (cc + 1) * GDN_CHUNK - 1:(cc + 1) * GDN_CHUNK] for h in hs]
        res = [_mm(jnp.concatenate([sol[h][rows, GDN_DV:], qg[h][rows]], axis=0), state[h])
               for h in hs]
        v_new = [sol[h][rows, :GDN_DV] - res[h][:GDN_CHUNK] for h in hs]
        v_all = [jnp.concatenate([v_new[h], zeros_c] if cc == 0 else [zeros_c, v_new[h]], axis=0)
                 for h in hs]
        o_c = [res[h][GDN_CHUNK:] + _mm(attn[h][rows], v_all[h]) for h in hs]
        k_dec_t = [k_t[h] * jnp.exp(jnp.where(in_chunk, g_last[h] - gc_r[h], NEG)) for h in hs]
        state = [state[h] * jnp.exp(g_last[h]) + _mm(k_dec_t[h], v_all[h]) for h in hs]
        for h in hs:
            o_ref[bi[h], rows, hi[h] * GDN_DV:(hi[h] + 1) * GDN_DV] = o_c[h].astype(o_ref.dtype)
    for h in hs:
        s_ref[h] = state[h]


def _gdn_out_body(x_ref, o_ref, gate_ref, ng_ref, wout_ref, y_ref):
    o = o_ref[0].astype(F32)
    parts = []
    for h in range(GDN_HEADS):
        oh = o[:, h * GDN_DV:(h + 1) * GDN_DV]
        parts.append(_rms(oh, ng_ref[...]))
    on = jnp.concatenate(parts, axis=1) * gate_ref[0].astype(F32)
    y_ref[0] = x_ref[0] + jnp.dot(on.astype(BF16), wout_ref[...], preferred_element_type=F32)


def _gdn_layer(x, ng, w_in, conv_w, a_log, dt_bias, norm_g, w_out, *, ts=256):
    B, S, D = x.shape
    nch = 2 * GDN_QK + GDN_V
    w_qkv = w_in[:, :nch].astype(BF16)
    w_gate = w_in[:, nch:nch + GDN_V].astype(BF16)
    w_ba = jnp.pad(w_in[:, nch + GDN_V:], ((0, 0), (0, LANES - 2 * GDN_HEADS))).astype(BF16)
    pad_a = (GDN_HEADS, LANES - 2 * GDN_HEADS)
    alog = jnp.pad(a_log.astype(F32), pad_a).reshape(1, LANES)
    dtb = jnp.pad(dt_bias.astype(F32), pad_a).reshape(1, LANES)

    tok = lambda w: pl.BlockSpec((1, ts, w), lambda b, j: (b, j, 0))
    q, k, v, gate, gb = pl.pallas_call(
        functools.partial(_gdn_in_body, ts=ts),
        grid=(B, S // ts),
        in_specs=[tok(D), _const_spec((1, D)), _const_spec(w_qkv.shape), _const_spec(w_gate.shape),
                  _const_spec(w_ba.shape), _const_spec(conv_w.shape), _const_spec((1, LANES)),
                  _const_spec((1, LANES))],
        out_specs=[tok(GDN_QK), tok(GDN_QK), tok(GDN_V), tok(GDN_V), tok(LANES)],
        out_shape=[jax.ShapeDtypeStruct((B, S, GDN_QK), BF16),
                   jax.ShapeDtypeStruct((B, S, GDN_QK), BF16),
                   jax.ShapeDtypeStruct((B, S, GDN_V), BF16),
                   jax.ShapeDtypeStruct((B, S, GDN_V), BF16),
                   jax.ShapeDtypeStruct((B, S, LANES), F32)],
        scratch_shapes=[pltpu.VMEM((CONV_HALO + ts, nch), F32)],
        compiler_params=_params(2),
        name="gdn_in",
    )(x, ng.reshape(1, D), w_qkv, w_gate, w_ba, conv_w, alog, dtb)

    gbt = jnp.transpose(gb[:, :, :2 * GDN_HEADS], (0, 2, 1))
    n = GDN_UNIT
    nb = 2 if B % 2 == 0 else 1
    utok = lambda w: pl.BlockSpec((nb, n, w), lambda b, j: (b, j, 0))
    o = pl.pallas_call(
        _gdn_core_body,
        grid=(B // nb, S // n),
        in_specs=[utok(GDN_QK), utok(GDN_QK), utok(GDN_V), utok(LANES),
                  pl.BlockSpec((nb, 2 * GDN_HEADS, n), lambda b, j: (b, 0, j))],
        out_specs=utok(GDN_V),
        out_shape=jax.ShapeDtypeStruct((B, S, GDN_V), BF16),
        scratch_shapes=[pltpu.VMEM((nb * GDN_HEADS, GDN_DK, GDN_DV), F32)],
        compiler_params=_params(2),
        name="gdn_core",
    )(q, k, v, gb, gbt)

    return pl.pallas_call(
        _gdn_out_body,
        grid=(B, S // ts),
        in_specs=[tok(D), tok(GDN_V), tok(GDN_V), _const_spec((1, GDN_DV)),
                  _const_spec(w_out.shape)],
        out_specs=tok(D),
        out_shape=jax.ShapeDtypeStruct((B, S, D), F32),
        compiler_params=_params(2),
        name="gdn_out",
    )(x, o, gate, norm_g.reshape(1, GDN_DV), w_out.astype(BF16))


def _mla_in_body(x_ref, pos_ref, ng_ref, win_ref, qg_ref, wuq_ref, kvg_ref, wukv_ref, inv_ref,
                 q_ref, k_ref, v_ref, gate_ref):
    c1 = MLA_Q_LORA
    c2 = c1 + MLA_KV_LORA
    c3 = c2 + LANES
    x = x_ref[0]
    h = _rms(x, ng_ref[...]).astype(BF16)
    cq = jnp.dot(h, win_ref[:, :c1], preferred_element_type=F32)
    ckv = jnp.dot(h, win_ref[:, c1:c2], preferred_element_type=F32)
    kr = jnp.dot(h, win_ref[:, c2:c3], preferred_element_type=F32)
    gate = jnp.dot(h, win_ref[:, c3:], preferred_element_type=F32)
    gate_ref[0] = _silu(gate).astype(BF16)

    ang = pos_ref[0].astype(F32) * inv_ref[...]
    cos = jnp.cos(ang)
    sin = jnp.sin(ang)
    lane = lax.broadcasted_iota(jnp.int32, (1, LANES), 1)
    quarter = lane // (MLA_ROPE // 2)
    sin_s = sin * jnp.where((quarter == 0) | (quarter == 3), -1.0, 1.0)

    def rope(t):
        return t * cos + pltpu.roll(t, MLA_ROPE, 1) * sin_s

    k_rope = jnp.where(lane < MLA_ROPE, rope(kr), 0.0).astype(BF16)
    qf = jnp.dot(_rms(cq, qg_ref[...]).astype(BF16), wuq_ref[...],
                 preferred_element_type=F32) * (MLA_QK ** -0.5 * LOG2E)
    kv = jnp.dot(_rms(ckv, kvg_ref[...]).astype(BF16), wukv_ref[...], preferred_element_type=F32)
    for hh in range(MLA_HEADS):
        q0 = hh * MLA_QPAD
        q_ref[0, hh, :, 0:MLA_NOPE] = qf[:, q0:q0 + MLA_NOPE].astype(BF16)
        q_ref[0, hh, :, MLA_NOPE:] = rope(qf[:, q0 + MLA_NOPE:q0 + MLA_QPAD]).astype(BF16)
        k0 = hh * (MLA_NOPE + MLA_V)
        k_ref[0, hh, :, 0:MLA_NOPE] = kv[:, k0:k0 + MLA_NOPE].astype(BF16)
        k_ref[0, hh, :, MLA_NOPE:] = k_rope
        v_ref[0, hh, :, :MLA_V] = kv[:, k0 + MLA_NOPE:k0 + MLA_NOPE + MLA_V].astype(BF16)
        v_ref[0, hh, :, MLA_V:] = jnp.ones((kv.shape[0], LANES), BF16)


def _attn_body(q_ref, k_ref, v_ref, o_ref, s_ref, p_ref, m_ref, a_ref, acc_ref, *, tq, tk):
    i = pl.program_id(2)
    nh = q_ref.shape[1]
    m_ref[...] = jnp.full(m_ref.shape, NEG, F32)
    acc_ref[...] = jnp.zeros(acc_ref.shape, F32)

    def step(c0, width, masked):
        for h in range(nh):
            s_ref[h, :, :width] = lax.dot_general(
                q_ref[0, h], k_ref[0, h, pl.ds(c0, width), :], (((1,), (1,)), ((), ())),
                preferred_element_type=F32)
        for h in range(nh):
            for r0 in range(0, tq, ATTN_STRIP):
                rs = slice(r0, r0 + ATTN_STRIP)
                s = s_ref[h, rs, :width]
                if masked:
                    row = i * tq + r0 + lax.broadcasted_iota(jnp.int32, (ATTN_STRIP, 1), 0)
                    col = c0 + lax.broadcasted_iota(jnp.int32, (1, width), 1)
                    s = jnp.where(col <= row, s, NEG)
                m_old = m_ref[h, rs, :]
                m_new = jnp.maximum(m_old, jnp.max(s, axis=-1, keepdims=True))
                a_ref[h, rs, :] = jnp.exp2(m_old - m_new)
                m_ref[h, rs, :] = m_new
                p_ref[h, rs, :width] = jnp.concatenate(
                    [jnp.exp2(s[:, c:c + LANES] - m_new) for c in range(0, width, LANES)],
                    axis=1).astype(BF16)
        for h in range(nh):
            pv = jnp.dot(p_ref[h, :, :width], v_ref[0, h, pl.ds(c0, width), :],
                         preferred_element_type=F32)
            a = a_ref[h]
            acc_ref[h, :, :MLA_V] = a * acc_ref[h, :, :MLA_V] + pv[:, :MLA_V]
            acc_ref[h, :, MLA_V:] = a * acc_ref[h, :, MLA_V:] + pv[:, MLA_V:]

    n_wide = (i * tq) // tk

    def wide_step(kb, carry):
        step(pl.multiple_of(kb * tk, tk), tk, False)
        return carry

    lax.fori_loop(0, n_wide, wide_step, 0)
    for d in range(tk // tq - 1):
        @pl.when(n_wide * tk + d * tq < i * tq)
        def _():
            step(pl.multiple_of(n_wide * tk + d * tq, tq), tq, False)
    step(pl.multiple_of(i * tq, tq), tq, True)
    for h in range(nh):
        o_ref[0, :, h * MLA_V:(h + 1) * MLA_V] = (
            acc_ref[h, :, :MLA_V] / acc_ref[h, :, MLA_V:]).astype(o_ref.dtype)


def _gated_out_body(x_ref, o_ref, gate_ref, wout_ref, y_ref):
    y = o_ref[0].astype(F32) * gate_ref[0].astype(F32)
    y_ref[0] = x_ref[0] + jnp.dot(y.astype(BF16), wout_ref[...], preferred_element_type=F32)


def _mla_layer(x, pos, ng, w_in, q_norm_g, w_uq, kv_norm_g, w_ukv, w_out, *, ts=256, tq=512, tk=1024, nh=4):
    B, S, D = x.shape
    H = MLA_HEADS
    half = MLA_ROPE // 2
    c1 = MLA_Q_LORA
    c2 = c1 + MLA_KV_LORA
    c3 = c2 + MLA_ROPE
    w_kr = w_in[:, c2:c3]
    w_kr = jnp.concatenate([w_kr, w_kr[:, half:], w_kr[:, :half]], axis=1)
    w_in_p = jnp.concatenate([w_in[:, :c2], w_kr, w_in[:, c3:]], axis=1).astype(BF16)
    wq = w_uq.reshape(MLA_Q_LORA, H, MLA_QK)
    wq_r = wq[:, :, MLA_NOPE:]
    wq_p = jnp.concatenate([wq, wq_r[:, :, half:], wq_r[:, :, :half]], axis=2)
    wq_p = wq_p.reshape(MLA_Q_LORA, H * MLA_QPAD).astype(BF16)
    inv = ROPE_THETA ** (-jnp.arange(half, dtype=F32) / half)
    inv = jnp.tile(inv, LANES // half).reshape(1, LANES)
    width = H * MLA_V

    tok = lambda w: pl.BlockSpec((1, ts, w), lambda b, j: (b, j, 0))
    head = lambda w: pl.BlockSpec((1, H, ts, w), lambda b, j: (b, 0, j, 0))
    q, k, v, gate = pl.pallas_call(
        _mla_in_body,
        grid=(B, S // ts),
        in_specs=[tok(D), tok(1), _const_spec((1, D)), _const_spec(w_in_p.shape),
                  _const_spec((1, MLA_Q_LORA)), _const_spec(wq_p.shape),
                  _const_spec((1, MLA_KV_LORA)), _const_spec(w_ukv.shape), _const_spec((1, LANES))],
        out_specs=[head(MLA_QPAD), head(MLA_QPAD), head(MLA_V + LANES), tok(width)],
        out_shape=[jax.ShapeDtypeStruct((B, H, S, MLA_QPAD), BF16),
                   jax.ShapeDtypeStruct((B, H, S, MLA_QPAD), BF16),
                   jax.ShapeDtypeStruct((B, H, S, MLA_V + LANES), BF16),
                   jax.ShapeDtypeStruct((B, S, width), BF16)],
        compiler_params=_params(2),
        name="mla_in",
    )(x, pos.reshape(B, S, 1), ng.reshape(1, D), w_in_p, q_norm_g.reshape(1, MLA_Q_LORA), wq_p,
      kv_norm_g.reshape(1, MLA_KV_LORA), w_ukv.astype(BF16), inv)

    o = pl.pallas_call(
        functools.partial(_attn_body, tq=tq, tk=tk),
        grid=(B, H // nh, S // tq),
        in_specs=[pl.BlockSpec((1, nh, tq, MLA_QPAD), lambda b, h, i: (b, h, i, 0)),
                  pl.BlockSpec((1, nh, S, MLA_QPAD), lambda b, h, i: (b, h, 0, 0)),
                  pl.BlockSpec((1, nh, S, MLA_V + LANES), lambda b, h, i: (b, h, 0, 0))],
        out_specs=pl.BlockSpec((1, tq, nh * MLA_V), lambda b, h, i: (b, i, h)),
        out_shape=jax.ShapeDtypeStruct((B, S, width), BF16),
        scratch_shapes=[pltpu.VMEM((nh, tq, tk), F32), pltpu.VMEM((nh, tq, tk), BF16),
                        pltpu.VMEM((nh, tq, LANES), F32), pltpu.VMEM((nh, tq, LANES), F32),
                        pltpu.VMEM((nh, tq, MLA_V + LANES), F32)],
        compiler_params=_params(3),
        name="mla_attn",
    )(q, k, v)

    return pl.pallas_call(
        _gated_out_body,
        grid=(B, S // ts),
        in_specs=[tok(D), tok(width), tok(width), _const_spec(w_out.shape)],
        out_specs=tok(D),
        out_shape=jax.ShapeDtypeStruct((B, S, D), F32),
        compiler_params=_params(2),
        name="mla_out",
    )(x, o, gate, w_out.astype(BF16))


def kernel(x, positions, norm_g, pool_w_in, pool_w_grp, pool_scale, pool_w_out, gdn_w_in, gdn_conv, gdn_a_log, gdn_dt_bias, gdn_norm_g, gdn_w_out, mla_w_in, mla_q_norm_g, mla_w_uq, mla_kv_norm_g, mla_w_ukv, mla_w_out, final_g):
    depth = norm_g.shape[0]
    for i in range(depth):
        kind, j = i % N_MIXERS, i // N_MIXERS
        last = i == depth - 1
        if kind == 0:
            x = _pool_layer(x, norm_g[i], pool_w_in[j], pool_w_grp[j], pool_scale[j], pool_w_out[j],
                            final_g if last else None)
        elif kind == 1:
            x = _gdn_layer(x, norm_g[i], gdn_w_in[j], gdn_conv[j], gdn_a_log[j], gdn_dt_bias[j],
                           gdn_norm_g[j], gdn_w_out[j])
        else:
            x = _mla_layer(x, positions, norm_g[i], mla_w_in[j], mla_q_norm_g[j], mla_w_uq[j],
                           mla_kv_norm_g[j], mla_w_ukv[j], mla_w_out[j])
        if last and kind != 0:
            x = _final_norm(x, final_g)
    return x


def _final_norm_body(x_ref, g_ref, o_ref):
    o_ref[0] = _rms(x_ref[0], g_ref[...])


def _final_norm(x, g, *, ts=512):
    B, S, D = x.shape
    tok = pl.BlockSpec((1, ts, D), lambda b, j: (b, j, 0))
    return pl.pallas_call(
        _final_norm_body, grid=(B, S // ts), in_specs=[tok, _const_spec((1, D))], out_specs=tok,
        out_shape=jax.ShapeDtypeStruct((B, S, D), F32), compiler_params=_params(2),
        name="final_norm",
    )(x, g.reshape(1, D))
```

```python
import functools

import jax
import jax.numpy as jnp
from jax import lax
from jax.experimental import pallas as pl
from jax.experimental.pallas import tpu as pltpu

F32 = jnp.float32
BF16 = jnp.bfloat16
EPS = 1e-6
NEG = -1e30
VMEM_LIMIT_V7X = 56 * 2 ** 20
ATTN_VMEM_LIMIT_V7X = 60 * 2 ** 20
LANES = 128
SUBLANES = 8

N_MIXERS = 3

POOL_WINDOWS = (2, 4, 8, 16)
POOL_HALO = SUBLANES * (max(POOL_WINDOWS).bit_length() - 1)

GDN_HEADS = 8
GDN_DK = 128
GDN_DV = 256
GDN_CONV = 4
GDN_CHUNK = 64
GDN_QK = GDN_HEADS * GDN_DK
GDN_V = GDN_HEADS * GDN_DV
GDN_UNIT = 2 * GDN_CHUNK
CONV_HALO = 8

MLA_HEADS = 16
MLA_NOPE = 128
MLA_ROPE = 64
MLA_V = 128
MLA_Q_LORA = 768
MLA_KV_LORA = 512
MLA_QK = MLA_NOPE + MLA_ROPE
MLA_QPAD = 256
ROPE_THETA = 10000.0
ATTN_STRIP = 32
LOG2E = 1.4426950408889634


def _const_spec(shape):
    zeros = (0,) * len(shape)
    return pl.BlockSpec(shape, lambda *_: zeros, pipeline_mode=pl.Buffered(1))


def _layer_spec(stacked_shape, layer):
    index = (layer,) + (0,) * (len(stacked_shape) - 1)
    return pl.BlockSpec((None,) + tuple(stacked_shape[1:]), lambda *_: index,
                        pipeline_mode=pl.Buffered(1))


def _params(n_axes, vmem_limit=VMEM_LIMIT_V7X):
    return pltpu.CompilerParams(dimension_semantics=("arbitrary",) * n_axes,
                                vmem_limit_bytes=vmem_limit)


def _rms(x, g):
    return x * lax.rsqrt(jnp.mean(x * x, axis=-1, keepdims=True) + EPS) * g


def _silu(x):
    h = 0.5 * x
    return h + h * jnp.tanh(h)


def _mm(a, b):
    return jnp.dot(a.astype(BF16), b.astype(BF16), preferred_element_type=F32)


def _mm_nt(a, b):
    return lax.dot_general(a.astype(BF16), b.astype(BF16), (((1,), (1,)), ((), ())),
                           preferred_element_type=F32)


def _pool_body(x_ref, ng_ref, win_ref, wgrp_ref, sc_ref, wout_ref, fg_ref, o_ref, ubuf_ref,
               sa_ref, sb_ref, *, ts, width, final):
    j = pl.program_id(1)
    grp = width // len(POOL_WINDOWS)

    def window_sum(w, c0):
        levels = w.bit_length() - 1
        src, cols, shift = ubuf_ref, slice(c0, c0 + grp), 1
        for lv in range(levels):
            r0 = POOL_HALO - SUBLANES * (levels - 1 - lv)
            n = POOL_HALO + ts - r0
            val = src[r0:r0 + n, cols] + src[r0 - shift:r0 - shift + n, cols]
            if lv == levels - 1:
                return val
            dst = (sa_ref, sb_ref)[lv % 2]
            dst[r0:r0 + n, :] = val
            src, cols, shift = dst, slice(0, grp), 2 * shift

    @pl.when(j == 0)
    def _():
        ubuf_ref[0:POOL_HALO, :] = jnp.zeros((POOL_HALO, width), F32)

    x = x_ref[0]
    h = _rms(x, ng_ref[...]).astype(BF16)
    u = jnp.dot(h, win_ref[:, :width], preferred_element_type=F32)
    gate = jnp.dot(h, win_ref[:, width:], preferred_element_type=F32)
    ubuf_ref[POOL_HALO:POOL_HALO + ts, :] = u
    t = j * ts + lax.broadcasted_iota(jnp.int32, (ts, 1), 0)
    parts = []
    for gi, w in enumerate(POOL_WINDOWS):
        c0 = gi * grp
        cnt = jnp.minimum(t + 1, w).astype(F32)
        p = window_sum(w, c0) / cnt - u[:, c0:c0 + grp]
        parts.append(jnp.dot(p.astype(BF16), wgrp_ref[gi], preferred_element_type=F32))
    pg = jnp.concatenate(parts, axis=1)
    y = pg * sc_ref[...] * _silu(gate)
    out = x + jnp.dot(y.astype(BF16), wout_ref[...], preferred_element_type=F32)
    ubuf_ref[0:POOL_HALO, :] = u[ts - POOL_HALO:, :]
    if final:
        out = _rms(out, fg_ref[...])
    o_ref[0] = out


def _pool_layer(x, ng, w_in, w_grp, scale, w_out, layer, final_g, *, ts=512):
    B, S, D = x.shape
    width = w_out.shape[1]
    final = final_g is not None
    fg = final_g if final else jnp.ones((D,), F32)
    body = functools.partial(_pool_body, ts=ts, width=width, final=final)
    return pl.pallas_call(
        body,
        grid=(B, S // ts),
        in_specs=[
            pl.BlockSpec((1, ts, D), lambda b, j: (b, j, 0)),
            _const_spec((1, D)),
            _layer_spec(w_in.shape, layer),
            _layer_spec(w_grp.shape, layer),
            _const_spec((1, width)),
            _layer_spec(w_out.shape, layer),
            _const_spec((1, D)),
        ],
        out_specs=pl.BlockSpec((1, ts, D), lambda b, j: (b, j, 0)),
        out_shape=jax.ShapeDtypeStruct((B, S, D), F32),
        scratch_shapes=[pltpu.VMEM((POOL_HALO + ts, width), F32),
                        pltpu.VMEM((POOL_HALO + ts, width // len(POOL_WINDOWS)), F32),
                        pltpu.VMEM((POOL_HALO + ts, width // len(POOL_WINDOWS)), F32)],
        compiler_params=_params(2),
        name="pool_layer",
    )(x, ng.reshape(1, D), w_in, w_grp, scale.reshape(1, width), w_out, fg.reshape(1, D))


def _gdn_in_body(x_ref, ng_ref, wqkv_ref, wgate_ref, wba_ref, conv_ref, alog_ref, dtb_ref,
                 q_ref, k_ref, v_ref, gate_ref, gb_ref, cbuf_ref, *, ts):
    j = pl.program_id(1)
    nch = 2 * GDN_QK + GDN_V

    @pl.when(j == 0)
    def _():
        cbuf_ref[0:CONV_HALO, :] = jnp.zeros((CONV_HALO, nch), F32)

    x = x_ref[0]
    h = _rms(x, ng_ref[...]).astype(BF16)
    pre = jnp.dot(h, wqkv_ref[...], preferred_element_type=F32)
    cbuf_ref[CONV_HALO:CONV_HALO + ts, :] = pre
    acc = pre * conv_ref[GDN_CONV - 1:GDN_CONV, :]
    for kk in range(GDN_CONV - 1):
        r0 = CONV_HALO - (GDN_CONV - 1) + kk
        acc = acc + cbuf_ref[r0:r0 + ts, :] * conv_ref[kk:kk + 1, :]
    cbuf_ref[0:CONV_HALO, :] = pre[ts - CONV_HALO:, :]
    a = _silu(acc)
    for hh in range(GDN_HEADS):
        c = slice(hh * GDN_DK, (hh + 1) * GDN_DK)
        qh = a[:, c]
        qn = qh * lax.rsqrt(jnp.sum(qh * qh, axis=-1, keepdims=True) + EPS) * (GDN_DK ** -0.5)
        q_ref[0, :, c] = qn.astype(BF16)
        kh = a[:, GDN_QK + hh * GDN_DK:GDN_QK + (hh + 1) * GDN_DK]
        kn = kh * lax.rsqrt(jnp.sum(kh * kh, axis=-1, keepdims=True) + EPS)
        k_ref[0, :, c] = kn.astype(BF16)
    v_ref[0] = a[:, 2 * GDN_QK:].astype(BF16)
    gate_ref[0] = _silu(jnp.dot(h, wgate_ref[...], preferred_element_type=F32)).astype(BF16)

    ba = jnp.dot(h, wba_ref[...], preferred_element_type=F32)
    lane = lax.broadcasted_iota(jnp.int32, (1, LANES), 1)
    z = ba + dtb_ref[...]
    softplus = jnp.maximum(z, 0.0) + jnp.log1p(jnp.exp(-jnp.abs(z)))
    g = jnp.where((lane >= GDN_HEADS) & (lane < 2 * GDN_HEADS),
                  -jnp.exp(alog_ref[...]) * softplus, 0.0)
    r = lax.broadcasted_iota(jnp.int32, (ts, 1), 0)
    c = lax.broadcasted_iota(jnp.int32, (1, ts), 1)
    tri = jnp.where(((r // GDN_CHUNK) == (c // GDN_CHUNK)) & (c <= r), 1.0, 0.0).astype(BF16)
    g_hi = g.astype(BF16)
    r1 = g - g_hi.astype(F32)
    g_mid = r1.astype(BF16)
    g_lo = (r1 - g_mid.astype(F32)).astype(BF16)
    gc = (jnp.dot(tri, g_hi, preferred_element_type=F32)
          + jnp.dot(tri, g_mid, preferred_element_type=F32)
          + jnp.dot(tri, g_lo, preferred_element_type=F32))
    gb_ref[0] = jnp.where(lane < GDN_HEADS, jax.nn.sigmoid(ba), gc)


def _gdn_core_body(q_ref, k_ref, v_ref, gb_ref, gbt_ref, gate_ref, ng_ref, o_ref, s_ref):
    j = pl.program_id(1)

    @pl.when(j == 0)
    def _():
        s_ref[...] = jnp.zeros_like(s_ref)

    n = GDN_UNIT
    r = lax.broadcasted_iota(jnp.int32, (n, 1), 0)
    c = lax.broadcasted_iota(jnp.int32, (1, n), 1)
    same = (r // GDN_CHUNK) == (c // GDN_CHUNK)
    causal = same & (c <= r)
    strict = same & (c < r)
    eye = jnp.where(r == c, 1.0, 0.0)
    zeros_c = jnp.zeros((GDN_CHUNK, GDN_DV), F32)
    nb = q_ref.shape[0]
    hs = range(nb * GDN_HEADS)
    bi = [ci // GDN_HEADS for ci in hs]
    hi = [ci % GDN_HEADS for ci in hs]
    gb = [gb_ref[b] for b in range(nb)]
    gbt = [gbt_ref[b] for b in range(nb)]
    beta_c = [gb[bi[h]][:, hi[h]:hi[h] + 1] for h in hs]
    gc_c = [gb[bi[h]][:, GDN_HEADS + hi[h]:GDN_HEADS + hi[h] + 1] for h in hs]
    gc_r = [gbt[bi[h]][GDN_HEADS + hi[h]:GDN_HEADS + hi[h] + 1, :] for h in hs]
    q = [q_ref[bi[h], :, hi[h] * GDN_DK:(hi[h] + 1) * GDN_DK] for h in hs]
    k = [k_ref[bi[h], :, hi[h] * GDN_DK:(hi[h] + 1) * GDN_DK] for h in hs]
    kf = [k[h].astype(F32) for h in hs]
    kb = [kf[h] * beta_c[h] for h in hs]
    decay = [jnp.exp(jnp.where(causal, gc_c[h] - gc_r[h], NEG)) for h in hs]
    kkqk = [_mm_nt(jnp.concatenate([kb[h].astype(BF16), q[h]], axis=0), k[h]) for h in hs]
    lmat = [jnp.where(strict, kkqk[h][:n] * decay[h], 0.0) for h in hs]
    attn = [kkqk[h][n:] * decay[h] for h in hs]
    ch = GDN_CHUNK
    eye_s = eye[:ch] + eye[ch:]

    def block_diag(side):
        return jnp.where(same, jnp.concatenate([side, side], axis=0), 0.0)

    l_s = [lmat[h][:ch] + lmat[h][ch:] for h in hs]
    xs = [eye_s - l_s[h] for h in hs]
    ps = [_mm(l_s[h], lmat[h]) for h in hs]
    for _ in range(4):
        res = [_mm(jnp.concatenate([xs[h], ps[h]], axis=0), block_diag(ps[h])) for h in hs]
        xs = [xs[h] + res[h][:ch] for h in hs]
        ps = [res[h][ch:] for h in hs]
    res = [_mm(xs[h], block_diag(ps[h])) for h in hs]
    xm = [block_diag(xs[h] + res[h]) for h in hs]
    eg = [jnp.exp(gc_c[h]) for h in hs]
    rhs = [jnp.concatenate(
        [v_ref[bi[h], :, hi[h] * GDN_DV:(hi[h] + 1) * GDN_DV].astype(F32) * beta_c[h],
         kb[h] * eg[h]], axis=1) for h in hs]
    sol = [rhs[h] + _mm(xm[h] - eye, rhs[h]) for h in hs]
    qg = [q[h].astype(F32) * eg[h] for h in hs]
    k_t = [kf[h].T for h in hs]
    state = [s_ref[h] for h in hs]
    for cc in range(2):
        rows = slice(cc * GDN_CHUNK, (cc + 1) * GDN_CHUNK)
        in_chunk = (c // GDN_CHUNK) == cc
        g_last = [gc_r[h][:, (cc + 1) * GDN_CHUNK - 1:(cc + 1) * GDN_CHUNK] for h in hs]
        res = [_mm(jnp.concatenate([sol[h][rows, GDN_DV:], qg[h][rows]], axis=0), state[h])
               for h in hs]
        v_new = [sol[h][rows, :GDN_DV] - res[h][:GDN_CHUNK] for h in hs]
        v_all = [jnp.concatenate([v_new[h], zeros_c] if cc == 0 else [zeros_c, v_new[h]], axis=0)
                 for h in hs]
        k_dec_t = [k_t[h] * jnp.exp(jnp.where(in_chunk, g_last[h] - gc_r[h], NEG)) for h in hs]
        upd = [_mm(jnp.concatenate([attn[h][rows], k_dec_t[h]], axis=0), v_all[h]) for h in hs]
        o_c = [res[h][GDN_CHUNK:] + upd[h][:GDN_CHUNK] for h in hs]
        state = [state[h] * jnp.exp(g_last[h]) + upd[h][GDN_CHUNK:] for h in hs]
        for h in hs:
            cols = slice(hi[h] * GDN_DV, (hi[h] + 1) * GDN_DV)
            gated = _rms(o_c[h], ng_ref[...]) * gate_ref[bi[h], rows, cols].astype(F32)
            o_ref[bi[h], rows, cols] = gated.astype(o_ref.dtype)
    for h in hs:
        s_ref[h] = state[h]


def _proj_out_body(x_ref, o_ref, wout_ref, y_ref):
    y_ref[0] = x_ref[0] + jnp.dot(o_ref[0], wout_ref[...], preferred_element_type=F32)


def _proj_out(x, o, w_out, *, ts, name):
    B, S, D = x.shape
    tok = lambda w: pl.BlockSpec((1, ts, w), lambda b, j: (b, j, 0))
    return pl.pallas_call(
        _proj_out_body,
        grid=(B, S // ts),
        in_specs=[tok(D), tok(o.shape[-1]), _const_spec(w_out.shape)],
        out_specs=tok(D),
        out_shape=jax.ShapeDtypeStruct((B, S, D), F32),
        compiler_params=_params(2),
        name=name,
    )(x, o, w_out.astype(BF16))


def _gdn_layer(x, ng, w_in, conv_w, a_log, dt_bias, norm_g, w_out, *, ts=256, ts_out=512):
    B, S, D = x.shape
    nch = 2 * GDN_QK + GDN_V
    w_qkv = w_in[:, :nch].astype(BF16)
    w_gate = w_in[:, nch:nch + GDN_V].astype(BF16)
    w_ba = jnp.pad(w_in[:, nch + GDN_V:], ((0, 0), (0, LANES - 2 * GDN_HEADS))).astype(BF16)
    pad_a = (GDN_HEADS, LANES - 2 * GDN_HEADS)
    alog = jnp.pad(a_log.astype(F32), pad_a).reshape(1, LANES)
    dtb = jnp.pad(dt_bias.astype(F32), pad_a).reshape(1, LANES)

    tok = lambda w: pl.BlockSpec((1, ts, w), lambda b, j: (b, j, 0))
    q, k, v, gate, gb = pl.pallas_call(
        functools.partial(_gdn_in_body, ts=ts),
        grid=(B, S // ts),
        in_specs=[tok(D), _const_spec((1, D)), _const_spec(w_qkv.shape), _const_spec(w_gate.shape),
                  _const_spec(w_ba.shape), _const_spec(conv_w.shape), _const_spec((1, LANES)),
                  _const_spec((1, LANES))],
        out_specs=[tok(GDN_QK), tok(GDN_QK), tok(GDN_V), tok(GDN_V), tok(LANES)],
        out_shape=[jax.ShapeDtypeStruct((B, S, GDN_QK), BF16),
                   jax.ShapeDtypeStruct((B, S, GDN_QK), BF16),
                   jax.ShapeDtypeStruct((B, S, GDN_V), BF16),
                   jax.ShapeDtypeStruct((B, S, GDN_V), BF16),
                   jax.ShapeDtypeStruct((B, S, LANES), F32)],
        scratch_shapes=[pltpu.VMEM((CONV_HALO + ts, nch), F32)],
        compiler_params=_params(2),
        name="gdn_in",
    )(x, ng.reshape(1, D), w_qkv, w_gate, w_ba, conv_w, alog, dtb)

    gbt = jnp.transpose(gb[:, :, :2 * GDN_HEADS], (0, 2, 1))
    n = GDN_UNIT
    nb = 2 if B % 2 == 0 else 1
    utok = lambda w: pl.BlockSpec((nb, n, w), lambda b, j: (b, j, 0))
    o = pl.pallas_call(
        _gdn_core_body,
        grid=(B // nb, S // n),
        in_specs=[utok(GDN_QK), utok(GDN_QK), utok(GDN_V), utok(LANES),
                  pl.BlockSpec((nb, 2 * GDN_HEADS, n), lambda b, j: (b, 0, j)),
                  utok(GDN_V), _const_spec((1, GDN_DV))],
        out_specs=utok(GDN_V),
        out_shape=jax.ShapeDtypeStruct((B, S, GDN_V), BF16),
        scratch_shapes=[pltpu.VMEM((nb * GDN_HEADS, GDN_DK, GDN_DV), F32)],
        compiler_params=_params(2),
        name="gdn_core",
    )(q, k, v, gb, gbt, gate, norm_g.reshape(1, GDN_DV))

    return _proj_out(x, o, w_out, ts=ts_out, name="gdn_out")


def _mla_in_body(x_ref, pos_ref, ng_ref, win_ref, qg_ref, wuq_ref, kvg_ref, wukv_ref, inv_ref,
                 q_ref, k_ref, v_ref, gate_ref):
    c1 = MLA_Q_LORA
    c2 = c1 + MLA_KV_LORA
    c3 = c2 + LANES
    x = x_ref[0]
    h = _rms(x, ng_ref[...]).astype(BF16)
    cq = jnp.dot(h, win_ref[:, :c1], preferred_element_type=F32)
    ckv = jnp.dot(h, win_ref[:, c1:c2], preferred_element_type=F32)
    kr = jnp.dot(h, win_ref[:, c2:c3], preferred_element_type=F32)
    gate = jnp.dot(h, win_ref[:, c3:], preferred_element_type=F32)
    gate_ref[0] = _silu(gate).astype(BF16)

    ang = pos_ref[0].astype(F32) * inv_ref[...]
    cos = jnp.cos(ang)
    sin = jnp.sin(ang)
    lane = lax.broadcasted_iota(jnp.int32, (1, LANES), 1)
    quarter = lane // (MLA_ROPE // 2)
    sin_s = sin * jnp.where((quarter == 0) | (quarter == 3), -1.0, 1.0)

    def rope(t):
        return t * cos + pltpu.roll(t, MLA_ROPE, 1) * sin_s

    k_rope = jnp.where(lane < MLA_ROPE, rope(kr), 0.0).astype(BF16)
    qf = jnp.dot(_rms(cq, qg_ref[...]).astype(BF16), wuq_ref[...],
                 preferred_element_type=F32) * (MLA_QK ** -0.5 * LOG2E)
    kv = jnp.dot(_rms(ckv, kvg_ref[...]).astype(BF16), wukv_ref[...], preferred_element_type=F32)
    for hh in range(MLA_HEADS):
        q0 = hh * MLA_QPAD
        q_ref[0, hh, :, 0:MLA_NOPE] = qf[:, q0:q0 + MLA_NOPE].astype(BF16)
        q_ref[0, hh, :, MLA_NOPE:] = rope(qf[:, q0 + MLA_NOPE:q0 + MLA_QPAD]).astype(BF16)
        k0 = hh * (MLA_NOPE + MLA_V)
        k_ref[0, hh, :, 0:MLA_NOPE] = kv[:, k0:k0 + MLA_NOPE].astype(BF16)
        k_ref[0, hh, :, MLA_NOPE:] = k_rope
        v_ref[0, hh, :, :MLA_V] = kv[:, k0 + MLA_NOPE:k0 + MLA_NOPE + MLA_V].astype(BF16)
        v_ref[0, hh, :, MLA_V:] = jnp.ones((kv.shape[0], LANES), BF16)


def _attn_body(q_ref, k_ref, v_ref, gate_ref, o_ref, s_ref, p_ref, m_ref, a_ref, acc_ref, *, tq, tk):
    i = pl.program_id(2)
    nh = q_ref.shape[1]
    m_ref[...] = jnp.full(m_ref.shape, NEG, F32)
    acc_ref[...] = jnp.zeros(acc_ref.shape, F32)

    def step(c0, width, masked):
        for h in range(nh):
            s_ref[h, :, :width] = lax.dot_general(
                q_ref[0, h], k_ref[0, h, pl.ds(c0, width), :], (((1,), (1,)), ((), ())),
                preferred_element_type=F32)
        for h in range(nh):
            for r0 in range(0, tq, ATTN_STRIP):
                rs = slice(r0, r0 + ATTN_STRIP)
                s = s_ref[h, rs, :width]
                if masked:
                    row = i * tq + r0 + lax.broadcasted_iota(jnp.int32, (ATTN_STRIP, 1), 0)
                    col = c0 + lax.broadcasted_iota(jnp.int32, (1, width), 1)
                    s = jnp.where(col <= row, s, NEG)
                m_old = m_ref[h, rs, :]
                m_new = jnp.maximum(m_old, jnp.max(s, axis=-1, keepdims=True))
                a_ref[h, rs, :] = jnp.exp2(m_old - m_new)
                m_ref[h, rs, :] = m_new
                p_ref[h, rs, :width] = jnp.concatenate(
                    [jnp.exp2(s[:, c:c + LANES] - m_new) for c in range(0, width, LANES)],
                    axis=1).astype(BF16)
        for h in range(nh):
            pv = jnp.dot(p_ref[h, :, :width], v_ref[0, h, pl.ds(c0, width), :],
                         preferred_element_type=F32)
            a = a_ref[h]
            acc_ref[h, :, :MLA_V] = a * acc_ref[h, :, :MLA_V] + pv[:, :MLA_V]
            acc_ref[h, :, MLA_V:] = a * acc_ref[h, :, MLA_V:] + pv[:, MLA_V:]

    n_wide = (i * tq) // tk

    def wide_step(kb, carry):
        step(pl.multiple_of(kb * tk, tk), tk, False)
        return carry

    lax.fori_loop(0, n_wide, wide_step, 0)
    for d in range(tk // tq - 1):
        @pl.when(n_wide * tk + d * tq < i * tq)
        def _():
            step(pl.multiple_of(n_wide * tk + d * tq, tq), tq, False)
    step(pl.multiple_of(i * tq, tq), tq, True)
    for h in range(nh):
        cols = slice(h * MLA_V, (h + 1) * MLA_V)
        o = acc_ref[h, :, :MLA_V] / acc_ref[h, :, MLA_V:]
        o_ref[0, :, cols] = (o * gate_ref[0, :, cols].astype(F32)).astype(o_ref.dtype)


def _mla_layer(x, pos, ng, w_in, q_norm_g, w_uq, kv_norm_g, w_ukv, w_out, *, ts=256, ts_out=512, tq=512, tk=1024,
               nh=4):
    B, S, D = x.shape
    H = MLA_HEADS
    half = MLA_ROPE // 2
    c1 = MLA_Q_LORA
    c2 = c1 + MLA_KV_LORA
    c3 = c2 + MLA_ROPE
    w_kr = w_in[:, c2:c3]
    w_kr = jnp.concatenate([w_kr, w_kr[:, half:], w_kr[:, :half]], axis=1)
    w_in_p = jnp.concatenate([w_in[:, :c2], w_kr, w_in[:, c3:]], axis=1).astype(BF16)
    wq = w_uq.reshape(MLA_Q_LORA, H, MLA_QK)
    wq_r = wq[:, :, MLA_NOPE:]
    wq_p = jnp.concatenate([wq, wq_r[:, :, half:], wq_r[:, :, :half]], axis=2)
    wq_p = wq_p.reshape(MLA_Q_LORA, H * MLA_QPAD).astype(BF16)
    inv = ROPE_THETA ** (-jnp.arange(half, dtype=F32) / half)
    inv = jnp.tile(inv, LANES // half).reshape(1, LANES)
    width = H * MLA_V

    tok = lambda w: pl.BlockSpec((1, ts, w), lambda b, j: (b, j, 0))
    head = lambda w: pl.BlockSpec((1, H, ts, w), lambda b, j: (b, 0, j, 0))
    q, k, v, gate = pl.pallas_call(
        _mla_in_body,
        grid=(B, S // ts),
        in_specs=[tok(D), tok(1), _const_spec((1, D)), _const_spec(w_in_p.shape),
                  _const_spec((1, MLA_Q_LORA)), _const_spec(wq_p.shape),
                  _const_spec((1, MLA_KV_LORA)), _const_spec(w_ukv.shape), _const_spec((1, LANES))],
        out_specs=[head(MLA_QPAD), head(MLA_QPAD), head(MLA_V + LANES), tok(width)],
        out_shape=[jax.ShapeDtypeStruct((B, H, S, MLA_QPAD), BF16),
                   jax.ShapeDtypeStruct((B, H, S, MLA_QPAD), BF16),
                   jax.ShapeDtypeStruct((B, H, S, MLA_V + LANES), BF16),
                   jax.ShapeDtypeStruct((B, S, width), BF16)],
        compiler_params=_params(2),
        name="mla_in",
    )(x, pos.reshape(B, S, 1), ng.reshape(1, D), w_in_p, q_norm_g.reshape(1, MLA_Q_LORA), wq_p,
      kv_norm_g.reshape(1, MLA_KV_LORA), w_ukv.astype(BF16), inv)

    o = pl.pallas_call(
        functools.partial(_attn_body, tq=tq, tk=tk),
        grid=(B, H // nh, S // tq),
        in_specs=[pl.BlockSpec((1, nh, tq, MLA_QPAD), lambda b, h, i: (b, h, i, 0)),
                  pl.BlockSpec((1, nh, S, MLA_QPAD), lambda b, h, i: (b, h, 0, 0)),
                  pl.BlockSpec((1, nh, S, MLA_V + LANES), lambda b, h, i: (b, h, 0, 0)),
                  pl.BlockSpec((1, tq, nh * MLA_V), lambda b, h, i: (b, i, h))],
        out_specs=pl.BlockSpec((1, tq, nh * MLA_V), lambda b, h, i: (b, i, h)),
        out_shape=jax.ShapeDtypeStruct((B, S, width), BF16),
        scratch_shapes=[pltpu.VMEM((nh, tq, tk), F32), pltpu.VMEM((nh, tq, tk), BF16),
                        pltpu.VMEM((nh, tq, LANES), F32), pltpu.VMEM((nh, tq, LANES), F32),
                        pltpu.VMEM((nh, tq, MLA_V + LANES), F32)],
        compiler_params=_params(3, ATTN_VMEM_LIMIT_V7X),
        name="mla_attn",
    )(q, k, v, gate)

    return _proj_out(x, o, w_out, ts=ts_out, name="mla_out")


def kernel(x, positions, norm_g, pool_w_in, pool_w_grp, pool_scale, pool_w_out, gdn_w_in, gdn_conv, gdn_a_log, gdn_dt_bias, gdn_norm_g, gdn_w_out, mla_w_in, mla_q_norm_g, mla_w_uq, mla_kv_norm_g, mla_w_ukv, mla_w_out, final_g):
    depth = norm_g.shape[0]
    pool_w = (pool_w_in.astype(BF16), pool_w_grp.astype(BF16), pool_w_out.astype(BF16))
    for i in range(depth):
        kind, j = i % N_MIXERS, i // N_MIXERS
        last = i == depth - 1
        if kind == 0:
            x = _pool_layer(x, norm_g[i], pool_w[0], pool_w[1], pool_scale[j], pool_w[2], j,
                            final_g if last else None)
        elif kind == 1:
            x = _gdn_layer(x, norm_g[i], gdn_w_in[j], gdn_conv[j], gdn_a_log[j], gdn_dt_bias[j],
                           gdn_norm_g[j], gdn_w_out[j])
        else:
            x = _mla_layer(x, positions, norm_g[i], mla_w_in[j], mla_q_norm_g[j], mla_w_uq[j],
                           mla_kv_norm_g[j], mla_w_ukv[j], mla_w_out[j])
        if last and kind != 0:
            x = _final_norm(x, final_g)
    return x


def _final_norm_body(x_ref, g_ref, o_ref):
    o_ref[0] = _rms(x_ref[0], g_ref[...])


def _final_norm(x, g, *, ts=512):
    B, S, D = x.shape
    tok = pl.BlockSpec((1, ts, D), lambda b, j: (b, j, 0))
    return pl.pallas_call(
        _final_norm_body, grid=(B, S // ts), in_specs=[tok, _const_spec((1, D))], out_specs=tok,
        out_shape=jax.ShapeDtypeStruct((B, S, D), F32), compiler_params=_params(2),
        name="final_norm",
    )(x, g.reshape(1, D))
```

```python
import functools

import jax
import jax.numpy as jnp
from jax import lax
from jax.experimental import pallas as pl
from jax.experimental.pallas import tpu as pltpu

F32 = jnp.float32
BF16 = jnp.bfloat16
EPS = 1e-6
NEG = -1e30
VMEM_LIMIT_V7X = 56 * 2 ** 20
ATTN_VMEM_LIMIT_V7X = 60 * 2 ** 20
LANES = 128
SUBLANES = 8

N_MIXERS = 3

POOL_WINDOWS = (2, 4, 8, 16)
POOL_HALO = SUBLANES * (max(POOL_WINDOWS).bit_length() - 1)

GDN_HEADS = 8
GDN_DK = 128
GDN_DV = 256
GDN_CONV = 4
GDN_CHUNK = 64
GDN_QK = GDN_HEADS * GDN_DK
GDN_V = GDN_HEADS * GDN_DV
GDN_UNIT = 2 * GDN_CHUNK
CONV_HALO = 8

MLA_HEADS = 16
MLA_NOPE = 128
MLA_ROPE = 64
MLA_V = 128
MLA_Q_LORA = 768
MLA_KV_LORA = 512
MLA_QK = MLA_NOPE + MLA_ROPE
MLA_QPAD = 256
ROPE_THETA = 10000.0
ATTN_STRIP = 32
LOG2E = 1.4426950408889634


def _const_spec(shape):
    zeros = (0,) * len(shape)
    return pl.BlockSpec(shape, lambda *_: zeros, pipeline_mode=pl.Buffered(1))


def _layer_spec(stacked_shape, layer):
    index = (layer,) + (0,) * (len(stacked_shape) - 1)
    return pl.BlockSpec((None,) + tuple(stacked_shape[1:]), lambda *_: index,
                        pipeline_mode=pl.Buffered(1))


def _params(n_axes, vmem_limit=VMEM_LIMIT_V7X):
    return pltpu.CompilerParams(dimension_semantics=("arbitrary",) * n_axes,
                                vmem_limit_bytes=vmem_limit)


def _rms(x, g):
    return x * lax.rsqrt(jnp.mean(x * x, axis=-1, keepdims=True) + EPS) * g


def _silu(x):
    h = 0.5 * x
    return h + h * jnp.tanh(h)


def _mm(a, b):
    return jnp.dot(a.astype(BF16), b.astype(BF16), preferred_element_type=F32)


def _mm_nt(a, b):
    return lax.dot_general(a.astype(BF16), b.astype(BF16), (((1,), (1,)), ((), ())),
                           preferred_element_type=F32)


def _pool_body(x_ref, ng_ref, win_ref, wgrp_ref, sc_ref, wout_ref, fg_ref, o_ref, ubuf_ref,
               sa_ref, sb_ref, *, ts, width, final):
    j = pl.program_id(1)
    grp = width // len(POOL_WINDOWS)

    def window_sum(w, c0):
        levels = w.bit_length() - 1
        src, cols, shift = ubuf_ref, slice(c0, c0 + grp), 1
        for lv in range(levels):
            r0 = POOL_HALO - SUBLANES * (levels - 1 - lv)
            n = POOL_HALO + ts - r0
            val = src[r0:r0 + n, cols] + src[r0 - shift:r0 - shift + n, cols]
            if lv == levels - 1:
                return val
            dst = (sa_ref, sb_ref)[lv % 2]
            dst[r0:r0 + n, :] = val
            src, cols, shift = dst, slice(0, grp), 2 * shift

    @pl.when(j == 0)
    def _():
        ubuf_ref[0:POOL_HALO, :] = jnp.zeros((POOL_HALO, width), F32)

    x = x_ref[0]
    h = _rms(x, ng_ref[...]).astype(BF16)
    u = jnp.dot(h, win_ref[:, :width], preferred_element_type=F32)
    gate = jnp.dot(h, win_ref[:, width:], preferred_element_type=F32)
    ubuf_ref[POOL_HALO:POOL_HALO + ts, :] = u
    t = j * ts + lax.broadcasted_iota(jnp.int32, (ts, 1), 0)
    parts = []
    for gi, w in enumerate(POOL_WINDOWS):
        c0 = gi * grp
        cnt = jnp.minimum(t + 1, w).astype(F32)
        p = window_sum(w, c0) / cnt - u[:, c0:c0 + grp]
        parts.append(jnp.dot(p.astype(BF16), wgrp_ref[gi], preferred_element_type=F32))
    pg = jnp.concatenate(parts, axis=1)
    y = pg * sc_ref[...] * _silu(gate)
    out = x + jnp.dot(y.astype(BF16), wout_ref[...], preferred_element_type=F32)
    ubuf_ref[0:POOL_HALO, :] = u[ts - POOL_HALO:, :]
    if final:
        out = _rms(out, fg_ref[...])
    o_ref[0] = out


def _pool_layer(x, ng, w_in, w_grp, scale, w_out, layer, final_g, *, ts=512):
    B, S, D = x.shape
    width = w_out.shape[1]
    final = final_g is not None
    fg = final_g if final else jnp.ones((D,), F32)
    body = functools.partial(_pool_body, ts=ts, width=width, final=final)
    return pl.pallas_call(
        body,
        grid=(B, S // ts),
        in_specs=[
            pl.BlockSpec((1, ts, D), lambda b, j: (b, j, 0)),
            _const_spec((1, D)),
            _layer_spec(w_in.shape, layer),
            _layer_spec(w_grp.shape, layer),
            _const_spec((1, width)),
            _layer_spec(w_out.shape, layer),
            _const_spec((1, D)),
        ],
        out_specs=pl.BlockSpec((1, ts, D), lambda b, j: (b, j, 0)),
        out_shape=jax.ShapeDtypeStruct((B, S, D), F32),
        scratch_shapes=[pltpu.VMEM((POOL_HALO + ts, width), F32),
                        pltpu.VMEM((POOL_HALO + ts, width // len(POOL_WINDOWS)), F32),
                        pltpu.VMEM((POOL_HALO + ts, width // len(POOL_WINDOWS)), F32)],
        compiler_params=_params(2),
        name="pool_layer",
    )(x, ng.reshape(1, D), w_in, w_grp, scale.reshape(1, width), w_out, fg.reshape(1, D))


def _gdn_in_body(x_ref, ng_ref, win_ref, wba_ref, conv_ref, alog_ref, dtb_ref,
                 q_ref, k_ref, v_ref, gate_ref, gb_ref, cbuf_ref, *, ts):
    j = pl.program_id(1)
    nch = 2 * GDN_QK + GDN_V
    wqkv_ref = win_ref.at[:, :nch]
    wgate_ref = win_ref.at[:, nch:nch + GDN_V]

    @pl.when(j == 0)
    def _():
        cbuf_ref[0:CONV_HALO, :] = jnp.zeros((CONV_HALO, nch), F32)

    x = x_ref[0]
    h = _rms(x, ng_ref[...]).astype(BF16)
    pre = jnp.dot(h, wqkv_ref[...], preferred_element_type=F32)
    cbuf_ref[CONV_HALO:CONV_HALO + ts, :] = pre
    acc = pre * conv_ref[GDN_CONV - 1:GDN_CONV, :]
    for kk in range(GDN_CONV - 1):
        r0 = CONV_HALO - (GDN_CONV - 1) + kk
        acc = acc + cbuf_ref[r0:r0 + ts, :] * conv_ref[kk:kk + 1, :]
    cbuf_ref[0:CONV_HALO, :] = pre[ts - CONV_HALO:, :]
    a = _silu(acc)
    for hh in range(GDN_HEADS):
        c = slice(hh * GDN_DK, (hh + 1) * GDN_DK)
        qh = a[:, c]
        qn = qh * lax.rsqrt(jnp.sum(qh * qh, axis=-1, keepdims=True) + EPS) * (GDN_DK ** -0.5)
        q_ref[0, :, c] = qn.astype(BF16)
        kh = a[:, GDN_QK + hh * GDN_DK:GDN_QK + (hh + 1) * GDN_DK]
        kn = kh * lax.rsqrt(jnp.sum(kh * kh, axis=-1, keepdims=True) + EPS)
        k_ref[0, :, c] = kn.astype(BF16)
    v_ref[0] = a[:, 2 * GDN_QK:].astype(BF16)
    gate_ref[0] = _silu(jnp.dot(h, wgate_ref[...], preferred_element_type=F32)).astype(BF16)

    ba = jnp.dot(h, wba_ref[...], preferred_element_type=F32)
    lane = lax.broadcasted_iota(jnp.int32, (1, LANES), 1)
    z = ba + dtb_ref[...]
    softplus = jnp.maximum(z, 0.0) + jnp.log1p(jnp.exp(-jnp.abs(z)))
    g = jnp.where((lane >= GDN_HEADS) & (lane < 2 * GDN_HEADS),
                  -jnp.exp(alog_ref[...]) * softplus, 0.0)
    r = lax.broadcasted_iota(jnp.int32, (ts, 1), 0)
    c = lax.broadcasted_iota(jnp.int32, (1, ts), 1)
    tri = jnp.where(((r // GDN_CHUNK) == (c // GDN_CHUNK)) & (c <= r), 1.0, 0.0).astype(BF16)
    g_hi = g.astype(BF16)
    r1 = g - g_hi.astype(F32)
    g_mid = r1.astype(BF16)
    g_lo = (r1 - g_mid.astype(F32)).astype(BF16)
    gc = (jnp.dot(tri, g_hi, preferred_element_type=F32)
          + jnp.dot(tri, g_mid, preferred_element_type=F32)
          + jnp.dot(tri, g_lo, preferred_element_type=F32))
    gb_ref[0] = jnp.where(lane < GDN_HEADS, jax.nn.sigmoid(ba), gc)


def _gdn_core_body(q_ref, k_ref, v_ref, gb_ref, gbt_ref, gate_ref, ng_ref, o_ref, s_ref):
    j = pl.program_id(1)

    @pl.when(j == 0)
    def _():
        s_ref[...] = jnp.zeros_like(s_ref)

    n = GDN_UNIT
    r = lax.broadcasted_iota(jnp.int32, (n, 1), 0)
    c = lax.broadcasted_iota(jnp.int32, (1, n), 1)
    same = (r // GDN_CHUNK) == (c // GDN_CHUNK)
    causal = same & (c <= r)
    strict = same & (c < r)
    eye = jnp.where(r == c, 1.0, 0.0)
    zeros_c = jnp.zeros((GDN_CHUNK, GDN_DV), F32)
    nb = q_ref.shape[0]
    hs = range(nb * GDN_HEADS)
    bi = [ci // GDN_HEADS for ci in hs]
    hi = [ci % GDN_HEADS for ci in hs]
    gb = [gb_ref[b] for b in range(nb)]
    gbt = [gbt_ref[b] for b in range(nb)]
    beta_c = [gb[bi[h]][:, hi[h]:hi[h] + 1] for h in hs]
    gc_c = [gb[bi[h]][:, GDN_HEADS + hi[h]:GDN_HEADS + hi[h] + 1] for h in hs]
    gc_r = [gbt[bi[h]][GDN_HEADS + hi[h]:GDN_HEADS + hi[h] + 1, :] for h in hs]
    q = [q_ref[bi[h], :, hi[h] * GDN_DK:(hi[h] + 1) * GDN_DK] for h in hs]
    k = [k_ref[bi[h], :, hi[h] * GDN_DK:(hi[h] + 1) * GDN_DK] for h in hs]
    kf = [k[h].astype(F32) for h in hs]
    kb = [kf[h] * beta_c[h] for h in hs]
    decay = [jnp.exp(jnp.where(causal, gc_c[h] - gc_r[h], NEG)) for h in hs]
    kkqk = [_mm_nt(jnp.concatenate([kb[h].astype(BF16), q[h]], axis=0), k[h]) for h in hs]
    lmat = [jnp.where(strict, kkqk[h][:n] * decay[h], 0.0) for h in hs]
    attn = [kkqk[h][n:] * decay[h] for h in hs]
    ch = GDN_CHUNK
    eye_s = eye[:ch] + eye[ch:]

    def block_diag(side):
        return jnp.where(same, jnp.concatenate([side, side], axis=0), 0.0)

    l_s = [lmat[h][:ch] + lmat[h][ch:] for h in hs]
    xs = [eye_s - l_s[h] for h in hs]
    ps = [_mm(l_s[h], lmat[h]) for h in hs]
    for _ in range(4):
        res = [_mm(jnp.concatenate([xs[h], ps[h]], axis=0), block_diag(ps[h])) for h in hs]
        xs = [xs[h] + res[h][:ch] for h in hs]
        ps = [res[h][ch:] for h in hs]
    res = [_mm(xs[h], block_diag(ps[h])) for h in hs]
    xm = [block_diag(xs[h] + res[h]) for h in hs]
    eg = [jnp.exp(gc_c[h]) for h in hs]
    rhs = [jnp.concatenate(
        [v_ref[bi[h], :, hi[h] * GDN_DV:(hi[h] + 1) * GDN_DV].astype(F32) * beta_c[h],
         kb[h] * eg[h]], axis=1) for h in hs]
    sol = [rhs[h] + _mm(xm[h] - eye, rhs[h]) for h in hs]
    qg = [q[h].astype(F32) * eg[h] for h in hs]
    k_t = [kf[h].T for h in hs]
    state = [s_ref[h] for h in hs]
    for cc in range(2):
        rows = slice(cc * GDN_CHUNK, (cc + 1) * GDN_CHUNK)
        in_chunk = (c // GDN_CHUNK) == cc
        g_last = [gc_r[h][:, (cc + 1) * GDN_CHUNK - 1:(cc + 1) * GDN_CHUNK] for h in hs]
        res = [_mm(jnp.concatenate([sol[h][rows, GDN_DV:], qg[h][rows]], axis=0), state[h])
               for h in hs]
        v_new = [sol[h][rows, :GDN_DV] - res[h][:GDN_CHUNK] for h in hs]
        v_all = [jnp.concatenate([v_new[h], zeros_c] if cc == 0 else [zeros_c, v_new[h]], axis=0)
                 for h in hs]
        k_dec_t = [k_t[h] * jnp.exp(jnp.where(in_chunk, g_last[h] - gc_r[h], NEG)) for h in hs]
        upd = [_mm(jnp.concatenate([attn[h][rows], k_dec_t[h]], axis=0), v_all[h]) for h in hs]
        o_c = [res[h][GDN_CHUNK:] + upd[h][:GDN_CHUNK] for h in hs]
        state = [state[h] * jnp.exp(g_last[h]) + upd[h][GDN_CHUNK:] for h in hs]
        for h in hs:
            cols = slice(hi[h] * GDN_DV, (hi[h] + 1) * GDN_DV)
            gated = _rms(o_c[h], ng_ref[...]) * gate_ref[bi[h], rows, cols].astype(F32)
            o_ref[bi[h], rows, cols] = gated.astype(o_ref.dtype)
    for h in hs:
        s_ref[h] = state[h]


def _proj_out_body(x_ref, o_ref, wout_ref, y_ref):
    y_ref[0] = x_ref[0] + jnp.dot(o_ref[0], wout_ref[...], preferred_element_type=F32)


def _proj_out(x, o, w_out, *, ts, name):
    B, S, D = x.shape
    tok = lambda w: pl.BlockSpec((1, ts, w), lambda b, j: (b, j, 0))
    return pl.pallas_call(
        _proj_out_body,
        grid=(B, S // ts),
        in_specs=[tok(D), tok(o.shape[-1]), _const_spec(w_out.shape)],
        out_specs=tok(D),
        out_shape=jax.ShapeDtypeStruct((B, S, D), F32),
        compiler_params=_params(2),
        name=name,
    )(x, o, w_out.astype(BF16))


def _gdn_layer(x, ng, w_in, conv_w, a_log, dt_bias, norm_g, w_out, *, ts=256, ts_out=512):
    B, S, D = x.shape
    nch = 2 * GDN_QK + GDN_V
    w_all = w_in.astype(BF16)
    w_ba = jnp.pad(w_in[:, nch + GDN_V:], ((0, 0), (0, LANES - 2 * GDN_HEADS))).astype(BF16)
    pad_a = (GDN_HEADS, LANES - 2 * GDN_HEADS)
    alog = jnp.pad(a_log.astype(F32), pad_a).reshape(1, LANES)
    dtb = jnp.pad(dt_bias.astype(F32), pad_a).reshape(1, LANES)

    tok = lambda w: pl.BlockSpec((1, ts, w), lambda b, j: (b, j, 0))
    q, k, v, gate, gb = pl.pallas_call(
        functools.partial(_gdn_in_body, ts=ts),
        grid=(B, S // ts),
        in_specs=[tok(D), _const_spec((1, D)), _const_spec(w_all.shape), _const_spec(w_ba.shape), _const_spec(conv_w.shape), _const_spec((1, LANES)),
                  _const_spec((1, LANES))],
        out_specs=[tok(GDN_QK), tok(GDN_QK), tok(GDN_V), tok(GDN_V), tok(LANES)],
        out_shape=[jax.ShapeDtypeStruct((B, S, GDN_QK), BF16),
                   jax.ShapeDtypeStruct((B, S, GDN_QK), BF16),
                   jax.ShapeDtypeStruct((B, S, GDN_V), BF16),
                   jax.ShapeDtypeStruct((B, S, GDN_V), BF16),
                   jax.ShapeDtypeStruct((B, S, LANES), F32)],
        scratch_shapes=[pltpu.VMEM((CONV_HALO + ts, nch), F32)],
        compiler_params=_params(2),
        name="gdn_in",
    )(x, ng.reshape(1, D), w_all, w_ba, conv_w, alog, dtb)

    gbt = jnp.transpose(gb[:, :, :2 * GDN_HEADS], (0, 2, 1))
    n = GDN_UNIT
    nb = 2 if B % 2 == 0 else 1
    utok = lambda w: pl.BlockSpec((nb, n, w), lambda b, j: (b, j, 0))
    o = pl.pallas_call(
        _gdn_core_body,
        grid=(B // nb, S // n),
        in_specs=[utok(GDN_QK), utok(GDN_QK), utok(GDN_V), utok(LANES),
                  pl.BlockSpec((nb, 2 * GDN_HEADS, n), lambda b, j: (b, 0, j)),
                  utok(GDN_V), _const_spec((1, GDN_DV))],
        out_specs=utok(GDN_V),
        out_shape=jax.ShapeDtypeStruct((B, S, GDN_V), BF16),
        scratch_shapes=[pltpu.VMEM((nb * GDN_HEADS, GDN_DK, GDN_DV), F32)],
        compiler_params=_params(2),
        name="gdn_core",
    )(q, k, v, gb, gbt, gate, norm_g.reshape(1, GDN_DV))

    return _proj_out(x, o, w_out, ts=ts_out, name="gdn_out")


def _mla_in_body(x_ref, pos_ref, ng_ref, win_ref, qg_ref, wuq_ref, kvg_ref, wukv_ref, inv_ref,
                 q_ref, k_ref, v_ref, gate_ref):
    c1 = MLA_Q_LORA
    c2 = c1 + MLA_KV_LORA
    c3 = c2 + LANES
    x = x_ref[0]
    h = _rms(x, ng_ref[...]).astype(BF16)
    cq = jnp.dot(h, win_ref[:, :c1], preferred_element_type=F32)
    ckv = jnp.dot(h, win_ref[:, c1:c2], preferred_element_type=F32)
    kr = jnp.dot(h, win_ref[:, c2:c3], preferred_element_type=F32)
    gate = jnp.dot(h, win_ref[:, c3:], preferred_element_type=F32)
    gate_ref[0] = _silu(gate).astype(BF16)

    half = MLA_ROPE // 2
    ang = pos_ref[0].astype(F32) * inv_ref[...]
    cos = jnp.cos(ang)
    sin = jnp.sin(ang)
    lane = lax.broadcasted_iota(jnp.int32, (1, LANES), 1)
    is_x1 = (lane // half) % 2 == 0
    sin_s = sin * jnp.where(is_x1, -1.0, 1.0)

    def rope(t):
        partner = jnp.where(is_x1, pltpu.roll(t, LANES - half, 1), pltpu.roll(t, half, 1))
        return t * cos + partner * sin_s

    k_rope_t = rope(kr).T.astype(BF16)
    qf = jnp.dot(_rms(cq, qg_ref[...]).astype(BF16), wuq_ref[...],
                 preferred_element_type=F32) * (MLA_QK ** -0.5 * LOG2E)
    kv = jnp.dot(_rms(ckv, kvg_ref[...]).astype(BF16), wukv_ref[...], preferred_element_type=F32)
    nope_w = MLA_HEADS * MLA_NOPE
    q_pad = jnp.zeros((qf.shape[0], MLA_QPAD - MLA_QK), F32)
    for hh in range(MLA_HEADS):
        if hh % 2 == 0:
            r0 = nope_w + (hh // 2) * LANES
            roped = rope(qf[:, r0:r0 + LANES])
        e = (hh % 2) * MLA_ROPE
        q_ref[0, hh] = jnp.concatenate(
            [qf[:, hh * MLA_NOPE:(hh + 1) * MLA_NOPE], roped[:, e:e + MLA_ROPE], q_pad],
            axis=1).astype(BF16)
        k0 = hh * (MLA_NOPE + MLA_V)
        k_ref[0, hh, 0:MLA_NOPE, :] = kv[:, k0:k0 + MLA_NOPE].T.astype(BF16)
        k_ref[0, hh, MLA_NOPE:, :] = k_rope_t
        v_ref[0, hh, :, :MLA_V] = kv[:, k0 + MLA_NOPE:k0 + MLA_NOPE + MLA_V].astype(BF16)
        v_ref[0, hh, :, MLA_V:] = jnp.ones((kv.shape[0], LANES), BF16)


def _attn_body(q_ref, k_ref, v_ref, gate_ref, o_ref, s_ref, p_ref, m_ref, a_ref, acc_ref, *, tq, tk):
    i = pl.program_id(2)
    nh = q_ref.shape[1]
    m_ref[...] = jnp.full(m_ref.shape, NEG, F32)
    acc_ref[...] = jnp.zeros(acc_ref.shape, F32)

    def step(c0, width, masked):
        for h in range(nh):
            s_ref[h, :, :width] = jnp.dot(q_ref[0, h], k_ref[0, h, :, pl.ds(c0, width)],
                                          preferred_element_type=F32)
        for h in range(nh):
            for r0 in range(0, tq, ATTN_STRIP):
                rs = slice(r0, r0 + ATTN_STRIP)
                s = s_ref[h, rs, :width]
                if masked:
                    row = i * tq + r0 + lax.broadcasted_iota(jnp.int32, (ATTN_STRIP, 1), 0)
                    col = c0 + lax.broadcasted_iota(jnp.int32, (1, width), 1)
                    s = jnp.where(col <= row, s, NEG)
                m_old = m_ref[h, rs, :]
                m_new = jnp.maximum(m_old, jnp.max(s, axis=-1, keepdims=True))
                a_ref[h, rs, :] = jnp.exp2(m_old - m_new)
                m_ref[h, rs, :] = m_new
                p_ref[h, rs, :width] = jnp.concatenate(
                    [jnp.exp2(s[:, c:c + LANES] - m_new) for c in range(0, width, LANES)],
                    axis=1).astype(BF16)
        for h in range(nh):
            pv = jnp.dot(p_ref[h, :, :width], v_ref[0, h, pl.ds(c0, width), :],
                         preferred_element_type=F32)
            a = a_ref[h]
            acc_ref[h, :, :MLA_V] = a * acc_ref[h, :, :MLA_V] + pv[:, :MLA_V]
            acc_ref[h, :, MLA_V:] = a * acc_ref[h, :, MLA_V:] + pv[:, MLA_V:]

    n_wide = (i * tq) // tk

    def wide_step(kb, carry):
        step(pl.multiple_of(kb * tk, tk), tk, False)
        return carry

    lax.fori_loop(0, n_wide, wide_step, 0)
    for d in range(tk // tq - 1):
        @pl.when(n_wide * tk + d * tq < i * tq)
        def _():
            step(pl.multiple_of(n_wide * tk + d * tq, tq), tq, False)
    step(pl.multiple_of(i * tq, tq), tq, True)
    for h in range(nh):
        cols = slice(h * MLA_V, (h + 1) * MLA_V)
        o = acc_ref[h, :, :MLA_V] / acc_ref[h, :, MLA_V:]
        o_ref[0, :, cols] = (o * gate_ref[0, :, cols].astype(F32)).astype(o_ref.dtype)


def _mla_layer(x, pos, ng, w_in, q_norm_g, w_uq, kv_norm_g, w_ukv, w_out, *, ts=256, ts_out=512, tq=512, tk=1024,
               nh=4):
    B, S, D = x.shape
    H = MLA_HEADS
    half = MLA_ROPE // 2
    c1 = MLA_Q_LORA
    c2 = c1 + MLA_KV_LORA
    c3 = c2 + MLA_ROPE
    w_in_p = jnp.concatenate(
        [w_in[:, :c3], jnp.zeros((D, LANES - MLA_ROPE), w_in.dtype), w_in[:, c3:]],
        axis=1).astype(BF16)
    wq = w_uq.reshape(MLA_Q_LORA, H, MLA_QK)
    wq_p = jnp.concatenate([wq[:, :, :MLA_NOPE].reshape(MLA_Q_LORA, H * MLA_NOPE),
                            wq[:, :, MLA_NOPE:].reshape(MLA_Q_LORA, H * MLA_ROPE)],
                           axis=1).astype(BF16)
    inv = ROPE_THETA ** (-jnp.arange(half, dtype=F32) / half)
    inv = jnp.tile(inv, LANES // half).reshape(1, LANES)
    width = H * MLA_V

    tok = lambda w: pl.BlockSpec((1, ts, w), lambda b, j: (b, j, 0))
    head = lambda w: pl.BlockSpec((1, H, ts, w), lambda b, j: (b, 0, j, 0))
    q, k, v, gate = pl.pallas_call(
        _mla_in_body,
        grid=(B, S // ts),
        in_specs=[tok(D), tok(1), _const_spec((1, D)), _const_spec(w_in_p.shape),
                  _const_spec((1, MLA_Q_LORA)), _const_spec(wq_p.shape),
                  _const_spec((1, MLA_KV_LORA)), _const_spec(w_ukv.shape), _const_spec((1, LANES))],
        out_specs=[head(MLA_QPAD), pl.BlockSpec((1, H, MLA_QPAD, ts), lambda b, j: (b, 0, 0, j)),
                   head(MLA_V + LANES), tok(width)],
        out_shape=[jax.ShapeDtypeStruct((B, H, S, MLA_QPAD), BF16),
                   jax.ShapeDtypeStruct((B, H, MLA_QPAD, S), BF16),
                   jax.ShapeDtypeStruct((B, H, S, MLA_V + LANES), BF16),
                   jax.ShapeDtypeStruct((B, S, width), BF16)],
        compiler_params=_params(2),
        name="mla_in",
    )(x, pos.reshape(B, S, 1), ng.reshape(1, D), w_in_p, q_norm_g.reshape(1, MLA_Q_LORA), wq_p,
      kv_norm_g.reshape(1, MLA_KV_LORA), w_ukv.astype(BF16), inv)

    o = pl.pallas_call(
        functools.partial(_attn_body, tq=tq, tk=tk),
        grid=(B, H // nh, S // tq),
        in_specs=[pl.BlockSpec((1, nh, tq, MLA_QPAD), lambda b, h, i: (b, h, i, 0)),
                  pl.BlockSpec((1, nh, MLA_QPAD, S), lambda b, h, i: (b, h, 0, 0)),
                  pl.BlockSpec((1, nh, S, MLA_V + LANES), lambda b, h, i: (b, h, 0, 0)),
                  pl.BlockSpec((1, tq, nh * MLA_V), lambda b, h, i: (b, i, h))],
        out_specs=pl.BlockSpec((1, tq, nh * MLA_V), lambda b, h, i: (b, i, h)),
        out_shape=jax.ShapeDtypeStruct((B, S, width), BF16),
        scratch_shapes=[pltpu.VMEM((nh, tq, tk), F32), pltpu.VMEM((nh, tq, tk), BF16),
                        pltpu.VMEM((nh, tq, LANES), F32), pltpu.VMEM((nh, tq, LANES), F32),
                        pltpu.VMEM((nh, tq, MLA_V + LANES), F32)],
        compiler_params=_params(3, ATTN_VMEM_LIMIT_V7X),
        name="mla_attn",
    )(q, k, v, gate)

    return _proj_out(x, o, w_out, ts=ts_out, name="mla_out")


def kernel(x, positions, norm_g, pool_w_in, pool_w_grp, pool_scale, pool_w_out, gdn_w_in, gdn_conv, gdn_a_log, gdn_dt_bias, gdn_norm_g, gdn_w_out, mla_w_in, mla_q_norm_g, mla_w_uq, mla_kv_norm_g, mla_w_ukv, mla_w_out, final_g):
    depth = norm_g.shape[0]
    pool_w = (pool_w_in.astype(BF16), pool_w_grp.astype(BF16), pool_w_out.astype(BF16))
    for i in range(depth):
        kind, j = i % N_MIXERS, i // N_MIXERS
        last = i == depth - 1
        if kind == 0:
            x = _pool_layer(x, norm_g[i], pool_w[0], pool_w[1], pool_scale[j], pool_w[2], j,
                            final_g if last else None)
        elif kind == 1:
            x = _gdn_layer(x, norm_g[i], gdn_w_in[j], gdn_conv[j], gdn_a_log[j], gdn_dt_bias[j],
                           gdn_norm_g[j], gdn_w_out[j])
        else:
            x = _mla_layer(x, positions, norm_g[i], mla_w_in[j], mla_q_norm_g[j], mla_w_uq[j],
                           mla_kv_norm_g[j], mla_w_ukv[j], mla_w_out[j])
        if last and kind != 0:
            x = _final_norm(x, final_g)
    return x


def _final_norm_body(x_ref, g_ref, o_ref):
    o_ref[0] = _rms(x_ref[0], g_ref[...])


def _final_norm(x, g, *, ts=512):
    B, S, D = x.shape
    tok = pl.BlockSpec((1, ts, D), lambda b, j: (b, j, 0))
    return pl.pallas_call(
        _final_norm_body, grid=(B, S // ts), in_specs=[tok, _const_spec((1, D))], out_specs=tok,
        out_shape=jax.ShapeDtypeStruct((B, S, D), F32), compiler_params=_params(2),
        name="final_norm",
    )(x, g.reshape(1, D))
```

```python
import functools

import jax
import jax.numpy as jnp
from jax import lax
from jax.experimental import pallas as pl
from jax.experimental.pallas import tpu as pltpu

F32 = jnp.float32
BF16 = jnp.bfloat16
EPS = 1e-6
NEG = -1e30
VMEM_LIMIT_V7X = 56 * 2 ** 20
ATTN_VMEM_LIMIT_V7X = 60 * 2 ** 20
LANES = 128
SUBLANES = 8
MXU_COLS = 256

N_MIXERS = 3

POOL_WINDOWS = (2, 4, 8, 16)
POOL_HALO = SUBLANES * (max(POOL_WINDOWS).bit_length() - 1)

GDN_HEADS = 8
GDN_DK = 128
GDN_DV = 256
GDN_CONV = 4
GDN_CHUNK = 64
GDN_QK = GDN_HEADS * GDN_DK
GDN_V = GDN_HEADS * GDN_DV
GDN_UNIT = 2 * GDN_CHUNK
CONV_HALO = 8

MLA_HEADS = 16
MLA_NOPE = 128
MLA_ROPE = 64
MLA_V = 128
MLA_Q_LORA = 768
MLA_KV_LORA = 512
MLA_QK = MLA_NOPE + MLA_ROPE
MLA_QPAD = 256
ROPE_THETA = 10000.0
ATTN_STRIP = 32
LOG2E = 1.4426950408889634


def _const_spec(shape):
    zeros = (0,) * len(shape)
    return pl.BlockSpec(shape, lambda *_: zeros, pipeline_mode=pl.Buffered(1))


def _layer_spec(stacked_shape, layer):
    index = (layer,) + (0,) * (len(stacked_shape) - 1)
    return pl.BlockSpec((None,) + tuple(stacked_shape[1:]), lambda *_: index,
                        pipeline_mode=pl.Buffered(1))


def _params(n_axes, vmem_limit=VMEM_LIMIT_V7X):
    return pltpu.CompilerParams(dimension_semantics=("arbitrary",) * n_axes,
                                vmem_limit_bytes=vmem_limit)


def _rms(x, g):
    return x * lax.rsqrt(jnp.mean(x * x, axis=-1, keepdims=True) + EPS) * g


def _silu(x):
    h = 0.5 * x
    return h + h * jnp.tanh(h)


def _mm(a, b):
    return jnp.dot(a.astype(BF16), b.astype(BF16), preferred_element_type=F32)


def _mm_nt(a, b):
    return lax.dot_general(a.astype(BF16), b.astype(BF16), (((1,), (1,)), ((), ())),
                           preferred_element_type=F32)


def _pool_body(x_ref, ng_ref, win_ref, wgrp_ref, sc_ref, wout_ref, fg_ref, o_ref, ubuf_ref,
               sa_ref, sb_ref, *, ts, nsub, width, final):
    j = pl.program_id(1)
    grp = width // len(POOL_WINDOWS)

    def window_sum(w, c0):
        levels = w.bit_length() - 1
        src, cols, shift = ubuf_ref, slice(c0, c0 + grp), 1
        for lv in range(levels):
            r0 = POOL_HALO - SUBLANES * (levels - 1 - lv)
            n = POOL_HALO + ts - r0
            val = src[r0:r0 + n, cols] + src[r0 - shift:r0 - shift + n, cols]
            if lv == levels - 1:
                return val
            dst = (sa_ref, sb_ref)[lv % 2]
            dst[r0:r0 + n, :] = val
            src, cols, shift = dst, slice(0, grp), 2 * shift

    @pl.when(j == 0)
    def _():
        ubuf_ref[0:POOL_HALO, :] = jnp.zeros((POOL_HALO, width), F32)

    for st in range(nsub):
        rows = slice(st * ts, (st + 1) * ts)
        x = x_ref[0, rows, :]
        h = _rms(x, ng_ref[...]).astype(BF16)
        u = jnp.dot(h, win_ref[:, :width], preferred_element_type=F32)
        gate = jnp.dot(h, win_ref[:, width:], preferred_element_type=F32)
        ubuf_ref[POOL_HALO:POOL_HALO + ts, :] = u
        t = (j * nsub + st) * ts + lax.broadcasted_iota(jnp.int32, (ts, 1), 0)
        parts = []
        for gi, w in enumerate(POOL_WINDOWS):
            c0 = gi * grp
            cnt = jnp.minimum(t + 1, w).astype(F32)
            p = window_sum(w, c0) / cnt - u[:, c0:c0 + grp]
            parts.append(jnp.dot(p.astype(BF16), wgrp_ref[gi], preferred_element_type=F32))
        pg = jnp.concatenate(parts, axis=1)
        y = pg * sc_ref[...] * _silu(gate)
        out = x + jnp.dot(y.astype(BF16), wout_ref[...], preferred_element_type=F32)
        ubuf_ref[0:POOL_HALO, :] = u[ts - POOL_HALO:, :]
        if final:
            out = _rms(out, fg_ref[...])
        o_ref[0, rows, :] = out


def _pool_layer(x, ng, w_in, w_grp, scale, w_out, layer, final_g, *, ts=512, nsub=2):
    B, S, D = x.shape
    width = w_out.shape[1]
    final = final_g is not None
    fg = final_g if final else jnp.ones((D,), F32)
    body = functools.partial(_pool_body, ts=ts, nsub=nsub, width=width, final=final)
    tb = ts * nsub
    return pl.pallas_call(
        body,
        grid=(B, S // tb),
        in_specs=[
            pl.BlockSpec((1, tb, D), lambda b, j: (b, j, 0)),
            _const_spec((1, D)),
            _layer_spec(w_in.shape, layer),
            _layer_spec(w_grp.shape, layer),
            _const_spec((1, width)),
            _layer_spec(w_out.shape, layer),
            _const_spec((1, D)),
        ],
        out_specs=pl.BlockSpec((1, tb, D), lambda b, j: (b, j, 0)),
        out_shape=jax.ShapeDtypeStruct((B, S, D), F32),
        scratch_shapes=[pltpu.VMEM((POOL_HALO + ts, width), F32),
                        pltpu.VMEM((POOL_HALO + ts, width // len(POOL_WINDOWS)), F32),
                        pltpu.VMEM((POOL_HALO + ts, width // len(POOL_WINDOWS)), F32)],
        compiler_params=_params(2),
        name="pool_layer",
    )(x, ng.reshape(1, D), w_in, w_grp, scale.reshape(1, width), w_out, fg.reshape(1, D))


def _gdn_in_body(x_ref, ng_ref, win_ref, wba_ref, conv_ref, alog_ref, dtb_ref,
                 q_ref, k_ref, v_ref, gate_ref, gb_ref, cbuf_ref, *, ts, nsub):
    j = pl.program_id(1)
    nch = 2 * GDN_QK + GDN_V

    @pl.when(j == 0)
    def _():
        cbuf_ref[0:CONV_HALO, :] = jnp.zeros((CONV_HALO, nch), F32)

    lane = lax.broadcasted_iota(jnp.int32, (1, LANES), 1)
    r = lax.broadcasted_iota(jnp.int32, (ts, 1), 0)
    c = lax.broadcasted_iota(jnp.int32, (1, ts), 1)
    tri = jnp.where(((r // GDN_CHUNK) == (c // GDN_CHUNK)) & (c <= r), 1.0, 0.0).astype(BF16)

    for st in range(nsub):
        rows = slice(st * ts, (st + 1) * ts)
        x = x_ref[0, rows, :]
        h = _rms(x, ng_ref[...]).astype(BF16)

        for c0 in range(0, nch, MXU_COLS):
            cols = slice(c0, c0 + MXU_COLS)
            pre = jnp.dot(h, win_ref[:, cols], preferred_element_type=F32)
            cbuf_ref[CONV_HALO:CONV_HALO + ts, cols] = pre
            acc = pre * conv_ref[GDN_CONV - 1:GDN_CONV, cols]
            for kk in range(GDN_CONV - 1):
                r0 = CONV_HALO - (GDN_CONV - 1) + kk
                acc = acc + cbuf_ref[r0:r0 + ts, cols] * conv_ref[kk:kk + 1, cols]
            cbuf_ref[0:CONV_HALO, cols] = pre[ts - CONV_HALO:, :]
            a = _silu(acc)
            if c0 < 2 * GDN_QK:
                is_q = c0 < GDN_QK
                out_ref, base = (q_ref, c0) if is_q else (k_ref, c0 - GDN_QK)
                for d0 in range(0, MXU_COLS, GDN_DK):
                    ah = a[:, d0:d0 + GDN_DK]
                    an = ah * lax.rsqrt(jnp.sum(ah * ah, axis=-1, keepdims=True) + EPS)
                    if is_q:
                        an = an * (GDN_DK ** -0.5)
                    out_ref[0, rows, base + d0:base + d0 + GDN_DK] = an.astype(BF16)
            else:
                v0 = c0 - 2 * GDN_QK
                v_ref[0, rows, v0:v0 + MXU_COLS] = a.astype(BF16)
        for c0 in range(0, GDN_V, MXU_COLS):
            gate = jnp.dot(h, win_ref[:, nch + c0:nch + c0 + MXU_COLS],
                           preferred_element_type=F32)
            gate_ref[0, rows, c0:c0 + MXU_COLS] = _silu(gate).astype(BF16)

        ba = jnp.dot(h, wba_ref[...], preferred_element_type=F32)
        z = ba + dtb_ref[...]
        softplus = jnp.maximum(z, 0.0) + jnp.log1p(jnp.exp(-jnp.abs(z)))
        g = jnp.where((lane >= GDN_HEADS) & (lane < 2 * GDN_HEADS),
                      -jnp.exp(alog_ref[...]) * softplus, 0.0)
        g_hi = g.astype(BF16)
        r1 = g - g_hi.astype(F32)
        g_mid = r1.astype(BF16)
        g_lo = (r1 - g_mid.astype(F32)).astype(BF16)
        gc = (jnp.dot(tri, g_hi, preferred_element_type=F32)
              + jnp.dot(tri, g_mid, preferred_element_type=F32)
              + jnp.dot(tri, g_lo, preferred_element_type=F32))
        gb_ref[0, rows, :] = jnp.where(lane < GDN_HEADS, jax.nn.sigmoid(ba), gc)


def _gdn_core_body(q_ref, k_ref, v_ref, gb_ref, gbt_ref, gate_ref, ng_ref, o_ref, s_ref):
    j = pl.program_id(1)

    @pl.when(j == 0)
    def _():
        s_ref[...] = jnp.zeros_like(s_ref)

    n = GDN_UNIT
    r = lax.broadcasted_iota(jnp.int32, (n, 1), 0)
    c = lax.broadcasted_iota(jnp.int32, (1, n), 1)
    same = (r // GDN_CHUNK) == (c // GDN_CHUNK)
    causal = same & (c <= r)
    strict = same & (c < r)
    eye = jnp.where(r == c, 1.0, 0.0)
    zeros_c = jnp.zeros((GDN_CHUNK, GDN_DV), F32)
    nb = q_ref.shape[0]
    hs = range(nb * GDN_HEADS)
    bi = [ci // GDN_HEADS for ci in hs]
    hi = [ci % GDN_HEADS for ci in hs]
    gb = [gb_ref[b] for b in range(nb)]
    gbt = [gbt_ref[b] for b in range(nb)]
    beta_c = [gb[bi[h]][:, hi[h]:hi[h] + 1] for h in hs]
    gc_c = [gb[bi[h]][:, GDN_HEADS + hi[h]:GDN_HEADS + hi[h] + 1] for h in hs]
    gc_r = [gbt[bi[h]][GDN_HEADS + hi[h]:GDN_HEADS + hi[h] + 1, :] for h in hs]
    q = [q_ref[bi[h], :, hi[h] * GDN_DK:(hi[h] + 1) * GDN_DK] for h in hs]
    k = [k_ref[bi[h], :, hi[h] * GDN_DK:(hi[h] + 1) * GDN_DK] for h in hs]
    kf = [k[h].astype(F32) for h in hs]
    kb = [kf[h] * beta_c[h] for h in hs]
    decay = [jnp.exp(jnp.where(causal, gc_c[h] - gc_r[h], NEG)) for h in hs]
    kkqk = [_mm_nt(jnp.concatenate([kb[h].astype(BF16), q[h]], axis=0), k[h]) for h in hs]
    lmat = [jnp.where(strict, kkqk[h][:n] * decay[h], 0.0) for h in hs]
    attn = [kkqk[h][n:] * decay[h] for h in hs]
    ch = GDN_CHUNK
    eye_s = eye[:ch] + eye[ch:]

    def block_diag(side):
        return jnp.where(same, jnp.concatenate([side, side], axis=0), 0.0)

    l_s = [lmat[h][:ch] + lmat[h][ch:] for h in hs]
    xs = [eye_s - l_s[h] for h in hs]
    ps = [_mm(l_s[h], lmat[h]) for h in hs]
    for _ in range(4):
        res = [_mm(jnp.concatenate([xs[h], ps[h]], axis=0), block_diag(ps[h])) for h in hs]
        xs = [xs[h] + res[h][:ch] for h in hs]
        ps = [res[h][ch:] for h in hs]
    res = [_mm(xs[h], block_diag(ps[h])) for h in hs]
    xm = [block_diag(xs[h] + res[h]) for h in hs]
    eg = [jnp.exp(gc_c[h]) for h in hs]
    rhs = [jnp.concatenate(
        [v_ref[bi[h], :, hi[h] * GDN_DV:(hi[h] + 1) * GDN_DV].astype(F32) * beta_c[h],
         kb[h] * eg[h]], axis=1) for h in hs]
    sol = [rhs[h] + _mm(xm[h] - eye, rhs[h]) for h in hs]
    qg = [q[h].astype(F32) * eg[h] for h in hs]
    k_t = [kf[h].T for h in hs]
    state = [s_ref[h] for h in hs]
    for cc in range(2):
        rows = slice(cc * GDN_CHUNK, (cc + 1) * GDN_CHUNK)
        in_chunk = (c // GDN_CHUNK) == cc
        g_last = [gc_r[h][:, (cc + 1) * GDN_CHUNK - 1:(cc + 1) * GDN_CHUNK] for h in hs]
        res = [_mm(jnp.concatenate([sol[h][rows, GDN_DV:], qg[h][rows]], axis=0), state[h])
               for h in hs]
        v_new = [sol[h][rows, :GDN_DV] - res[h][:GDN_CHUNK] for h in hs]
        v_all = [jnp.concatenate([v_new[h], zeros_c] if cc == 0 else [zeros_c, v_new[h]], axis=0)
                 for h in hs]
        k_dec_t = [k_t[h] * jnp.exp(jnp.where(in_chunk, g_last[h] - gc_r[h], NEG)) for h in hs]
        upd = [_mm(jnp.concatenate([attn[h][rows], k_dec_t[h]], axis=0), v_all[h]) for h in hs]
        o_c = [res[h][GDN_CHUNK:] + upd[h][:GDN_CHUNK] for h in hs]
        state = [state[h] * jnp.exp(g_last[h]) + upd[h][GDN_CHUNK:] for h in hs]
        for h in hs:
            cols = slice(hi[h] * GDN_DV, (hi[h] + 1) * GDN_DV)
            gated = _rms(o_c[h], ng_ref[...]) * gate_ref[bi[h], rows, cols].astype(F32)
            o_ref[bi[h], rows, cols] = gated.astype(o_ref.dtype)
    for h in hs:
        s_ref[h] = state[h]


def _proj_out_body(x_ref, o_ref, wout_ref, y_ref):
    y_ref[0] = x_ref[0] + jnp.dot(o_ref[0], wout_ref[...], preferred_element_type=F32)


def _proj_out(x, o, w_out, *, ts, name):
    B, S, D = x.shape
    tok = lambda w: pl.BlockSpec((1, ts, w), lambda b, j: (b, j, 0))
    return pl.pallas_call(
        _proj_out_body,
        grid=(B, S // ts),
        in_specs=[tok(D), tok(o.shape[-1]), _const_spec(w_out.shape)],
        out_specs=tok(D),
        out_shape=jax.ShapeDtypeStruct((B, S, D), F32),
        compiler_params=_params(2),
        name=name,
    )(x, o, w_out.astype(BF16))


def _gdn_layer(x, ng, w_in, conv_w, a_log, dt_bias, norm_g, w_out, *, ts=256, nsub=2, ts_out=1024):
    B, S, D = x.shape
    nch = 2 * GDN_QK + GDN_V
    w_all = w_in.astype(BF16)
    w_ba = jnp.pad(w_in[:, nch + GDN_V:], ((0, 0), (0, LANES - 2 * GDN_HEADS))).astype(BF16)
    pad_a = (GDN_HEADS, LANES - 2 * GDN_HEADS)
    alog = jnp.pad(a_log.astype(F32), pad_a).reshape(1, LANES)
    dtb = jnp.pad(dt_bias.astype(F32), pad_a).reshape(1, LANES)

    tb = ts * nsub
    tok = lambda w: pl.BlockSpec((1, tb, w), lambda b, j: (b, j, 0))
    q, k, v, gate, gb = pl.pallas_call(
        functools.partial(_gdn_in_body, ts=ts, nsub=nsub),
        grid=(B, S // tb),
        in_specs=[tok(D), _const_spec((1, D)), _const_spec(w_all.shape), _const_spec(w_ba.shape),
                  _const_spec(conv_w.shape), _const_spec((1, LANES)), _const_spec((1, LANES))],
        out_specs=[tok(GDN_QK), tok(GDN_QK), tok(GDN_V), tok(GDN_V), tok(LANES)],
        out_shape=[jax.ShapeDtypeStruct((B, S, GDN_QK), BF16),
                   jax.ShapeDtypeStruct((B, S, GDN_QK), BF16),
                   jax.ShapeDtypeStruct((B, S, GDN_V), BF16),
                   jax.ShapeDtypeStruct((B, S, GDN_V), BF16),
                   jax.ShapeDtypeStruct((B, S, LANES), F32)],
        scratch_shapes=[pltpu.VMEM((CONV_HALO + ts, nch), F32)],
        compiler_params=_params(2),
        name="gdn_in",
    )(x, ng.reshape(1, D), w_all, w_ba, conv_w, alog, dtb)

    gbt = jnp.transpose(gb[:, :, :2 * GDN_HEADS], (0, 2, 1))
    n = GDN_UNIT
    nb = 2 if B % 2 == 0 else 1
    utok = lambda w: pl.BlockSpec((nb, n, w), lambda b, j: (b, j, 0))
    o = pl.pallas_call(
        _gdn_core_body,
        grid=(B // nb, S // n),
        in_specs=[utok(GDN_QK), utok(GDN_QK), utok(GDN_V), utok(LANES),
                  pl.BlockSpec((nb, 2 * GDN_HEADS, n), lambda b, j: (b, 0, j)),
                  utok(GDN_V), _const_spec((1, GDN_DV))],
        out_specs=utok(GDN_V),
        out_shape=jax.ShapeDtypeStruct((B, S, GDN_V), BF16),
        scratch_shapes=[pltpu.VMEM((nb * GDN_HEADS, GDN_DK, GDN_DV), F32)],
        compiler_params=_params(2),
        name="gdn_core",
    )(q, k, v, gb, gbt, gate, norm_g.reshape(1, GDN_DV))

    return _proj_out(x, o, w_out, ts=ts_out, name="gdn_out")


def _mla_in_body(x_ref, pos_ref, ng_ref, win_ref, qg_ref, wuq_ref, kvg_ref, wukv_ref, inv_ref,
                 q_ref, k_ref, v_ref, gate_ref):
    c1 = MLA_Q_LORA
    c2 = c1 + MLA_KV_LORA
    c3 = c2 + LANES
    x = x_ref[0]
    h = _rms(x, ng_ref[...]).astype(BF16)
    cq = jnp.dot(h, win_ref[:, :c1], preferred_element_type=F32)
    ckv = jnp.dot(h, win_ref[:, c1:c2], preferred_element_type=F32)
    kr = jnp.dot(h, win_ref[:, c2:c3], preferred_element_type=F32)
    gate = jnp.dot(h, win_ref[:, c3:], preferred_element_type=F32)
    gate_ref[0] = _silu(gate).astype(BF16)

    half = MLA_ROPE // 2
    ang = pos_ref[0].astype(F32) * inv_ref[...]
    cos = jnp.cos(ang)
    sin = jnp.sin(ang)
    lane = lax.broadcasted_iota(jnp.int32, (1, LANES), 1)
    is_x1 = (lane // half) % 2 == 0
    sin_s = sin * jnp.where(is_x1, -1.0, 1.0)

    def rope(t):
        partner = jnp.where(is_x1, pltpu.roll(t, LANES - half, 1), pltpu.roll(t, half, 1))
        return t * cos + partner * sin_s

    k_rope_t = rope(kr).T.astype(BF16)
    qf = jnp.dot(_rms(cq, qg_ref[...]).astype(BF16), wuq_ref[...],
                 preferred_element_type=F32) * (MLA_QK ** -0.5 * LOG2E)
    kv = jnp.dot(_rms(ckv, kvg_ref[...]).astype(BF16), wukv_ref[...], preferred_element_type=F32)
    nope_w = MLA_HEADS * MLA_NOPE
    q_pad = jnp.zeros((qf.shape[0], MLA_QPAD - MLA_QK), F32)
    for hh in range(MLA_HEADS):
        if hh % 2 == 0:
            r0 = nope_w + (hh // 2) * LANES
            roped = rope(qf[:, r0:r0 + LANES])
        e = (hh % 2) * MLA_ROPE
        q_ref[0, hh] = jnp.concatenate(
            [qf[:, hh * MLA_NOPE:(hh + 1) * MLA_NOPE], roped[:, e:e + MLA_ROPE], q_pad],
            axis=1).astype(BF16)
        k0 = hh * (MLA_NOPE + MLA_V)
        k_ref[0, hh, 0:MLA_NOPE, :] = kv[:, k0:k0 + MLA_NOPE].T.astype(BF16)
        k_ref[0, hh, MLA_NOPE:, :] = k_rope_t
        v_ref[0, hh, :, :MLA_V] = kv[:, k0 + MLA_NOPE:k0 + MLA_NOPE + MLA_V].astype(BF16)
        v_ref[0, hh, :, MLA_V:] = jnp.ones((kv.shape[0], LANES), BF16)


def _attn_body(q_ref, k_ref, v_ref, gate_ref, o_ref, s_ref, p_ref, m_ref, a_ref, acc_ref, *, tq, tk):
    i = pl.program_id(2)
    nh = q_ref.shape[1]
    m_ref[...] = jnp.full(m_ref.shape, NEG, F32)
    acc_ref[...] = jnp.zeros(acc_ref.shape, F32)

    def step(c0, width, masked):
        for h in range(nh):
            s_ref[h, :, :width] = jnp.dot(q_ref[0, h], k_ref[0, h, :, pl.ds(c0, width)],
                                          preferred_element_type=F32)
        for h in range(nh):
            for r0 in range(0, tq, ATTN_STRIP):
                rs = slice(r0, r0 + ATTN_STRIP)
                s = s_ref[h, rs, :width]
                if masked:
                    row = i * tq + r0 + lax.broadcasted_iota(jnp.int32, (ATTN_STRIP, 1), 0)
                    col = c0 + lax.broadcasted_iota(jnp.int32, (1, width), 1)
                    s = jnp.where(col <= row, s, NEG)
                m_old = m_ref[h, rs, :]
                m_new = jnp.maximum(m_old, jnp.max(s, axis=-1, keepdims=True))
                a_ref[h, rs, :] = jnp.exp2(m_old - m_new)
                m_ref[h, rs, :] = m_new
                p_ref[h, rs, :width] = jnp.concatenate(
                    [jnp.exp2(s[:, c:c + LANES] - m_new) for c in range(0, width, LANES)],
                    axis=1).astype(BF16)
        for h in range(nh):
            pv = jnp.dot(p_ref[h, :, :width], v_ref[0, h, pl.ds(c0, width), :],
                         preferred_element_type=F32)
            a = a_ref[h]
            acc_ref[h, :, :MLA_V] = a * acc_ref[h, :, :MLA_V] + pv[:, :MLA_V]
            acc_ref[h, :, MLA_V:] = a * acc_ref[h, :, MLA_V:] + pv[:, MLA_V:]

    n_wide = (i * tq) // tk

    def wide_step(kb, carry):
        step(pl.multiple_of(kb * tk, tk), tk, False)
        return carry

    lax.fori_loop(0, n_wide, wide_step, 0)
    for d in range(tk // tq - 1):
        @pl.when(n_wide * tk + d * tq < i * tq)
        def _():
            step(pl.multiple_of(n_wide * tk + d * tq, tq), tq, False)
    step(pl.multiple_of(i * tq, tq), tq, True)
    for h in range(nh):
        cols = slice(h * MLA_V, (h + 1) * MLA_V)
        o = acc_ref[h, :, :MLA_V] / acc_ref[h, :, MLA_V:]
        o_ref[0, :, cols] = (o * gate_ref[0, :, cols].astype(F32)).astype(o_ref.dtype)


def _mla_layer(x, pos, ng, w_in, q_norm_g, w_uq, kv_norm_g, w_ukv, w_out, *, ts=256, ts_out=1024, tq=512, tk=1024,
               nh=4):
    B, S, D = x.shape
    H = MLA_HEADS
    half = MLA_ROPE // 2
    c1 = MLA_Q_LORA
    c2 = c1 + MLA_KV_LORA
    c3 = c2 + MLA_ROPE
    w_in_p = jnp.concatenate(
        [w_in[:, :c3], jnp.zeros((D, LANES - MLA_ROPE), w_in.dtype), w_in[:, c3:]],
        axis=1).astype(BF16)
    wq = w_uq.reshape(MLA_Q_LORA, H, MLA_QK)
    wq_p = jnp.concatenate([wq[:, :, :MLA_NOPE].reshape(MLA_Q_LORA, H * MLA_NOPE),
                            wq[:, :, MLA_NOPE:].reshape(MLA_Q_LORA, H * MLA_ROPE)],
                           axis=1).astype(BF16)
    inv = ROPE_THETA ** (-jnp.arange(half, dtype=F32) / half)
    inv = jnp.tile(inv, LANES // half).reshape(1, LANES)
    width = H * MLA_V

    tok = lambda w: pl.BlockSpec((1, ts, w), lambda b, j: (b, j, 0))
    head = lambda w: pl.BlockSpec((1, H, ts, w), lambda b, j: (b, 0, j, 0))
    q, k, v, gate = pl.pallas_call(
        _mla_in_body,
        grid=(B, S // ts),
        in_specs=[tok(D), tok(1), _const_spec((1, D)), _const_spec(w_in_p.shape),
                  _const_spec((1, MLA_Q_LORA)), _const_spec(wq_p.shape),
                  _const_spec((1, MLA_KV_LORA)), _const_spec(w_ukv.shape), _const_spec((1, LANES))],
        out_specs=[head(MLA_QPAD), pl.BlockSpec((1, H, MLA_QPAD, ts), lambda b, j: (b, 0, 0, j)),
                   head(MLA_V + LANES), tok(width)],
        out_shape=[jax.ShapeDtypeStruct((B, H, S, MLA_QPAD), BF16),
                   jax.ShapeDtypeStruct((B, H, MLA_QPAD, S), BF16),
                   jax.ShapeDtypeStruct((B, H, S, MLA_V + LANES), BF16),
                   jax.ShapeDtypeStruct((B, S, width), BF16)],
        compiler_params=_params(2),
        name="mla_in",
    )(x, pos.reshape(B, S, 1), ng.reshape(1, D), w_in_p, q_norm_g.reshape(1, MLA_Q_LORA), wq_p,
      kv_norm_g.reshape(1, MLA_KV_LORA), w_ukv.astype(BF16), inv)

    o = pl.pallas_call(
        functools.partial(_attn_body, tq=tq, tk=tk),
        grid=(B, H // nh, S // tq),
        in_specs=[pl.BlockSpec((1, nh, tq, MLA_QPAD), lambda b, h, i: (b, h, i, 0)),
                  pl.BlockSpec((1, nh, MLA_QPAD, S), lambda b, h, i: (b, h, 0, 0)),
                  pl.BlockSpec((1, nh, S, MLA_V + LANES), lambda b, h, i: (b, h, 0, 0)),
                  pl.BlockSpec((1, tq, nh * MLA_V), lambda b, h, i: (b, i, h))],
        out_specs=pl.BlockSpec((1, tq, nh * MLA_V), lambda b, h, i: (b, i, h)),
        out_shape=jax.ShapeDtypeStruct((B, S, width), BF16),
        scratch_shapes=[pltpu.VMEM((nh, tq, tk), F32), pltpu.VMEM((nh, tq, tk), BF16),
                        pltpu.VMEM((nh, tq, LANES), F32), pltpu.VMEM((nh, tq, LANES), F32),
                        pltpu.VMEM((nh, tq, MLA_V + LANES), F32)],
        compiler_params=_params(3, ATTN_VMEM_LIMIT_V7X),
        name="mla_attn",
    )(q, k, v, gate)

    return _proj_out(x, o, w_out, ts=ts_out, name="mla_out")


def kernel(x, positions, norm_g, pool_w_in, pool_w_grp, pool_scale, pool_w_out, gdn_w_in, gdn_conv, gdn_a_log, gdn_dt_bias, gdn_norm_g, gdn_w_out, mla_w_in, mla_q_norm_g, mla_w_uq, mla_kv_norm_g, mla_w_ukv, mla_w_out, final_g):
    depth = norm_g.shape[0]
    pool_w = (pool_w_in.astype(BF16), pool_w_grp.astype(BF16), pool_w_out.astype(BF16))
    for i in range(depth):
        kind, j = i % N_MIXERS, i // N_MIXERS
        last = i == depth - 1
        if kind == 0:
            x = _pool_layer(x, norm_g[i], pool_w[0], pool_w[1], pool_scale[j], pool_w[2], j,
                            final_g if last else None)
        elif kind == 1:
            x = _gdn_layer(x, norm_g[i], gdn_w_in[j], gdn_conv[j], gdn_a_log[j], gdn_dt_bias[j],
                           gdn_norm_g[j], gdn_w_out[j])
        else:
            x = _mla_layer(x, positions, norm_g[i], mla_w_in[j], mla_q_norm_g[j], mla_w_uq[j],
                           mla_kv_norm_g[j], mla_w_ukv[j], mla_w_out[j])
        if last and kind != 0:
            x = _final_norm(x, final_g)
    return x


def _final_norm_body(x_ref, g_ref, o_ref):
    o_ref[0] = _rms(x_ref[0], g_ref[...])


def _final_norm(x, g, *, ts=512):
    B, S, D = x.shape
    tok = pl.BlockSpec((1, ts, D), lambda b, j: (b, j, 0))
    return pl.pallas_call(
        _final_norm_body, grid=(B, S // ts), in_specs=[tok, _const_spec((1, D))], out_specs=tok,
        out_shape=jax.ShapeDtypeStruct((B, S, D), F32), compiler_params=_params(2),
        name="final_norm",
    )(x, g.reshape(1, D))
```

```python
import functools

import jax
import jax.numpy as jnp
from jax import lax
from jax.experimental import pallas as pl
from jax.experimental.pallas import tpu as pltpu

F32 = jnp.float32
BF16 = jnp.bfloat16
EPS = 1e-6
NEG = -1e30
VMEM_LIMIT_V7X = 56 * 2 ** 20
ATTN_VMEM_LIMIT_V7X = 60 * 2 ** 20
LANES = 128
SUBLANES = 8
MXU_COLS = 256

N_MIXERS = 3

POOL_WINDOWS = (2, 4, 8, 16)
POOL_HALO = SUBLANES * (max(POOL_WINDOWS).bit_length() - 1)

GDN_HEADS = 8
GDN_DK = 128
GDN_DV = 256
GDN_CONV = 4
GDN_CHUNK = 64
GDN_QK = GDN_HEADS * GDN_DK
GDN_V = GDN_HEADS * GDN_DV
GDN_UNIT = 2 * GDN_CHUNK
CONV_HALO = 8

MLA_HEADS = 16
MLA_NOPE = 128
MLA_ROPE = 64
MLA_V = 128
MLA_Q_LORA = 768
MLA_KV_LORA = 512
MLA_QK = MLA_NOPE + MLA_ROPE
MLA_QPAD = 256
ROPE_THETA = 10000.0
ATTN_STRIP = 32
LOG2E = 1.4426950408889634


def _const_spec(shape):
    zeros = (0,) * len(shape)
    return pl.BlockSpec(shape, lambda *_: zeros, pipeline_mode=pl.Buffered(1))


def _layer_spec(stacked_shape, layer):
    index = (layer,) + (0,) * (len(stacked_shape) - 1)
    return pl.BlockSpec((None,) + tuple(stacked_shape[1:]), lambda *_: index,
                        pipeline_mode=pl.Buffered(1))


def _params(n_axes, vmem_limit=VMEM_LIMIT_V7X):
    return pltpu.CompilerParams(dimension_semantics=("arbitrary",) * n_axes,
                                vmem_limit_bytes=vmem_limit)


def _rms(x, g):
    return x * lax.rsqrt(jnp.mean(x * x, axis=-1, keepdims=True) + EPS) * g


def _silu_of_twice(h):
    return h + h * jnp.tanh(h)


def _silu(x):
    return _silu_of_twice(0.5 * x)


def _mm(a, b):
    return jnp.dot(a.astype(BF16), b.astype(BF16), preferred_element_type=F32)


def _mm_nt(a, b):
    return lax.dot_general(a.astype(BF16), b.astype(BF16), (((1,), (1,)), ((), ())),
                           preferred_element_type=F32)


def _pool_body(x_ref, ng_ref, win_ref, wgrp_ref, sc_ref, wout_ref, fg_ref, o_ref, ubuf_ref,
               sa_ref, sb_ref, *, ts, nsub, width, final):
    j = pl.program_id(1)
    grp = width // len(POOL_WINDOWS)

    def window_sum(w, c0):
        levels = w.bit_length() - 1
        src, cols, shift = ubuf_ref, slice(c0, c0 + grp), 1
        for lv in range(levels):
            r0 = POOL_HALO - SUBLANES * (levels - 1 - lv)
            n = POOL_HALO + ts - r0
            val = src[r0:r0 + n, cols] + src[r0 - shift:r0 - shift + n, cols]
            if lv == levels - 1:
                return val
            dst = (sa_ref, sb_ref)[lv % 2]
            dst[r0:r0 + n, :] = val
            src, cols, shift = dst, slice(0, grp), 2 * shift

    @pl.when(j == 0)
    def _():
        ubuf_ref[0:POOL_HALO, :] = jnp.zeros((POOL_HALO, width), F32)

    for st in range(nsub):
        rows = slice(st * ts, (st + 1) * ts)
        x = x_ref[0, rows, :]
        h = _rms(x, ng_ref[...]).astype(BF16)
        u = jnp.dot(h, win_ref[:, :width], preferred_element_type=F32)
        gate = jnp.dot(h, win_ref[:, width:], preferred_element_type=F32)
        ubuf_ref[POOL_HALO:POOL_HALO + ts, :] = u
        t = (j * nsub + st) * ts + lax.broadcasted_iota(jnp.int32, (ts, 1), 0)
        parts = []
        for gi, w in enumerate(POOL_WINDOWS):
            c0 = gi * grp
            cnt = jnp.minimum(t + 1, w).astype(F32)
            p = window_sum(w, c0) / cnt - u[:, c0:c0 + grp]
            parts.append(jnp.dot(p.astype(BF16), wgrp_ref[gi], preferred_element_type=F32))
        pg = jnp.concatenate(parts, axis=1)
        y = pg * sc_ref[...] * _silu(gate)
        out = x + jnp.dot(y.astype(BF16), wout_ref[...], preferred_element_type=F32)
        ubuf_ref[0:POOL_HALO, :] = u[ts - POOL_HALO:, :]
        if final:
            out = _rms(out, fg_ref[...])
        o_ref[0, rows, :] = out


def _pool_layer(x, ng, w_in, w_grp, scale, w_out, layer, final_g, *, ts=512, nsub=2):
    B, S, D = x.shape
    width = w_out.shape[1]
    final = final_g is not None
    fg = final_g if final else jnp.ones((D,), F32)
    body = functools.partial(_pool_body, ts=ts, nsub=nsub, width=width, final=final)
    tb = ts * nsub
    return pl.pallas_call(
        body,
        grid=(B, S // tb),
        in_specs=[
            pl.BlockSpec((1, tb, D), lambda b, j: (b, j, 0)),
            _const_spec((1, D)),
            _layer_spec(w_in.shape, layer),
            _layer_spec(w_grp.shape, layer),
            _const_spec((1, width)),
            _layer_spec(w_out.shape, layer),
            _const_spec((1, D)),
        ],
        out_specs=pl.BlockSpec((1, tb, D), lambda b, j: (b, j, 0)),
        out_shape=jax.ShapeDtypeStruct((B, S, D), F32),
        scratch_shapes=[pltpu.VMEM((POOL_HALO + ts, width), F32),
                        pltpu.VMEM((POOL_HALO + ts, width // len(POOL_WINDOWS)), F32),
                        pltpu.VMEM((POOL_HALO + ts, width // len(POOL_WINDOWS)), F32)],
        compiler_params=_params(2),
        name="pool_layer",
    )(x, ng.reshape(1, D), w_in, w_grp, scale.reshape(1, width), w_out, fg.reshape(1, D))


def _gdn_in_body(x_ref, ng_ref, win_ref, wba_ref, conv_ref, alog_ref, dtb_ref,
                 q_ref, k_ref, v_ref, gate_ref, gb_ref, cbuf_ref, *, ts, nsub):
    j = pl.program_id(1)
    nch = 2 * GDN_QK + GDN_V

    @pl.when(j == 0)
    def _():
        cbuf_ref[0:CONV_HALO, :] = jnp.zeros((CONV_HALO, nch), F32)

    lane = lax.broadcasted_iota(jnp.int32, (1, LANES), 1)
    r = lax.broadcasted_iota(jnp.int32, (ts, 1), 0)
    c = lax.broadcasted_iota(jnp.int32, (1, ts), 1)
    tri = jnp.where(((r // GDN_CHUNK) == (c // GDN_CHUNK)) & (c <= r), 1.0, 0.0).astype(BF16)

    for st in range(nsub):
        rows = slice(st * ts, (st + 1) * ts)
        x = x_ref[0, rows, :]
        xn = _rms(x, ng_ref[...])
        h = xn.astype(BF16)
        h_half = (0.5 * xn).astype(BF16)

        for c0 in range(0, nch, MXU_COLS):
            cols = slice(c0, c0 + MXU_COLS)
            pre = jnp.dot(h_half, win_ref[:, cols], preferred_element_type=F32)
            cbuf_ref[CONV_HALO:CONV_HALO + ts, cols] = pre
            acc = pre * conv_ref[GDN_CONV - 1:GDN_CONV, cols]
            for kk in range(GDN_CONV - 1):
                r0 = CONV_HALO - (GDN_CONV - 1) + kk
                acc = acc + cbuf_ref[r0:r0 + ts, cols] * conv_ref[kk:kk + 1, cols]
            cbuf_ref[0:CONV_HALO, cols] = pre[ts - CONV_HALO:, :]
            a = _silu_of_twice(acc)
            if c0 < 2 * GDN_QK:
                is_q = c0 < GDN_QK
                out_ref, base = (q_ref, c0) if is_q else (k_ref, c0 - GDN_QK)
                for d0 in range(0, MXU_COLS, GDN_DK):
                    ah = a[:, d0:d0 + GDN_DK]
                    an = ah * lax.rsqrt(jnp.sum(ah * ah, axis=-1, keepdims=True) + EPS)
                    if is_q:
                        an = an * (GDN_DK ** -0.5)
                    out_ref[0, rows, base + d0:base + d0 + GDN_DK] = an.astype(BF16)
            else:
                v0 = c0 - 2 * GDN_QK
                v_ref[0, rows, v0:v0 + MXU_COLS] = a.astype(BF16)
        for c0 in range(0, GDN_V, MXU_COLS):
            gate = jnp.dot(h_half, win_ref[:, nch + c0:nch + c0 + MXU_COLS],
                           preferred_element_type=F32)
            gate_ref[0, rows, c0:c0 + MXU_COLS] = _silu_of_twice(gate).astype(BF16)

        ba = jnp.dot(h, wba_ref[...], preferred_element_type=F32)
        z = ba + dtb_ref[...]
        softplus = jnp.maximum(z, 0.0) + jnp.log1p(jnp.exp(-jnp.abs(z)))
        g = jnp.where((lane >= GDN_HEADS) & (lane < 2 * GDN_HEADS),
                      -jnp.exp(alog_ref[...]) * softplus, 0.0)
        g_hi = g.astype(BF16)
        r1 = g - g_hi.astype(F32)
        g_mid = r1.astype(BF16)
        g_lo = (r1 - g_mid.astype(F32)).astype(BF16)
        gc = (jnp.dot(tri, g_hi, preferred_element_type=F32)
              + jnp.dot(tri, g_mid, preferred_element_type=F32)
              + jnp.dot(tri, g_lo, preferred_element_type=F32))
        gb_ref[0, rows, :] = jnp.where(lane < GDN_HEADS, jax.nn.sigmoid(ba), gc)


def _gdn_core_body(q_ref, k_ref, v_ref, gb_ref, gbt_ref, gate_ref, ng_ref, o_ref, s_ref):
    j = pl.program_id(1)

    @pl.when(j == 0)
    def _():
        s_ref[...] = jnp.zeros_like(s_ref)

    n = GDN_UNIT
    r = lax.broadcasted_iota(jnp.int32, (n, 1), 0)
    c = lax.broadcasted_iota(jnp.int32, (1, n), 1)
    same = (r // GDN_CHUNK) == (c // GDN_CHUNK)
    causal = same & (c <= r)
    strict = same & (c < r)
    eye = jnp.where(r == c, 1.0, 0.0)
    zeros_c = jnp.zeros((GDN_CHUNK, GDN_DV), F32)
    nb = q_ref.shape[0]
    hs = range(nb * GDN_HEADS)
    bi = [ci // GDN_HEADS for ci in hs]
    hi = [ci % GDN_HEADS for ci in hs]
    gb = [gb_ref[b] for b in range(nb)]
    gbt = [gbt_ref[b] for b in range(nb)]
    beta_c = [gb[bi[h]][:, hi[h]:hi[h] + 1] for h in hs]
    gc_c = [gb[bi[h]][:, GDN_HEADS + hi[h]:GDN_HEADS + hi[h] + 1] for h in hs]
    gc_r = [gbt[bi[h]][GDN_HEADS + hi[h]:GDN_HEADS + hi[h] + 1, :] for h in hs]
    q = [q_ref[bi[h], :, hi[h] * GDN_DK:(hi[h] + 1) * GDN_DK] for h in hs]
    k = [k_ref[bi[h], :, hi[h] * GDN_DK:(hi[h] + 1) * GDN_DK] for h in hs]
    kf = [k[h].astype(F32) for h in hs]
    kb = [kf[h] * beta_c[h] for h in hs]
    decay = [jnp.exp(jnp.where(causal, gc_c[h] - gc_r[h], NEG)) for h in hs]
    kkqk = [_mm_nt(jnp.concatenate([kb[h].astype(BF16), q[h]], axis=0), k[h]) for h in hs]
    lmat = [jnp.where(strict, kkqk[h][:n] * decay[h], 0.0) for h in hs]
    attn = [kkqk[h][n:] * decay[h] for h in hs]
    ch = GDN_CHUNK
    eye_s = eye[:ch] + eye[ch:]

    def block_diag(side):
        return jnp.where(same, jnp.concatenate([side, side], axis=0), 0.0)

    l_s = [lmat[h][:ch] + lmat[h][ch:] for h in hs]
    xs = [eye_s - l_s[h] for h in hs]
    ps = [_mm(l_s[h], lmat[h]) for h in hs]
    for _ in range(4):
        res = [_mm(jnp.concatenate([xs[h], ps[h]], axis=0), block_diag(ps[h])) for h in hs]
        xs = [xs[h] + res[h][:ch] for h in hs]
        ps = [res[h][ch:] for h in hs]
    res = [_mm(xs[h], block_diag(ps[h])) for h in hs]
    xm = [block_diag(xs[h] + res[h]) for h in hs]
    eg = [jnp.exp(gc_c[h]) for h in hs]
    rhs = [jnp.concatenate(
        [v_ref[bi[h], :, hi[h] * GDN_DV:(hi[h] + 1) * GDN_DV].astype(F32) * beta_c[h],
         kb[h] * eg[h]], axis=1) for h in hs]
    sol = [rhs[h] + _mm(xm[h] - eye, rhs[h]) for h in hs]
    qg = [q[h].astype(F32) * eg[h] for h in hs]
    k_t = [kf[h].T for h in hs]
    state = [s_ref[h] for h in hs]
    for cc in range(2):
        rows = slice(cc * GDN_CHUNK, (cc + 1) * GDN_CHUNK)
        in_chunk = (c // GDN_CHUNK) == cc
        g_last = [gc_r[h][:, (cc + 1) * GDN_CHUNK - 1:(cc + 1) * GDN_CHUNK] for h in hs]
        res = [_mm(jnp.concatenate([sol[h][rows, GDN_DV:], qg[h][rows]], axis=0), state[h])
               for h in hs]
        v_new = [sol[h][rows, :GDN_DV] - res[h][:GDN_CHUNK] for h in hs]
        v_all = [jnp.concatenate([v_new[h], zeros_c] if cc == 0 else [zeros_c, v_new[h]], axis=0)
                 for h in hs]
        k_dec_t = [k_t[h] * jnp.exp(jnp.where(in_chunk, g_last[h] - gc_r[h], NEG)) for h in hs]
        upd = [_mm(jnp.concatenate([attn[h][rows], k_dec_t[h]], axis=0), v_all[h]) for h in hs]
        o_c = [res[h][GDN_CHUNK:] + upd[h][:GDN_CHUNK] for h in hs]
        state = [state[h] * jnp.exp(g_last[h]) + upd[h][GDN_CHUNK:] for h in hs]
        for h in hs:
            cols = slice(hi[h] * GDN_DV, (hi[h] + 1) * GDN_DV)
            gated = _rms(o_c[h], ng_ref[...]) * gate_ref[bi[h], rows, cols].astype(F32)
            o_ref[bi[h], rows, cols] = gated.astype(o_ref.dtype)
    for h in hs:
        s_ref[h] = state[h]


def _proj_out_body(x_ref, o_ref, wout_ref, y_ref):
    y_ref[0] = x_ref[0] + jnp.dot(o_ref[0], wout_ref[...], preferred_element_type=F32)


def _proj_out(x, o, w_out, *, ts, name):
    B, S, D = x.shape
    tok = lambda w: pl.BlockSpec((1, ts, w), lambda b, j: (b, j, 0))
    return pl.pallas_call(
        _proj_out_body,
        grid=(B, S // ts),
        in_specs=[tok(D), tok(o.shape[-1]), _const_spec(w_out.shape)],
        out_specs=tok(D),
        out_shape=jax.ShapeDtypeStruct((B, S, D), F32),
        compiler_params=_params(2),
        name=name,
    )(x, o, w_out.astype(BF16))


def _gdn_layer(x, ng, w_in, conv_w, a_log, dt_bias, norm_g, w_out, *, ts=256, nsub=1, ts_out=1024):
    B, S, D = x.shape
    nch = 2 * GDN_QK + GDN_V
    w_all = w_in.astype(BF16)
    w_ba = jnp.pad(w_in[:, nch + GDN_V:], ((0, 0), (0, LANES - 2 * GDN_HEADS))).astype(BF16)
    pad_a = (GDN_HEADS, LANES - 2 * GDN_HEADS)
    alog = jnp.pad(a_log.astype(F32), pad_a).reshape(1, LANES)
    dtb = jnp.pad(dt_bias.astype(F32), pad_a).reshape(1, LANES)

    tb = ts * nsub
    tok = lambda w: pl.BlockSpec((1, tb, w), lambda b, j: (b, j, 0))
    q, k, v, gate, gb = pl.pallas_call(
        functools.partial(_gdn_in_body, ts=ts, nsub=nsub),
        grid=(B, S // tb),
        in_specs=[tok(D), _const_spec((1, D)), _const_spec(w_all.shape), _const_spec(w_ba.shape),
                  _const_spec(conv_w.shape), _const_spec((1, LANES)), _const_spec((1, LANES))],
        out_specs=[tok(GDN_QK), tok(GDN_QK), tok(GDN_V), tok(GDN_V), tok(LANES)],
        out_shape=[jax.ShapeDtypeStruct((B, S, GDN_QK), BF16),
                   jax.ShapeDtypeStruct((B, S, GDN_QK), BF16),
                   jax.ShapeDtypeStruct((B, S, GDN_V), BF16),
                   jax.ShapeDtypeStruct((B, S, GDN_V), BF16),
                   jax.ShapeDtypeStruct((B, S, LANES), F32)],
        scratch_shapes=[pltpu.VMEM((CONV_HALO + ts, nch), F32)],
        compiler_params=_params(2),
        name="gdn_in",
    )(x, ng.reshape(1, D), w_all, w_ba, conv_w, alog, dtb)

    gbt = jnp.transpose(gb[:, :, :2 * GDN_HEADS], (0, 2, 1))
    n = GDN_UNIT
    nb = 2 if B % 2 == 0 else 1
    utok = lambda w: pl.BlockSpec((nb, n, w), lambda b, j: (b, j, 0))
    o = pl.pallas_call(
        _gdn_core_body,
        grid=(B // nb, S // n),
        in_specs=[utok(GDN_QK), utok(GDN_QK), utok(GDN_V), utok(LANES),
                  pl.BlockSpec((nb, 2 * GDN_HEADS, n), lambda b, j: (b, 0, j)),
                  utok(GDN_V), _const_spec((1, GDN_DV))],
        out_specs=utok(GDN_V),
        out_shape=jax.ShapeDtypeStruct((B, S, GDN_V), BF16),
        scratch_shapes=[pltpu.VMEM((nb * GDN_HEADS, GDN_DK, GDN_DV), F32)],
        compiler_params=_params(2),
        name="gdn_core",
    )(q, k, v, gb, gbt, gate, norm_g.reshape(1, GDN_DV))

    return _proj_out(x, o, w_out, ts=ts_out, name="gdn_out")


def _mla_in_body(x_ref, pos_ref, ng_ref, win_ref, qg_ref, wuq_ref, kvg_ref, wukv_ref, inv_ref,
                 q_ref, k_ref, v_ref, gate_ref):
    c1 = MLA_Q_LORA
    c2 = c1 + MLA_KV_LORA
    c3 = c2 + LANES
    x = x_ref[0]
    h = _rms(x, ng_ref[...]).astype(BF16)
    cq = jnp.dot(h, win_ref[:, :c1], preferred_element_type=F32)
    ckv = jnp.dot(h, win_ref[:, c1:c2], preferred_element_type=F32)
    kr = jnp.dot(h, win_ref[:, c2:c3], preferred_element_type=F32)
    gate = jnp.dot(h, win_ref[:, c3:], preferred_element_type=F32)
    gate_ref[0] = _silu(gate).astype(BF16)

    half = MLA_ROPE // 2
    ang = pos_ref[0].astype(F32) * inv_ref[...]
    cos = jnp.cos(ang)
    sin = jnp.sin(ang)
    lane = lax.broadcasted_iota(jnp.int32, (1, LANES), 1)
    is_x1 = (lane // half) % 2 == 0
    sin_s = sin * jnp.where(is_x1, -1.0, 1.0)

    def rope(t):
        partner = jnp.where(is_x1, pltpu.roll(t, LANES - half, 1), pltpu.roll(t, half, 1))
        return t * cos + partner * sin_s

    k_rope_t = rope(kr).T.astype(BF16)
    qf = jnp.dot(_rms(cq, qg_ref[...]).astype(BF16), wuq_ref[...],
                 preferred_element_type=F32) * (MLA_QK ** -0.5 * LOG2E)
    kv = jnp.dot(_rms(ckv, kvg_ref[...]).astype(BF16), wukv_ref[...], preferred_element_type=F32)
    nope_w = MLA_HEADS * MLA_NOPE
    q_pad = jnp.zeros((qf.shape[0], MLA_QPAD - MLA_QK), F32)
    for hh in range(MLA_HEADS):
        if hh % 2 == 0:
            r0 = nope_w + (hh // 2) * LANES
            roped = rope(qf[:, r0:r0 + LANES])
        e = (hh % 2) * MLA_ROPE
        q_ref[0, hh] = jnp.concatenate(
            [qf[:, hh * MLA_NOPE:(hh + 1) * MLA_NOPE], roped[:, e:e + MLA_ROPE], q_pad],
            axis=1).astype(BF16)
        k0 = hh * (MLA_NOPE + MLA_V)
        k_ref[0, hh, 0:MLA_NOPE, :] = kv[:, k0:k0 + MLA_NOPE].T.astype(BF16)
        k_ref[0, hh, MLA_NOPE:, :] = k_rope_t
        v_ref[0, hh, :, :MLA_V] = kv[:, k0 + MLA_NOPE:k0 + MLA_NOPE + MLA_V].astype(BF16)
        v_ref[0, hh, :, MLA_V:] = jnp.ones((kv.shape[0], LANES), BF16)


def _attn_body(q_ref, k_ref, v_ref, gate_ref, o_ref, s_ref, p_ref, m_ref, a_ref, acc_ref, *, tq, tk):
    i = pl.program_id(2)
    nh = q_ref.shape[1]
    m_ref[...] = jnp.full(m_ref.shape, NEG, F32)
    acc_ref[...] = jnp.zeros(acc_ref.shape, F32)

    def step(c0, width, masked):
        def scores(h):
            s_ref[h, :, :width] = jnp.dot(q_ref[0, h], k_ref[0, h, :, pl.ds(c0, width)],
                                          preferred_element_type=F32)

        def softmax(h):
            for r0 in range(0, tq, ATTN_STRIP):
                rs = slice(r0, r0 + ATTN_STRIP)
                s = s_ref[h, rs, :width]
                if masked:
                    row = i * tq + r0 + lax.broadcasted_iota(jnp.int32, (ATTN_STRIP, 1), 0)
                    col = c0 + lax.broadcasted_iota(jnp.int32, (1, width), 1)
                    s = jnp.where(col <= row, s, NEG)
                m_old = m_ref[h, rs, :]
                m_new = jnp.maximum(m_old, jnp.max(s, axis=-1, keepdims=True))
                a_ref[h, rs, :] = jnp.exp2(m_old - m_new)
                m_ref[h, rs, :] = m_new
                p_ref[h, rs, :width] = jnp.concatenate(
                    [jnp.exp2(s[:, c:c + LANES] - m_new) for c in range(0, width, LANES)],
                    axis=1).astype(BF16)
        def values(h):
            pv = jnp.dot(p_ref[h, :, :width], v_ref[0, h, pl.ds(c0, width), :],
                         preferred_element_type=F32)
            a = a_ref[h]
            acc_ref[h, :, :MLA_V] = a * acc_ref[h, :, :MLA_V] + pv[:, :MLA_V]
            acc_ref[h, :, MLA_V:] = a * acc_ref[h, :, MLA_V:] + pv[:, MLA_V:]

        for stage in (scores, softmax, values):
            for h in range(nh):
                stage(h)

    n_wide = (i * tq) // tk

    def wide_step(kb, carry):
        step(pl.multiple_of(kb * tk, tk), tk, False)
        return carry

    lax.fori_loop(0, n_wide, wide_step, 0)
    for d in range(tk // tq - 1):
        @pl.when(n_wide * tk + d * tq < i * tq)
        def _():
            step(pl.multiple_of(n_wide * tk + d * tq, tq), tq, False)
    step(pl.multiple_of(i * tq, tq), tq, True)
    for h in range(nh):
        cols = slice(h * MLA_V, (h + 1) * MLA_V)
        o = acc_ref[h, :, :MLA_V] / acc_ref[h, :, MLA_V:]
        o_ref[0, :, cols] = (o * gate_ref[0, :, cols].astype(F32)).astype(o_ref.dtype)


def _mla_layer(x, pos, ng, w_in, q_norm_g, w_uq, kv_norm_g, w_ukv, w_out, *, ts=256, ts_out=1024, tq=512, tk=1024,
               nh=4):
    B, S, D = x.shape
    H = MLA_HEADS
    half = MLA_ROPE // 2
    c1 = MLA_Q_LORA
    c2 = c1 + MLA_KV_LORA
    c3 = c2 + MLA_ROPE
    w_in_p = jnp.concatenate(
        [w_in[:, :c3], jnp.zeros((D, LANES - MLA_ROPE), w_in.dtype), w_in[:, c3:]],
        axis=1).astype(BF16)
    wq = w_uq.reshape(MLA_Q_LORA, H, MLA_QK)
    wq_p = jnp.concatenate([wq[:, :, :MLA_NOPE].reshape(MLA_Q_LORA, H * MLA_NOPE),
                            wq[:, :, MLA_NOPE:].reshape(MLA_Q_LORA, H * MLA_ROPE)],
                           axis=1).astype(BF16)
    inv = ROPE_THETA ** (-jnp.arange(half, dtype=F32) / half)
    inv = jnp.tile(inv, LANES // half).reshape(1, LANES)
    width = H * MLA_V

    tok = lambda w: pl.BlockSpec((1, ts, w), lambda b, j: (b, j, 0))
    head = lambda w: pl.BlockSpec((1, H, ts, w), lambda b, j: (b, 0, j, 0))
    q, k, v, gate = pl.pallas_call(
        _mla_in_body,
        grid=(B, S // ts),
        in_specs=[tok(D), tok(1), _const_spec((1, D)), _const_spec(w_in_p.shape),
                  _const_spec((1, MLA_Q_LORA)), _const_spec(wq_p.shape),
                  _const_spec((1, MLA_KV_LORA)), _const_spec(w_ukv.shape), _const_spec((1, LANES))],
        out_specs=[head(MLA_QPAD), pl.BlockSpec((1, H, MLA_QPAD, ts), lambda b, j: (b, 0, 0, j)),
                   head(MLA_V + LANES), tok(width)],
        out_shape=[jax.ShapeDtypeStruct((B, H, S, MLA_QPAD), BF16),
                   jax.ShapeDtypeStruct((B, H, MLA_QPAD, S), BF16),
                   jax.ShapeDtypeStruct((B, H, S, MLA_V + LANES), BF16),
                   jax.ShapeDtypeStruct((B, S, width), BF16)],
        compiler_params=_params(2),
        name="mla_in",
    )(x, pos.reshape(B, S, 1), ng.reshape(1, D), w_in_p, q_norm_g.reshape(1, MLA_Q_LORA), wq_p,
      kv_norm_g.reshape(1, MLA_KV_LORA), w_ukv.astype(BF16), inv)

    o = pl.pallas_call(
        functools.partial(_attn_body, tq=tq, tk=tk),
        grid=(B, H // nh, S // tq),
        in_specs=[pl.BlockSpec((1, nh, tq, MLA_QPAD), lambda b, h, i: (b, h, i, 0)),
                  pl.BlockSpec((1, nh, MLA_QPAD, S), lambda b, h, i: (b, h, 0, 0)),
                  pl.BlockSpec((1, nh, S, MLA_V + LANES), lambda b, h, i: (b, h, 0, 0)),
                  pl.BlockSpec((1, tq, nh * MLA_V), lambda b, h, i: (b, i, h))],
        out_specs=pl.BlockSpec((1, tq, nh * MLA_V), lambda b, h, i: (b, i, h)),
        out_shape=jax.ShapeDtypeStruct((B, S, width), BF16),
        scratch_shapes=[pltpu.VMEM((nh, tq, tk), F32), pltpu.VMEM((nh, tq, tk), BF16),
                        pltpu.VMEM((nh, tq, LANES), F32), pltpu.VMEM((nh, tq, LANES), F32),
                        pltpu.VMEM((nh, tq, MLA_V + LANES), F32)],
        compiler_params=_params(3, ATTN_VMEM_LIMIT_V7X),
        name="mla_attn",
    )(q, k, v, gate)

    return _proj_out(x, o, w_out, ts=ts_out, name="mla_out")


def kernel(x, positions, norm_g, pool_w_in, pool_w_grp, pool_scale, pool_w_out, gdn_w_in, gdn_conv, gdn_a_log, gdn_dt_bias, gdn_norm_g, gdn_w_out, mla_w_in, mla_q_norm_g, mla_w_uq, mla_kv_norm_g, mla_w_ukv, mla_w_out, final_g):
    depth = norm_g.shape[0]
    pool_w = (pool_w_in.astype(BF16), pool_w_grp.astype(BF16), pool_w_out.astype(BF16))
    for i in range(depth):
        kind, j = i % N_MIXERS, i // N_MIXERS
        last = i == depth - 1
        if kind == 0:
            x = _pool_layer(x, norm_g[i], pool_w[0], pool_w[1], pool_scale[j], pool_w[2], j,
                            final_g if last else None)
        elif kind == 1:
            x = _gdn_layer(x, norm_g[i], gdn_w_in[j], gdn_conv[j], gdn_a_log[j], gdn_dt_bias[j],
                           gdn_norm_g[j], gdn_w_out[j])
        else:
            x = _mla_layer(x, positions, norm_g[i], mla_w_in[j], mla_q_norm_g[j], mla_w_uq[j],
                           mla_kv_norm_g[j], mla_w_ukv[j], mla_w_out[j])
        if last and kind != 0:
            x = _final_norm(x, final_g)
    return x


def _final_norm_body(x_ref, g_ref, o_ref):
    o_ref[0] = _rms(x_ref[0], g_ref[...])


def _final_norm(x, g, *, ts=512):
    B, S, D = x.shape
    tok = pl.BlockSpec((1, ts, D), lambda b, j: (b, j, 0))
    return pl.pallas_call(
        _final_norm_body, grid=(B, S // ts), in_specs=[tok, _const_spec((1, D))], out_specs=tok,
        out_shape=jax.ShapeDtypeStruct((B, S, D), F32), compiler_params=_params(2),
        name="final_norm",
    )(x, g.reshape(1, D))
```

```python
import functools

import jax
import jax.numpy as jnp
from jax import lax
from jax.experimental import pallas as pl
from jax.experimental.pallas import tpu as pltpu

F32 = jnp.float32
BF16 = jnp.bfloat16
EPS = 1e-6
NEG = -1e30
VMEM_LIMIT_V7X = 56 * 2 ** 20
ATTN_VMEM_LIMIT_V7X = 60 * 2 ** 20
LANES = 128
SUBLANES = 8
MXU_COLS = 256

N_MIXERS = 3

POOL_WINDOWS = (2, 4, 8, 16)
POOL_HALO = SUBLANES * (max(POOL_WINDOWS).bit_length() - 1)

GDN_HEADS = 8
GDN_DK = 128
GDN_DV = 256
GDN_CONV = 4
GDN_CHUNK = 64
GDN_QK = GDN_HEADS * GDN_DK
GDN_V = GDN_HEADS * GDN_DV
GDN_UNIT = 2 * GDN_CHUNK
CONV_HALO = 8

MLA_HEADS = 16
MLA_NOPE = 128
MLA_ROPE = 64
MLA_V = 128
MLA_Q_LORA = 768
MLA_KV_LORA = 512
MLA_QK = MLA_NOPE + MLA_ROPE
MLA_QPAD = 256
ROPE_THETA = 10000.0
ATTN_STRIP = 32
LOG2E = 1.4426950408889634


def _const_spec(shape):
    zeros = (0,) * len(shape)
    return pl.BlockSpec(shape, lambda *_: zeros, pipeline_mode=pl.Buffered(1))


def _layer_spec(stacked_shape, layer):
    index = (layer,) + (0,) * (len(stacked_shape) - 1)
    return pl.BlockSpec((None,) + tuple(stacked_shape[1:]), lambda *_: index,
                        pipeline_mode=pl.Buffered(1))


def _params(n_axes, vmem_limit=VMEM_LIMIT_V7X):
    return pltpu.CompilerParams(dimension_semantics=("arbitrary",) * n_axes,
                                vmem_limit_bytes=vmem_limit)


def _rms(x, g):
    return x * lax.rsqrt(jnp.mean(x * x, axis=-1, keepdims=True) + EPS) * g


def _silu_of_twice(h):
    return h + h * jnp.tanh(h)


def _silu(x):
    return _silu_of_twice(0.5 * x)


def _mm(a, b):
    return jnp.dot(a.astype(BF16), b.astype(BF16), preferred_element_type=F32)


def _mm_nt(a, b):
    return lax.dot_general(a.astype(BF16), b.astype(BF16), (((1,), (1,)), ((), ())),
                           preferred_element_type=F32)


def _pool_body(x_ref, ng_ref, win_ref, wgrp_ref, sc_ref, wout_ref, fg_ref, o_ref, ubuf_ref,
               sa_ref, sb_ref, *, ts, nsub, width, final):
    j = pl.program_id(1)
    grp = width // len(POOL_WINDOWS)

    def window_sum(w, c0):
        levels = w.bit_length() - 1
        src, cols, shift = ubuf_ref, slice(c0, c0 + grp), 1
        for lv in range(levels):
            r0 = POOL_HALO - SUBLANES * (levels - 1 - lv)
            n = POOL_HALO + ts - r0
            val = src[r0:r0 + n, cols] + src[r0 - shift:r0 - shift + n, cols]
            if lv == levels - 1:
                return val
            dst = (sa_ref, sb_ref)[lv % 2]
            dst[r0:r0 + n, :] = val
            src, cols, shift = dst, slice(0, grp), 2 * shift

    @pl.when(j == 0)
    def _():
        ubuf_ref[0:POOL_HALO, :] = jnp.zeros((POOL_HALO, width), F32)

    for st in range(nsub):
        rows = slice(st * ts, (st + 1) * ts)
        x = x_ref[0, rows, :]
        h = _rms(x, ng_ref[...]).astype(BF16)
        u = jnp.dot(h, win_ref[:, :width], preferred_element_type=F32)
        gate = jnp.dot(h, win_ref[:, width:], preferred_element_type=F32)
        ubuf_ref[POOL_HALO:POOL_HALO + ts, :] = u
        t = (j * nsub + st) * ts + lax.broadcasted_iota(jnp.int32, (ts, 1), 0)
        parts = []
        for gi, w in enumerate(POOL_WINDOWS):
            c0 = gi * grp
            cnt = jnp.minimum(t + 1, w).astype(F32)
            p = window_sum(w, c0) / cnt - u[:, c0:c0 + grp]
            parts.append(jnp.dot(p.astype(BF16), wgrp_ref[gi], preferred_element_type=F32))
        pg = jnp.concatenate(parts, axis=1)
        y = pg * sc_ref[...] * _silu(gate)
        out = x + jnp.dot(y.astype(BF16), wout_ref[...], preferred_element_type=F32)
        ubuf_ref[0:POOL_HALO, :] = u[ts - POOL_HALO:, :]
        if final:
            out = _rms(out, fg_ref[...])
        o_ref[0, rows, :] = out


def _pool_layer(x, ng, w_in, w_grp, scale, w_out, layer, final_g, *, ts=512, nsub=2):
    B, S, D = x.shape
    width = w_out.shape[1]
    final = final_g is not None
    fg = final_g if final else jnp.ones((D,), F32)
    body = functools.partial(_pool_body, ts=ts, nsub=nsub, width=width, final=final)
    tb = ts * nsub
    return pl.pallas_call(
        body,
        grid=(B, S // tb),
        in_specs=[
            pl.BlockSpec((1, tb, D), lambda b, j: (b, j, 0)),
            _const_spec((1, D)),
            _layer_spec(w_in.shape, layer),
            _layer_spec(w_grp.shape, layer),
            _const_spec((1, width)),
            _layer_spec(w_out.shape, layer),
            _const_spec((1, D)),
        ],
        out_specs=pl.BlockSpec((1, tb, D), lambda b, j: (b, j, 0)),
        out_shape=jax.ShapeDtypeStruct((B, S, D), F32),
        scratch_shapes=[pltpu.VMEM((POOL_HALO + ts, width), F32),
                        pltpu.VMEM((POOL_HALO + ts, width // len(POOL_WINDOWS)), F32),
                        pltpu.VMEM((POOL_HALO + ts, width // len(POOL_WINDOWS)), F32)],
        compiler_params=_params(2),
        name="pool_layer",
    )(x, ng.reshape(1, D), w_in, w_grp, scale.reshape(1, width), w_out, fg.reshape(1, D))


def _gdn_in_body(x_ref, ng_ref, win_ref, wba_ref, conv_ref, alog_ref, dtb_ref,
                 q_ref, k_ref, v_ref, gate_ref, gb_ref, cbuf_ref, *, ts, nsub):
    j = pl.program_id(1)
    nch = 2 * GDN_QK + GDN_V

    @pl.when(j == 0)
    def _():
        cbuf_ref[0:CONV_HALO, :] = jnp.zeros((CONV_HALO, nch), F32)

    lane = lax.broadcasted_iota(jnp.int32, (1, LANES), 1)
    r = lax.broadcasted_iota(jnp.int32, (ts, 1), 0)
    c = lax.broadcasted_iota(jnp.int32, (1, ts), 1)
    tri = jnp.where(((r // GDN_CHUNK) == (c // GDN_CHUNK)) & (c <= r), 1.0, 0.0).astype(BF16)

    for st in range(nsub):
        rows = slice(st * ts, (st + 1) * ts)
        x = x_ref[0, rows, :]
        xn = _rms(x, ng_ref[...])
        h = xn.astype(BF16)
        h_half = (0.5 * xn).astype(BF16)

        for c0 in range(0, nch, MXU_COLS):
            cols = slice(c0, c0 + MXU_COLS)
            pre = jnp.dot(h_half, win_ref[:, cols], preferred_element_type=F32)
            cbuf_ref[CONV_HALO:CONV_HALO + ts, cols] = pre
            acc = pre * conv_ref[GDN_CONV - 1:GDN_CONV, cols]
            for kk in range(GDN_CONV - 1):
                r0 = CONV_HALO - (GDN_CONV - 1) + kk
                acc = acc + cbuf_ref[r0:r0 + ts, cols] * conv_ref[kk:kk + 1, cols]
            cbuf_ref[0:CONV_HALO, cols] = pre[ts - CONV_HALO:, :]
            a = _silu_of_twice(acc)
            if c0 < 2 * GDN_QK:
                is_q = c0 < GDN_QK
                out_ref, base = (q_ref, c0) if is_q else (k_ref, c0 - GDN_QK)
                for d0 in range(0, MXU_COLS, GDN_DK):
                    ah = a[:, d0:d0 + GDN_DK]
                    an = ah * lax.rsqrt(jnp.sum(ah * ah, axis=-1, keepdims=True) + EPS)
                    if is_q:
                        an = an * (GDN_DK ** -0.5)
                    out_ref[0, rows, base + d0:base + d0 + GDN_DK] = an.astype(BF16)
            else:
                v0 = c0 - 2 * GDN_QK
                v_ref[0, rows, v0:v0 + MXU_COLS] = a.astype(BF16)
        for c0 in range(0, GDN_V, MXU_COLS):
            gate = jnp.dot(h_half, win_ref[:, nch + c0:nch + c0 + MXU_COLS],
                           preferred_element_type=F32)
            gate_ref[0, rows, c0:c0 + MXU_COLS] = _silu_of_twice(gate).astype(BF16)

        ba = jnp.dot(h, wba_ref[...], preferred_element_type=F32)
        z = ba + dtb_ref[...]
        softplus = jnp.maximum(z, 0.0) + jnp.log1p(jnp.exp(-jnp.abs(z)))
        g = jnp.where((lane >= GDN_HEADS) & (lane < 2 * GDN_HEADS),
                      -jnp.exp(alog_ref[...]) * softplus, 0.0)
        g_hi = g.astype(BF16)
        r1 = g - g_hi.astype(F32)
        g_mid = r1.astype(BF16)
        g_lo = (r1 - g_mid.astype(F32)).astype(BF16)
        gc = (jnp.dot(tri, g_hi, preferred_element_type=F32)
              + jnp.dot(tri, g_mid, preferred_element_type=F32)
              + jnp.dot(tri, g_lo, preferred_element_type=F32))
        gb_ref[0, rows, :] = jnp.where(lane < GDN_HEADS, jax.nn.sigmoid(ba), gc)


def _gdn_core_body(q_ref, k_ref, v_ref, gb_ref, gbt_ref, gate_ref, ng_ref, o_ref, s_ref):
    j = pl.program_id(1)

    @pl.when(j == 0)
    def _():
        s_ref[...] = jnp.zeros_like(s_ref)

    n = GDN_UNIT
    r = lax.broadcasted_iota(jnp.int32, (n, 1), 0)
    c = lax.broadcasted_iota(jnp.int32, (1, n), 1)
    same = (r // GDN_CHUNK) == (c // GDN_CHUNK)
    causal = same & (c <= r)
    strict = same & (c < r)
    eye = jnp.where(r == c, 1.0, 0.0)
    zeros_c = jnp.zeros((GDN_CHUNK, GDN_DV), F32)
    nb = q_ref.shape[0]
    hs = range(nb * GDN_HEADS)
    bi = [ci // GDN_HEADS for ci in hs]
    hi = [ci % GDN_HEADS for ci in hs]
    gb = [gb_ref[b] for b in range(nb)]
    gbt = [gbt_ref[b] for b in range(nb)]
    beta_c = [gb[bi[h]][:, hi[h]:hi[h] + 1] for h in hs]
    gc_c = [gb[bi[h]][:, GDN_HEADS + hi[h]:GDN_HEADS + hi[h] + 1] for h in hs]
    gc_r = [gbt[bi[h]][GDN_HEADS + hi[h]:GDN_HEADS + hi[h] + 1, :] for h in hs]
    q = [q_ref[bi[h], :, hi[h] * GDN_DK:(hi[h] + 1) * GDN_DK] for h in hs]
    k = [k_ref[bi[h], :, hi[h] * GDN_DK:(hi[h] + 1) * GDN_DK] for h in hs]
    kf = [k[h].astype(F32) for h in hs]
    kb = [kf[h] * beta_c[h] for h in hs]
    decay = [jnp.exp(jnp.where(causal, gc_c[h] - gc_r[h], NEG)) for h in hs]
    kkqk = [_mm_nt(jnp.concatenate([kb[h].astype(BF16), q[h]], axis=0), k[h]) for h in hs]
    lmat = [jnp.where(strict, kkqk[h][:n] * decay[h], 0.0) for h in hs]
    attn = [kkqk[h][n:] * decay[h] for h in hs]
    ch = GDN_CHUNK
    eye_s = eye[:ch] + eye[ch:]

    def block_diag(side):
        return jnp.where(same, jnp.concatenate([side, side], axis=0), 0.0)

    l_s = [lmat[h][:ch] + lmat[h][ch:] for h in hs]
    xs = [eye_s - l_s[h] for h in hs]
    ps = [_mm(l_s[h], lmat[h]) for h in hs]
    for _ in range(4):
        res = [_mm(jnp.concatenate([xs[h], ps[h]], axis=0), block_diag(ps[h])) for h in hs]
        xs = [xs[h] + res[h][:ch] for h in hs]
        ps = [res[h][ch:] for h in hs]
    res = [_mm(xs[h], block_diag(ps[h])) for h in hs]
    xm = [block_diag(xs[h] + res[h]) for h in hs]
    eg = [jnp.exp(gc_c[h]) for h in hs]
    rhs = [jnp.concatenate(
        [v_ref[bi[h], :, hi[h] * GDN_DV:(hi[h] + 1) * GDN_DV].astype(F32) * beta_c[h],
         kb[h] * eg[h]], axis=1) for h in hs]
    sol = [rhs[h] + _mm(xm[h] - eye, rhs[h]) for h in hs]
    qg = [q[h].astype(F32) * eg[h] for h in hs]
    k_t = [kf[h].T for h in hs]
    state = [s_ref[h] for h in hs]
    for cc in range(2):
        rows = slice(cc * GDN_CHUNK, (cc + 1) * GDN_CHUNK)
        in_chunk = (c // GDN_CHUNK) == cc
        g_last = [gc_r[h][:, (cc + 1) * GDN_CHUNK - 1:(cc + 1) * GDN_CHUNK] for h in hs]
        res = [_mm(jnp.concatenate([sol[h][rows, GDN_DV:], qg[h][rows]], axis=0), state[h])
               for h in hs]
        v_new = [sol[h][rows, :GDN_DV] - res[h][:GDN_CHUNK] for h in hs]
        v_all = [jnp.concatenate([v_new[h], zeros_c] if cc == 0 else [zeros_c, v_new[h]], axis=0)
                 for h in hs]
        k_dec_t = [k_t[h] * jnp.exp(jnp.where(in_chunk, g_last[h] - gc_r[h], NEG)) for h in hs]
        upd = [_mm(jnp.concatenate([attn[h][rows], k_dec_t[h]], axis=0), v_all[h]) for h in hs]
        o_c = [res[h][GDN_CHUNK:] + upd[h][:GDN_CHUNK] for h in hs]
        state = [state[h] * jnp.exp(g_last[h]) + upd[h][GDN_CHUNK:] for h in hs]
        for h in hs:
            cols = slice(hi[h] * GDN_DV, (hi[h] + 1) * GDN_DV)
            gated = _rms(o_c[h], ng_ref[...]) * gate_ref[bi[h], rows, cols].astype(F32)
            o_ref[bi[h], rows, cols] = gated.astype(o_ref.dtype)
    for h in hs:
        s_ref[h] = state[h]


def _proj_out_body(x_ref, o_ref, wout_ref, y_ref):
    y_ref[0] = x_ref[0] + jnp.dot(o_ref[0], wout_ref[...], preferred_element_type=F32)


def _proj_out(x, o, w_out, *, ts, name):
    B, S, D = x.shape
    tok = lambda w: pl.BlockSpec((1, ts, w), lambda b, j: (b, j, 0))
    return pl.pallas_call(
        _proj_out_body,
        grid=(B, S // ts),
        in_specs=[tok(D), tok(o.shape[-1]), _const_spec(w_out.shape)],
        out_specs=tok(D),
        out_shape=jax.ShapeDtypeStruct((B, S, D), F32),
        compiler_params=_params(2),
        name=name,
    )(x, o, w_out.astype(BF16))


def _gdn_layer(x, ng, w_in, conv_w, a_log, dt_bias, norm_g, w_out, *, ts=256, nsub=1, ts_out=1024):
    B, S, D = x.shape
    nch = 2 * GDN_QK + GDN_V
    w_all = w_in.astype(BF16)
    w_ba = jnp.pad(w_in[:, nch + GDN_V:], ((0, 0), (0, LANES - 2 * GDN_HEADS))).astype(BF16)
    pad_a = (GDN_HEADS, LANES - 2 * GDN_HEADS)
    alog = jnp.pad(a_log.astype(F32), pad_a).reshape(1, LANES)
    dtb = jnp.pad(dt_bias.astype(F32), pad_a).reshape(1, LANES)

    tb = ts * nsub
    tok = lambda w: pl.BlockSpec((1, tb, w), lambda b, j: (b, j, 0))
    q, k, v, gate, gb = pl.pallas_call(
        functools.partial(_gdn_in_body, ts=ts, nsub=nsub),
        grid=(B, S // tb),
        in_specs=[tok(D), _const_spec((1, D)), _const_spec(w_all.shape), _const_spec(w_ba.shape),
                  _const_spec(conv_w.shape), _const_spec((1, LANES)), _const_spec((1, LANES))],
        out_specs=[tok(GDN_QK), tok(GDN_QK), tok(GDN_V), tok(GDN_V), tok(LANES)],
        out_shape=[jax.ShapeDtypeStruct((B, S, GDN_QK), BF16),
                   jax.ShapeDtypeStruct((B, S, GDN_QK), BF16),
                   jax.ShapeDtypeStruct((B, S, GDN_V), BF16),
                   jax.ShapeDtypeStruct((B, S, GDN_V), BF16),
                   jax.ShapeDtypeStruct((B, S, LANES), F32)],
        scratch_shapes=[pltpu.VMEM((CONV_HALO + ts, nch), F32)],
        compiler_params=_params(2),
        name="gdn_in",
    )(x, ng.reshape(1, D), w_all, w_ba, conv_w, alog, dtb)

    gbt = jnp.transpose(gb[:, :, :2 * GDN_HEADS], (0, 2, 1))
    n = GDN_UNIT
    nb = 2 if B % 2 == 0 else 1
    utok = lambda w: pl.BlockSpec((nb, n, w), lambda b, j: (b, j, 0))
    o = pl.pallas_call(
        _gdn_core_body,
        grid=(B // nb, S // n),
        in_specs=[utok(GDN_QK), utok(GDN_QK), utok(GDN_V), utok(LANES),
                  pl.BlockSpec((nb, 2 * GDN_HEADS, n), lambda b, j: (b, 0, j)),
                  utok(GDN_V), _const_spec((1, GDN_DV))],
        out_specs=utok(GDN_V),
        out_shape=jax.ShapeDtypeStruct((B, S, GDN_V), BF16),
        scratch_shapes=[pltpu.VMEM((nb * GDN_HEADS, GDN_DK, GDN_DV), F32)],
        compiler_params=_params(2),
        name="gdn_core",
    )(q, k, v, gb, gbt, gate, norm_g.reshape(1, GDN_DV))

    return _proj_out(x, o, w_out, ts=ts_out, name="gdn_out")


def _mla_in_body(x_ref, pos_ref, ng_ref, win_ref, qg_ref, wuq_ref, kvg_ref, wukv_ref, inv_ref,
                 q_ref, k_ref, v_ref, gate_ref):
    c1 = MLA_Q_LORA
    c2 = c1 + MLA_KV_LORA
    c3 = c2 + LANES
    x = x_ref[0]
    h = _rms(x, ng_ref[...]).astype(BF16)
    cq = jnp.dot(h, win_ref[:, :c1], preferred_element_type=F32)
    ckv = jnp.dot(h, win_ref[:, c1:c2], preferred_element_type=F32)
    kr = jnp.dot(h, win_ref[:, c2:c3], preferred_element_type=F32)
    gate = jnp.dot(h, win_ref[:, c3:], preferred_element_type=F32)
    gate_ref[0] = _silu(gate).astype(BF16)

    half = MLA_ROPE // 2
    ang = pos_ref[0].astype(F32) * inv_ref[...]
    cos = jnp.cos(ang)
    sin = jnp.sin(ang)
    lane = lax.broadcasted_iota(jnp.int32, (1, LANES), 1)
    is_x1 = (lane // half) % 2 == 0
    sin_s = sin * jnp.where(is_x1, -1.0, 1.0)

    def rope(t):
        partner = jnp.where(is_x1, pltpu.roll(t, LANES - half, 1), pltpu.roll(t, half, 1))
        return t * cos + partner * sin_s

    k_rope_t = rope(kr).T.astype(BF16)
    qf = jnp.dot(_rms(cq, qg_ref[...]).astype(BF16), wuq_ref[...],
                 preferred_element_type=F32) * (MLA_QK ** -0.5 * LOG2E)
    kv = jnp.dot(_rms(ckv, kvg_ref[...]).astype(BF16), wukv_ref[...], preferred_element_type=F32)
    nope_w = MLA_HEADS * MLA_NOPE
    q_pad = jnp.zeros((qf.shape[0], MLA_QPAD - MLA_QK), F32)
    for hh in range(MLA_HEADS):
        if hh % 2 == 0:
            r0 = nope_w + (hh // 2) * LANES
            roped = rope(qf[:, r0:r0 + LANES])
        e = (hh % 2) * MLA_ROPE
        q_ref[0, hh] = jnp.concatenate(
            [qf[:, hh * MLA_NOPE:(hh + 1) * MLA_NOPE], roped[:, e:e + MLA_ROPE], q_pad],
            axis=1).astype(BF16)
        k0 = hh * (MLA_NOPE + MLA_V)
        k_ref[0, hh, 0:MLA_NOPE, :] = kv[:, k0:k0 + MLA_NOPE].T.astype(BF16)
        k_ref[0, hh, MLA_NOPE:, :] = k_rope_t
        v_ref[0, hh, :, :MLA_V] = kv[:, k0 + MLA_NOPE:k0 + MLA_NOPE + MLA_V].astype(BF16)
        v_ref[0, hh, :, MLA_V:] = jnp.ones((kv.shape[0], LANES), BF16)


def _attn_body(q_ref, k_ref, v_ref, gate_ref, o_ref, s_ref, p_ref, m_ref, a_ref, acc_ref, *, tq, tk):
    i = pl.program_id(2)
    nh = q_ref.shape[1]
    m_ref[...] = jnp.full(m_ref.shape, NEG, F32)
    acc_ref[...] = jnp.zeros(acc_ref.shape, F32)

    def step(c0, width, masked):
        def scores(h):
            s_ref[h, :, :width] = jnp.dot(q_ref[0, h], k_ref[0, h, :, pl.ds(c0, width)],
                                          preferred_element_type=F32)

        def softmax(h):
            for r0 in range(0, tq, ATTN_STRIP):
                rs = slice(r0, r0 + ATTN_STRIP)
                pieces = []
                for c in range(0, width, LANES):
                    if masked and c >= r0 + ATTN_STRIP:
                        pieces.append(None)
                        continue
                    sc = s_ref[h, rs, c:c + LANES]
                    if masked and c + LANES - 1 > r0:
                        sub = lax.broadcasted_iota(jnp.int32, (ATTN_STRIP, 1), 0)
                        ln = lax.broadcasted_iota(jnp.int32, (1, LANES), 1)
                        sc = jnp.where(c + ln <= r0 + sub, sc, NEG)
                    pieces.append(sc)
                visible = [sc for sc in pieces if sc is not None]
                mx = visible[0]
                for sc in visible[1:]:
                    mx = jnp.maximum(mx, sc)
                m_old = m_ref[h, rs, :]
                m_new = jnp.maximum(m_old, jnp.max(mx, axis=-1, keepdims=True))
                a_ref[h, rs, :] = jnp.exp2(m_old - m_new)
                m_ref[h, rs, :] = m_new
                zero = jnp.zeros((ATTN_STRIP, LANES), F32)
                p_ref[h, rs, :width] = jnp.concatenate(
                    [zero if sc is None else jnp.exp2(sc - m_new) for sc in pieces],
                    axis=1).astype(BF16)


        def values(h):
            pv = jnp.dot(p_ref[h, :, :width], v_ref[0, h, pl.ds(c0, width), :],
                         preferred_element_type=F32)
            a = a_ref[h]
            acc_ref[h, :, :MLA_V] = a * acc_ref[h, :, :MLA_V] + pv[:, :MLA_V]
            acc_ref[h, :, MLA_V:] = a * acc_ref[h, :, MLA_V:] + pv[:, MLA_V:]

        for stage in (scores, softmax, values):
            for h in range(nh):
                stage(h)

    n_wide = (i * tq) // tk

    def wide_step(kb, carry):
        step(pl.multiple_of(kb * tk, tk), tk, False)
        return carry

    lax.fori_loop(0, n_wide, wide_step, 0)
    for d in range(tk // tq - 1):
        @pl.when(n_wide * tk + d * tq < i * tq)
        def _():
            step(pl.multiple_of(n_wide * tk + d * tq, tq), tq, False)
    step(pl.multiple_of(i * tq, tq), tq, True)
    for h in range(nh):
        cols = slice(h * MLA_V, (h + 1) * MLA_V)
        o = acc_ref[h, :, :MLA_V] / acc_ref[h, :, MLA_V:]
        o_ref[0, :, cols] = (o * gate_ref[0, :, cols].astype(F32)).astype(o_ref.dtype)


def _mla_layer(x, pos, ng, w_in, q_norm_g, w_uq, kv_norm_g, w_ukv, w_out, *, ts=256, ts_out=1024, tq=512, tk=1024,
               nh=4):
    B, S, D = x.shape
    H = MLA_HEADS
    half = MLA_ROPE // 2
    c1 = MLA_Q_LORA
    c2 = c1 + MLA_KV_LORA
    c3 = c2 + MLA_ROPE
    w_in_p = jnp.concatenate(
        [w_in[:, :c3], jnp.zeros((D, LANES - MLA_ROPE), w_in.dtype), w_in[:, c3:]],
        axis=1).astype(BF16)
    wq = w_uq.reshape(MLA_Q_LORA, H, MLA_QK)
    wq_p = jnp.concatenate([wq[:, :, :MLA_NOPE].reshape(MLA_Q_LORA, H * MLA_NOPE),
                            wq[:, :, MLA_NOPE:].reshape(MLA_Q_LORA, H * MLA_ROPE)],
                           axis=1).astype(BF16)
    inv = ROPE_THETA ** (-jnp.arange(half, dtype=F32) / half)
    inv = jnp.tile(inv, LANES // half).reshape(1, LANES)
    width = H * MLA_V

    tok = lambda w: pl.BlockSpec((1, ts, w), lambda b, j: (b, j, 0))
    head = lambda w: pl.BlockSpec((1, H, ts, w), lambda b, j: (b, 0, j, 0))
    q, k, v, gate = pl.pallas_call(
        _mla_in_body,
        grid=(B, S // ts),
        in_specs=[tok(D), tok(1), _const_spec((1, D)), _const_spec(w_in_p.shape),
                  _const_spec((1, MLA_Q_LORA)), _const_spec(wq_p.shape),
                  _const_spec((1, MLA_KV_LORA)), _const_spec(w_ukv.shape), _const_spec((1, LANES))],
        out_specs=[head(MLA_QPAD), pl.BlockSpec((1, H, MLA_QPAD, ts), lambda b, j: (b, 0, 0, j)),
                   head(MLA_V + LANES), tok(width)],
        out_shape=[jax.ShapeDtypeStruct((B, H, S, MLA_QPAD), BF16),
                   jax.ShapeDtypeStruct((B, H, MLA_QPAD, S), BF16),
                   jax.ShapeDtypeStruct((B, H, S, MLA_V + LANES), BF16),
                   jax.ShapeDtypeStruct((B, S, width), BF16)],
        compiler_params=_params(2),
        name="mla_in",
    )(x, pos.reshape(B, S, 1), ng.reshape(1, D), w_in_p, q_norm_g.reshape(1, MLA_Q_LORA), wq_p,
      kv_norm_g.reshape(1, MLA_KV_LORA), w_ukv.astype(BF16), inv)

    o = pl.pallas_call(
        functools.partial(_attn_body, tq=tq, tk=tk),
        grid=(B, H // nh, S // tq),
        in_specs=[pl.BlockSpec((1, nh, tq, MLA_QPAD), lambda b, h, i: (b, h, i, 0)),
                  pl.BlockSpec((1, nh, MLA_QPAD, S), lambda b, h, i: (b, h, 0, 0)),
                  pl.BlockSpec((1, nh, S, MLA_V + LANES), lambda b, h, i: (b, h, 0, 0)),
                  pl.BlockSpec((1, tq, nh * MLA_V), lambda b, h, i: (b, i, h))],
        out_specs=pl.BlockSpec((1, tq, nh * MLA_V), lambda b, h, i: (b, i, h)),
        out_shape=jax.ShapeDtypeStruct((B, S, width), BF16),
        scratch_shapes=[pltpu.VMEM((nh, tq, tk), F32), pltpu.VMEM((nh, tq, tk), BF16),
                        pltpu.VMEM((nh, tq, LANES), F32), pltpu.VMEM((nh, tq, LANES), F32),
                        pltpu.VMEM((nh, tq, MLA_V + LANES), F32)],
        compiler_params=_params(3, ATTN_VMEM_LIMIT_V7X),
        name="mla_attn",
    )(q, k, v, gate)

    return _proj_out(x, o, w_out, ts=ts_out, name="mla_out")


def kernel(x, positions, norm_g, pool_w_in, pool_w_grp, pool_scale, pool_w_out, gdn_w_in, gdn_conv, gdn_a_log, gdn_dt_bias, gdn_norm_g, gdn_w_out, mla_w_in, mla_q_norm_g, mla_w_uq, mla_kv_norm_g, mla_w_ukv, mla_w_out, final_g):
    depth = norm_g.shape[0]
    pool_w = (pool_w_in.astype(BF16), pool_w_grp.astype(BF16), pool_w_out.astype(BF16))
    for i in range(depth):
        kind, j = i % N_MIXERS, i // N_MIXERS
        last = i == depth - 1
        if kind == 0:
            x = _pool_layer(x, norm_g[i], pool_w[0], pool_w[1], pool_scale[j], pool_w[2], j,
                            final_g if last else None)
        elif kind == 1:
            x = _gdn_layer(x, norm_g[i], gdn_w_in[j], gdn_conv[j], gdn_a_log[j], gdn_dt_bias[j],
                           gdn_norm_g[j], gdn_w_out[j])
        else:
            x = _mla_layer(x, positions, norm_g[i], mla_w_in[j], mla_q_norm_g[j], mla_w_uq[j],
                           mla_kv_norm_g[j], mla_w_ukv[j], mla_w_out[j])
        if last and kind != 0:
            x = _final_norm(x, final_g)
    return x


def _final_norm_body(x_ref, g_ref, o_ref):
    o_ref[0] = _rms(x_ref[0], g_ref[...])


def _final_norm(x, g, *, ts=512):
    B, S, D = x.shape
    tok = pl.BlockSpec((1, ts, D), lambda b, j: (b, j, 0))
    return pl.pallas_call(
        _final_norm_body, grid=(B, S // ts), in_specs=[tok, _const_spec((1, D))], out_specs=tok,
        out_shape=jax.ShapeDtypeStruct((B, S, D), F32), compiler_params=_params(2),
        name="final_norm",
    )(x, g.reshape(1, D))
```

```python
import functools

import jax
import jax.numpy as jnp
from jax import lax
from jax.experimental import pallas as pl
from jax.experimental.pallas import tpu as pltpu

F32 = jnp.float32
BF16 = jnp.bfloat16
EPS = 1e-6
NEG = -1e30
VMEM_LIMIT_V7X = 56 * 2 ** 20
ATTN_VMEM_LIMIT_V7X = 60 * 2 ** 20
LANES = 128
SUBLANES = 8
MXU_COLS = 256

N_MIXERS = 3

POOL_WINDOWS = (2, 4, 8, 16)
POOL_HALO = SUBLANES * (max(POOL_WINDOWS).bit_length() - 1)

GDN_HEADS = 8
GDN_DK = 128
GDN_DV = 256
GDN_CONV = 4
GDN_CHUNK = 64
GDN_QK = GDN_HEADS * GDN_DK
GDN_V = GDN_HEADS * GDN_DV
GDN_UNIT = 2 * GDN_CHUNK
CONV_HALO = 8

MLA_HEADS = 16
MLA_NOPE = 128
MLA_ROPE = 64
MLA_V = 128
MLA_Q_LORA = 768
MLA_KV_LORA = 512
MLA_QK = MLA_NOPE + MLA_ROPE
MLA_QPAD = 256
ROPE_THETA = 10000.0
ATTN_STRIP = 32
LOG2E = 1.4426950408889634


def _const_spec(shape):
    zeros = (0,) * len(shape)
    return pl.BlockSpec(shape, lambda *_: zeros, pipeline_mode=pl.Buffered(1))


def _layer_spec(stacked_shape, layer):
    index = (layer,) + (0,) * (len(stacked_shape) - 1)
    return pl.BlockSpec((None,) + tuple(stacked_shape[1:]), lambda *_: index,
                        pipeline_mode=pl.Buffered(1))


def _params(n_axes, vmem_limit=VMEM_LIMIT_V7X):
    return pltpu.CompilerParams(dimension_semantics=("arbitrary",) * n_axes,
                                vmem_limit_bytes=vmem_limit)


def _rms(x, g):
    return x * lax.rsqrt(jnp.mean(x * x, axis=-1, keepdims=True) + EPS) * g


def _silu_of_twice(h):
    return h + h * jnp.tanh(h)


def _silu(x):
    return _silu_of_twice(0.5 * x)


def _mm(a, b):
    return jnp.dot(a.astype(BF16), b.astype(BF16), preferred_element_type=F32)


def _mm_nt(a, b):
    return lax.dot_general(a.astype(BF16), b.astype(BF16), (((1,), (1,)), ((), ())),
                           preferred_element_type=F32)


def _pool_body(x_ref, ng_ref, win_ref, wgrp_ref, sc_ref, wout_ref, fg_ref, o_ref, ubuf_ref,
               sa_ref, sb_ref, *, ts, nsub, width, final):
    j = pl.program_id(1)
    grp = width // len(POOL_WINDOWS)

    def window_sum(w, c0):
        levels = w.bit_length() - 1
        src, cols, shift = ubuf_ref, slice(c0, c0 + grp), 1
        for lv in range(levels):
            r0 = POOL_HALO - SUBLANES * (levels - 1 - lv)
            n = POOL_HALO + ts - r0
            val = src[r0:r0 + n, cols] + src[r0 - shift:r0 - shift + n, cols]
            if lv == levels - 1:
                return val
            dst = (sa_ref, sb_ref)[lv % 2]
            dst[r0:r0 + n, :] = val
            src, cols, shift = dst, slice(0, grp), 2 * shift

    @pl.when(j == 0)
    def _():
        ubuf_ref[0:POOL_HALO, :] = jnp.zeros((POOL_HALO, width), F32)

    for st in range(nsub):
        rows = slice(st * ts, (st + 1) * ts)
        x = x_ref[0, rows, :]
        h = _rms(x, ng_ref[...]).astype(BF16)
        u = jnp.dot(h, win_ref[:, :width], preferred_element_type=F32)
        gate = jnp.dot(h, win_ref[:, width:], preferred_element_type=F32)
        ubuf_ref[POOL_HALO:POOL_HALO + ts, :] = u
        t = (j * nsub + st) * ts + lax.broadcasted_iota(jnp.int32, (ts, 1), 0)
        parts = []
        for gi, w in enumerate(POOL_WINDOWS):
            c0 = gi * grp
            cnt = jnp.minimum(t + 1, w).astype(F32)
            p = window_sum(w, c0) / cnt - u[:, c0:c0 + grp]
            parts.append(jnp.dot(p.astype(BF16), wgrp_ref[gi], preferred_element_type=F32))
        pg = jnp.concatenate(parts, axis=1)
        y = pg * sc_ref[...] * _silu(gate)
        out = x + jnp.dot(y.astype(BF16), wout_ref[...], preferred_element_type=F32)
        ubuf_ref[0:POOL_HALO, :] = u[ts - POOL_HALO:, :]
        if final:
            out = _rms(out, fg_ref[...])
        o_ref[0, rows, :] = out


def _pool_layer(x, ng, w_in, w_grp, scale, w_out, layer, final_g, *, ts=512, nsub=2):
    B, S, D = x.shape
    width = w_out.shape[1]
    final = final_g is not None
    fg = final_g if final else jnp.ones((D,), F32)
    body = functools.partial(_pool_body, ts=ts, nsub=nsub, width=width, final=final)
    tb = ts * nsub
    return pl.pallas_call(
        body,
        grid=(B, S // tb),
        in_specs=[
            pl.BlockSpec((1, tb, D), lambda b, j: (b, j, 0)),
            _const_spec((1, D)),
            _layer_spec(w_in.shape, layer),
            _layer_spec(w_grp.shape, layer),
            _const_spec((1, width)),
            _layer_spec(w_out.shape, layer),
            _const_spec((1, D)),
        ],
        out_specs=pl.BlockSpec((1, tb, D), lambda b, j: (b, j, 0)),
        out_shape=jax.ShapeDtypeStruct((B, S, D), F32),
        scratch_shapes=[pltpu.VMEM((POOL_HALO + ts, width), F32),
                        pltpu.VMEM((POOL_HALO + ts, width // len(POOL_WINDOWS)), F32),
                        pltpu.VMEM((POOL_HALO + ts, width // len(POOL_WINDOWS)), F32)],
        compiler_params=_params(2),
        name="pool_layer",
    )(x, ng.reshape(1, D), w_in, w_grp, scale.reshape(1, width), w_out, fg.reshape(1, D))


def _gdn_in_body(x_ref, ng_ref, win_ref, wba_ref, conv_ref, alog_ref, dtb_ref,
                 q_ref, k_ref, v_ref, gate_ref, gb_ref, cbuf_ref, *, ts, nsub):
    j = pl.program_id(1)
    nch = 2 * GDN_QK + GDN_V

    @pl.when(j == 0)
    def _():
        cbuf_ref[0:CONV_HALO, :] = jnp.zeros((CONV_HALO, nch), F32)

    lane = lax.broadcasted_iota(jnp.int32, (1, LANES), 1)
    r = lax.broadcasted_iota(jnp.int32, (ts, 1), 0)
    c = lax.broadcasted_iota(jnp.int32, (1, ts), 1)
    tri = jnp.where(((r // GDN_CHUNK) == (c // GDN_CHUNK)) & (c <= r), 1.0, 0.0).astype(BF16)

    for st in range(nsub):
        rows = slice(st * ts, (st + 1) * ts)
        x = x_ref[0, rows, :]
        xn = _rms(x, ng_ref[...])
        h = xn.astype(BF16)
        h_half = (0.5 * xn).astype(BF16)

        for c0 in range(0, nch, MXU_COLS):
            cols = slice(c0, c0 + MXU_COLS)
            pre = jnp.dot(h_half, win_ref[:, cols], preferred_element_type=F32)
            cbuf_ref[CONV_HALO:CONV_HALO + ts, cols] = pre
            acc = pre * conv_ref[GDN_CONV - 1:GDN_CONV, cols]
            for kk in range(GDN_CONV - 1):
                r0 = CONV_HALO - (GDN_CONV - 1) + kk
                acc = acc + cbuf_ref[r0:r0 + ts, cols] * conv_ref[kk:kk + 1, cols]
            cbuf_ref[0:CONV_HALO, cols] = pre[ts - CONV_HALO:, :]
            a = _silu_of_twice(acc)
            if c0 < 2 * GDN_QK:
                is_q = c0 < GDN_QK
                out_ref, base = (q_ref, c0) if is_q else (k_ref, c0 - GDN_QK)
                for d0 in range(0, MXU_COLS, GDN_DK):
                    ah = a[:, d0:d0 + GDN_DK]
                    an = ah * lax.rsqrt(jnp.sum(ah * ah, axis=-1, keepdims=True) + EPS)
                    if is_q:
                        an = an * (GDN_DK ** -0.5)
                    out_ref[0, rows, base + d0:base + d0 + GDN_DK] = an.astype(BF16)
            else:
                v0 = c0 - 2 * GDN_QK
                v_ref[0, rows, v0:v0 + MXU_COLS] = a.astype(BF16)
        for c0 in range(0, GDN_V, MXU_COLS):
            gate = jnp.dot(h_half, win_ref[:, nch + c0:nch + c0 + MXU_COLS],
                           preferred_element_type=F32)
            gate_ref[0, rows, c0:c0 + MXU_COLS] = _silu_of_twice(gate).astype(BF16)

        ba = jnp.dot(h, wba_ref[...], preferred_element_type=F32)
        z = ba + dtb_ref[...]
        softplus = jnp.maximum(z, 0.0) + jnp.log1p(jnp.exp(-jnp.abs(z)))
        g = jnp.where((lane >= GDN_HEADS) & (lane < 2 * GDN_HEADS),
                      -jnp.exp(alog_ref[...]) * softplus, 0.0)
        g_hi = g.astype(BF16)
        r1 = g - g_hi.astype(F32)
        g_mid = r1.astype(BF16)
        g_lo = (r1 - g_mid.astype(F32)).astype(BF16)
        gc = (jnp.dot(tri, g_hi, preferred_element_type=F32)
              + jnp.dot(tri, g_mid, preferred_element_type=F32)
              + jnp.dot(tri, g_lo, preferred_element_type=F32))
        gb_ref[0, rows, :] = jnp.where(lane < GDN_HEADS, jax.nn.sigmoid(ba), gc)


def _gdn_core_body(q_ref, k_ref, v_ref, gb_ref, gbt_ref, gate_ref, ng_ref, o_ref, s_ref):
    j = pl.program_id(1)

    @pl.when(j == 0)
    def _():
        s_ref[...] = jnp.zeros_like(s_ref)

    n = GDN_UNIT
    r = lax.broadcasted_iota(jnp.int32, (n, 1), 0)
    c = lax.broadcasted_iota(jnp.int32, (1, n), 1)
    same = (r // GDN_CHUNK) == (c // GDN_CHUNK)
    causal = same & (c <= r)
    strict = same & (c < r)
    eye = jnp.where(r == c, 1.0, 0.0)
    zeros_c = jnp.zeros((GDN_CHUNK, GDN_DV), F32)
    nb = q_ref.shape[0]
    hs = range(nb * GDN_HEADS)
    bi = [ci // GDN_HEADS for ci in hs]
    hi = [ci % GDN_HEADS for ci in hs]
    gb = [gb_ref[b] for b in range(nb)]
    gbt = [gbt_ref[b] for b in range(nb)]
    beta_c = [gb[bi[h]][:, hi[h]:hi[h] + 1] for h in hs]
    gc_c = [gb[bi[h]][:, GDN_HEADS + hi[h]:GDN_HEADS + hi[h] + 1] for h in hs]
    gc_r = [gbt[bi[h]][GDN_HEADS + hi[h]:GDN_HEADS + hi[h] + 1, :] for h in hs]
    q = [q_ref[bi[h], :, hi[h] * GDN_DK:(hi[h] + 1) * GDN_DK] for h in hs]
    k = [k_ref[bi[h], :, hi[h] * GDN_DK:(hi[h] + 1) * GDN_DK] for h in hs]
    kf = [k[h].astype(F32) for h in hs]
    kb = [kf[h] * beta_c[h] for h in hs]
    decay = [jnp.exp(jnp.where(causal, gc_c[h] - gc_r[h], NEG)) for h in hs]
    kkqk = [_mm_nt(jnp.concatenate([kb[h].astype(BF16), q[h]], axis=0), k[h]) for h in hs]
    lmat = [jnp.where(strict, kkqk[h][:n] * decay[h], 0.0) for h in hs]
    attn = [kkqk[h][n:] * decay[h] for h in hs]
    ch = GDN_CHUNK
    eye_s = eye[:ch] + eye[ch:]

    def block_diag(side):
        return jnp.where(same, jnp.concatenate([side, side], axis=0), 0.0)

    l_s = [lmat[h][:ch] + lmat[h][ch:] for h in hs]
    xs = [eye_s - l_s[h] for h in hs]
    ps = [_mm(l_s[h], lmat[h]) for h in hs]
    for _ in range(4):
        res = [_mm(jnp.concatenate([xs[h], ps[h]], axis=0), block_diag(ps[h])) for h in hs]
        xs = [xs[h] + res[h][:ch] for h in hs]
        ps = [res[h][ch:] for h in hs]
    res = [_mm(xs[h], block_diag(ps[h])) for h in hs]
    xm = [block_diag(xs[h] + res[h]) for h in hs]
    eg = [jnp.exp(gc_c[h]) for h in hs]
    rhs = [jnp.concatenate(
        [v_ref[bi[h], :, hi[h] * GDN_DV:(hi[h] + 1) * GDN_DV].astype(F32) * beta_c[h],
         kb[h] * eg[h]], axis=1) for h in hs]
    sol = [rhs[h] + _mm(xm[h] - eye, rhs[h]) for h in hs]
    qg = [q[h].astype(F32) * eg[h] for h in hs]
    k_t = [kf[h].T for h in hs]
    state = [s_ref[h] for h in hs]
    for cc in range(2):
        rows = slice(cc * GDN_CHUNK, (cc + 1) * GDN_CHUNK)
        in_chunk = (c // GDN_CHUNK) == cc
        g_last = [gc_r[h][:, (cc + 1) * GDN_CHUNK - 1:(cc + 1) * GDN_CHUNK] for h in hs]
        res = [_mm(jnp.concatenate([sol[h][rows, GDN_DV:], qg[h][rows]], axis=0), state[h])
               for h in hs]
        v_new = [sol[h][rows, :GDN_DV] - res[h][:GDN_CHUNK] for h in hs]
        v_all = [jnp.concatenate([v_new[h], zeros_c] if cc == 0 else [zeros_c, v_new[h]], axis=0)
                 for h in hs]
        k_dec_t = [k_t[h] * jnp.exp(jnp.where(in_chunk, g_last[h] - gc_r[h], NEG)) for h in hs]
        upd = [_mm(jnp.concatenate([attn[h][rows], k_dec_t[h]], axis=0), v_all[h]) for h in hs]
        o_c = [res[h][GDN_CHUNK:] + upd[h][:GDN_CHUNK] for h in hs]
        state = [state[h] * jnp.exp(g_last[h]) + upd[h][GDN_CHUNK:] for h in hs]
        for h in hs:
            cols = slice(hi[h] * GDN_DV, (hi[h] + 1) * GDN_DV)
            gated = _rms(o_c[h], ng_ref[...]) * gate_ref[bi[h], rows, cols].astype(F32)
            o_ref[bi[h], rows, cols] = gated.astype(o_ref.dtype)
    for h in hs:
        s_ref[h] = state[h]


def _proj_out_body(x_ref, o_ref, wout_ref, y_ref):
    y_ref[0] = x_ref[0] + jnp.dot(o_ref[0], wout_ref[...], preferred_element_type=F32)


def _proj_out(x, o, w_out, *, ts, name):
    B, S, D = x.shape
    tok = lambda w: pl.BlockSpec((1, ts, w), lambda b, j: (b, j, 0))
    return pl.pallas_call(
        _proj_out_body,
        grid=(B, S // ts),
        in_specs=[tok(D), tok(o.shape[-1]), _const_spec(w_out.shape)],
        out_specs=tok(D),
        out_shape=jax.ShapeDtypeStruct((B, S, D), F32),
        compiler_params=_params(2),
        name=name,
    )(x, o, w_out.astype(BF16))


def _gdn_layer(x, ng, w_in, conv_w, a_log, dt_bias, norm_g, w_out, *, ts=256, nsub=1, ts_out=1024):
    B, S, D = x.shape
    nch = 2 * GDN_QK + GDN_V
    w_all = w_in.astype(BF16)
    w_ba = jnp.pad(w_in[:, nch + GDN_V:], ((0, 0), (0, LANES - 2 * GDN_HEADS))).astype(BF16)
    pad_a = (GDN_HEADS, LANES - 2 * GDN_HEADS)
    alog = jnp.pad(a_log.astype(F32), pad_a).reshape(1, LANES)
    dtb = jnp.pad(dt_bias.astype(F32), pad_a).reshape(1, LANES)

    tb = ts * nsub
    tok = lambda w: pl.BlockSpec((1, tb, w), lambda b, j: (b, j, 0))
    q, k, v, gate, gb = pl.pallas_call(
        functools.partial(_gdn_in_body, ts=ts, nsub=nsub),
        grid=(B, S // tb),
        in_specs=[tok(D), _const_spec((1, D)), _const_spec(w_all.shape), _const_spec(w_ba.shape),
                  _const_spec(conv_w.shape), _const_spec((1, LANES)), _const_spec((1, LANES))],
        out_specs=[tok(GDN_QK), tok(GDN_QK), tok(GDN_V), tok(GDN_V), tok(LANES)],
        out_shape=[jax.ShapeDtypeStruct((B, S, GDN_QK), BF16),
                   jax.ShapeDtypeStruct((B, S, GDN_QK), BF16),
                   jax.ShapeDtypeStruct((B, S, GDN_V), BF16),
                   jax.ShapeDtypeStruct((B, S, GDN_V), BF16),
                   jax.ShapeDtypeStruct((B, S, LANES), F32)],
        scratch_shapes=[pltpu.VMEM((CONV_HALO + ts, nch), F32)],
        compiler_params=_params(2),
        name="gdn_in",
    )(x, ng.reshape(1, D), w_all, w_ba, conv_w, alog, dtb)

    gbt = jnp.transpose(gb[:, :, :2 * GDN_HEADS], (0, 2, 1))
    n = GDN_UNIT
    nb = 2 if B % 2 == 0 else 1
    utok = lambda w: pl.BlockSpec((nb, n, w), lambda b, j: (b, j, 0))
    o = pl.pallas_call(
        _gdn_core_body,
        grid=(B // nb, S // n),
        in_specs=[utok(GDN_QK), utok(GDN_QK), utok(GDN_V), utok(LANES),
                  pl.BlockSpec((nb, 2 * GDN_HEADS, n), lambda b, j: (b, 0, j)),
                  utok(GDN_V), _const_spec((1, GDN_DV))],
        out_specs=utok(GDN_V),
        out_shape=jax.ShapeDtypeStruct((B, S, GDN_V), BF16),
        scratch_shapes=[pltpu.VMEM((nb * GDN_HEADS, GDN_DK, GDN_DV), F32)],
        compiler_params=_params(2),
        name="gdn_core",
    )(q, k, v, gb, gbt, gate, norm_g.reshape(1, GDN_DV))

    return _proj_out(x, o, w_out, ts=ts_out, name="gdn_out")


def _mla_in_body(x_ref, pos_ref, ng_ref, win_ref, qg_ref, wuq_ref, kvg_ref, wukv_ref, inv_ref,
                 q_ref, k_ref, v_ref, gate_ref):
    c1 = MLA_Q_LORA
    c2 = c1 + MLA_KV_LORA
    c3 = c2 + LANES
    x = x_ref[0]
    h = _rms(x, ng_ref[...]).astype(BF16)
    cq = jnp.dot(h, win_ref[:, :c1], preferred_element_type=F32)
    ckv = jnp.dot(h, win_ref[:, c1:c2], preferred_element_type=F32)
    kr = jnp.dot(h, win_ref[:, c2:c3], preferred_element_type=F32)
    gate = jnp.dot(h, win_ref[:, c3:], preferred_element_type=F32)
    gate_ref[0] = _silu(gate).astype(BF16)

    half = MLA_ROPE // 2
    ang = pos_ref[0].astype(F32) * inv_ref[...]
    cos = jnp.cos(ang)
    sin = jnp.sin(ang)
    lane = lax.broadcasted_iota(jnp.int32, (1, LANES), 1)
    is_x1 = (lane // half) % 2 == 0
    sin_s = sin * jnp.where(is_x1, -1.0, 1.0)

    def rope(t):
        partner = jnp.where(is_x1, pltpu.roll(t, LANES - half, 1), pltpu.roll(t, half, 1))
        return t * cos + partner * sin_s

    k_rope_t = rope(kr).T.astype(BF16)
    qf = jnp.dot(_rms(cq, qg_ref[...]).astype(BF16), wuq_ref[...],
                 preferred_element_type=F32) * (MLA_QK ** -0.5 * LOG2E)
    kv = jnp.dot(_rms(ckv, kvg_ref[...]).astype(BF16), wukv_ref[...], preferred_element_type=F32)
    nope_w = MLA_HEADS * MLA_NOPE
    q_pad = jnp.zeros((qf.shape[0], MLA_QPAD - MLA_QK), F32)
    for hh in range(MLA_HEADS):
        if hh % 2 == 0:
            r0 = nope_w + (hh // 2) * LANES
            roped = rope(qf[:, r0:r0 + LANES])
        e = (hh % 2) * MLA_ROPE
        q_ref[0, hh] = jnp.concatenate(
            [qf[:, hh * MLA_NOPE:(hh + 1) * MLA_NOPE], roped[:, e:e + MLA_ROPE], q_pad],
            axis=1).astype(BF16)
        k0 = hh * (MLA_NOPE + MLA_V)
        k_ref[0, hh, 0:MLA_NOPE, :] = kv[:, k0:k0 + MLA_NOPE].T.astype(BF16)
        k_ref[0, hh, MLA_NOPE:, :] = k_rope_t
        v_ref[0, hh, :, :MLA_V] = kv[:, k0 + MLA_NOPE:k0 + MLA_NOPE + MLA_V].astype(BF16)
        v_ref[0, hh, :, MLA_V:] = jnp.ones((kv.shape[0], LANES), BF16)


def _attn_body(q_ref, k_ref, v_ref, gate_ref, o_ref, s_ref, p_ref, m_ref, a_ref, acc_ref, *, tq, tk):
    i = pl.program_id(2)
    nh = q_ref.shape[1]
    m_ref[...] = jnp.full(m_ref.shape, NEG, F32)
    acc_ref[...] = jnp.zeros(acc_ref.shape, F32)

    def step(c0, parts, masked):
        chains = [(h,) + part for part in parts for h in range(nh)]

        def scores(h, lo, hi, width):
            s_ref[h, lo:hi, :width] = jnp.dot(q_ref[0, h, lo:hi, :],
                                              k_ref[0, h, :, pl.ds(c0, width)],
                                              preferred_element_type=F32)

        def softmax(h, lo, hi, width):
            for r0 in range(lo, hi, ATTN_STRIP):
                rs = slice(r0, r0 + ATTN_STRIP)
                pieces = []
                for c in range(0, width, LANES):
                    if masked and c >= r0 + ATTN_STRIP:
                        pieces.append(None)
                        continue
                    sc = s_ref[h, rs, c:c + LANES]
                    if masked and c + LANES - 1 > r0:
                        sub = lax.broadcasted_iota(jnp.int32, (ATTN_STRIP, 1), 0)
                        ln = lax.broadcasted_iota(jnp.int32, (1, LANES), 1)
                        sc = jnp.where(c + ln <= r0 + sub, sc, NEG)
                    pieces.append(sc)
                visible = [sc for sc in pieces if sc is not None]
                mx = visible[0]
                for sc in visible[1:]:
                    mx = jnp.maximum(mx, sc)
                m_old = m_ref[h, rs, :]
                m_new = jnp.maximum(m_old, jnp.max(mx, axis=-1, keepdims=True))
                a_ref[h, rs, :] = jnp.exp2(m_old - m_new)
                m_ref[h, rs, :] = m_new
                zero = jnp.zeros((ATTN_STRIP, LANES), F32)
                p_ref[h, rs, :width] = jnp.concatenate(
                    [zero if sc is None else jnp.exp2(sc - m_new) for sc in pieces],
                    axis=1).astype(BF16)

        def values(h, lo, hi, width):
            pv = jnp.dot(p_ref[h, lo:hi, :width], v_ref[0, h, pl.ds(c0, width), :],
                         preferred_element_type=F32)
            a = a_ref[h, lo:hi, :]
            acc_ref[h, lo:hi, :MLA_V] = a * acc_ref[h, lo:hi, :MLA_V] + pv[:, :MLA_V]
            acc_ref[h, lo:hi, MLA_V:] = a * acc_ref[h, lo:hi, MLA_V:] + pv[:, MLA_V:]

        for stage in (scores, softmax, values):
            for chain in chains:
                stage(*chain)

    n_wide = (i * tq) // tk

    def wide_step(kb, carry):
        step(pl.multiple_of(kb * tk, tk), [(0, tq, tk)], False)
        return carry

    lax.fori_loop(0, n_wide, wide_step, 0)
    for d in range(tk // tq - 1):
        @pl.when(n_wide * tk + d * tq < i * tq)
        def _():
            step(pl.multiple_of(n_wide * tk + d * tq, tq), [(0, tq, tq)], False)
    step(pl.multiple_of(i * tq, tq), [(0, tq // 2, tq // 2), (tq // 2, tq, tq)], True)
    for h in range(nh):
        cols = slice(h * MLA_V, (h + 1) * MLA_V)
        o = acc_ref[h, :, :MLA_V] / acc_ref[h, :, MLA_V:]
        o_ref[0, :, cols] = (o * gate_ref[0, :, cols].astype(F32)).astype(o_ref.dtype)


def _mla_layer(x, pos, ng, w_in, q_norm_g, w_uq, kv_norm_g, w_ukv, w_out, *, ts=256, ts_out=1024, tq=512, tk=1024,
               nh=4):
    B, S, D = x.shape
    H = MLA_HEADS
    half = MLA_ROPE // 2
    c1 = MLA_Q_LORA
    c2 = c1 + MLA_KV_LORA
    c3 = c2 + MLA_ROPE
    w_in_p = jnp.concatenate(
        [w_in[:, :c3], jnp.zeros((D, LANES - MLA_ROPE), w_in.dtype), w_in[:, c3:]],
        axis=1).astype(BF16)
    wq = w_uq.reshape(MLA_Q_LORA, H, MLA_QK)
    wq_p = jnp.concatenate([wq[:, :, :MLA_NOPE].reshape(MLA_Q_LORA, H * MLA_NOPE),
                            wq[:, :, MLA_NOPE:].reshape(MLA_Q_LORA, H * MLA_ROPE)],
                           axis=1).astype(BF16)
    inv = ROPE_THETA ** (-jnp.arange(half, dtype=F32) / half)
    inv = jnp.tile(inv, LANES // half).reshape(1, LANES)
    width = H * MLA_V

    tok = lambda w: pl.BlockSpec((1, ts, w), lambda b, j: (b, j, 0))
    head = lambda w: pl.BlockSpec((1, H, ts, w), lambda b, j: (b, 0, j, 0))
    q, k, v, gate = pl.pallas_call(
        _mla_in_body,
        grid=(B, S // ts),
        in_specs=[tok(D), tok(1), _const_spec((1, D)), _const_spec(w_in_p.shape),
                  _const_spec((1, MLA_Q_LORA)), _const_spec(wq_p.shape),
                  _const_spec((1, MLA_KV_LORA)), _const_spec(w_ukv.shape), _const_spec((1, LANES))],
        out_specs=[head(MLA_QPAD), pl.BlockSpec((1, H, MLA_QPAD, ts), lambda b, j: (b, 0, 0, j)),
                   head(MLA_V + LANES), tok(width)],
        out_shape=[jax.ShapeDtypeStruct((B, H, S, MLA_QPAD), BF16),
                   jax.ShapeDtypeStruct((B, H, MLA_QPAD, S), BF16),
                   jax.ShapeDtypeStruct((B, H, S, MLA_V + LANES), BF16),
                   jax.ShapeDtypeStruct((B, S, width), BF16)],
        compiler_params=_params(2),
        name="mla_in",
    )(x, pos.reshape(B, S, 1), ng.reshape(1, D), w_in_p, q_norm_g.reshape(1, MLA_Q_LORA), wq_p,
      kv_norm_g.reshape(1, MLA_KV_LORA), w_ukv.astype(BF16), inv)

    o = pl.pallas_call(
        functools.partial(_attn_body, tq=tq, tk=tk),
        grid=(B, H // nh, S // tq),
        in_specs=[pl.BlockSpec((1, nh, tq, MLA_QPAD), lambda b, h, i: (b, h, i, 0)),
                  pl.BlockSpec((1, nh, MLA_QPAD, S), lambda b, h, i: (b, h, 0, 0)),
                  pl.BlockSpec((1, nh, S, MLA_V + LANES), lambda b, h, i: (b, h, 0, 0)),
                  pl.BlockSpec((1, tq, nh * MLA_V), lambda b, h, i: (b, i, h))],
        out_specs=pl.BlockSpec((1, tq, nh * MLA_V), lambda b, h, i: (b, i, h)),
        out_shape=jax.ShapeDtypeStruct((B, S, width), BF16),
        scratch_shapes=[pltpu.VMEM((nh, tq, tk), F32), pltpu.VMEM((nh, tq, tk), BF16),
                        pltpu.VMEM((nh, tq, LANES), F32), pltpu.VMEM((nh, tq, LANES), F32),
                        pltpu.VMEM((nh, tq, MLA_V + LANES), F32)],
        compiler_params=_params(3, ATTN_VMEM_LIMIT_V7X),
        name="mla_attn",
    )(q, k, v, gate)

    return _proj_out(x, o, w_out, ts=ts_out, name="mla_out")


def kernel(x, positions, norm_g, pool_w_in, pool_w_grp, pool_scale, pool_w_out, gdn_w_in, gdn_conv, gdn_a_log, gdn_dt_bias, gdn_norm_g, gdn_w_out, mla_w_in, mla_q_norm_g, mla_w_uq, mla_kv_norm_g, mla_w_ukv, mla_w_out, final_g):
    depth = norm_g.shape[0]
    pool_w = (pool_w_in.astype(BF16), pool_w_grp.astype(BF16), pool_w_out.astype(BF16))
    for i in range(depth):
        kind, j = i % N_MIXERS, i // N_MIXERS
        last = i == depth - 1
        if kind == 0:
            x = _pool_layer(x, norm_g[i], pool_w[0], pool_w[1], pool_scale[j], pool_w[2], j,
                            final_g if last else None)
        elif kind == 1:
            x = _gdn_layer(x, norm_g[i], gdn_w_in[j], gdn_conv[j], gdn_a_log[j], gdn_dt_bias[j],
                           gdn_norm_g[j], gdn_w_out[j])
        else:
            x = _mla_layer(x, positions, norm_g[i], mla_w_in[j], mla_q_norm_g[j], mla_w_uq[j],
                           mla_kv_norm_g[j], mla_w_ukv[j], mla_w_out[j])
        if last and kind != 0:
            x = _final_norm(x, final_g)
    return x


def _final_norm_body(x_ref, g_ref, o_ref):
    o_ref[0] = _rms(x_ref[0], g_ref[...])


def _final_norm(x, g, *, ts=512):
    B, S, D = x.shape
    tok = pl.BlockSpec((1, ts, D), lambda b, j: (b, j, 0))
    return pl.pallas_call(
        _final_norm_body, grid=(B, S // ts), in_specs=[tok, _const_spec((1, D))], out_specs=tok,
        out_shape=jax.ShapeDtypeStruct((B, S, D), F32), compiler_params=_params(2),
        name="final_norm",
    )(x, g.reshape(1, D))
```

```python
import functools

import jax
import jax.numpy as jnp
from jax import lax
from jax.experimental import pallas as pl
from jax.experimental.pallas import tpu as pltpu

F32 = jnp.float32
BF16 = jnp.bfloat16
EPS = 1e-6
NEG = -1e30
VMEM_LIMIT_V7X = 56 * 2 ** 20
ATTN_VMEM_LIMIT_V7X = 60 * 2 ** 20
LANES = 128
SUBLANES = 8
MXU_COLS = 256

N_MIXERS = 3

POOL_WINDOWS = (2, 4, 8, 16)
POOL_HALO = SUBLANES * (max(POOL_WINDOWS).bit_length() - 1)

GDN_HEADS = 8
GDN_DK = 128
GDN_DV = 256
GDN_CONV = 4
GDN_CHUNK = 64
GDN_QK = GDN_HEADS * GDN_DK
GDN_V = GDN_HEADS * GDN_DV
GDN_UNIT = 2 * GDN_CHUNK
CONV_HALO = SUBLANES

MLA_HEADS = 16
MLA_NOPE = 128
MLA_ROPE = 64
MLA_V = 128
MLA_Q_LORA = 768
MLA_KV_LORA = 512
MLA_QK = MLA_NOPE + MLA_ROPE
MLA_QPAD = 256
ROPE_THETA = 10000.0
ATTN_STRIP = 32
LOG2E = 1.4426950408889634


def _const_spec(shape):
    zeros = (0,) * len(shape)
    return pl.BlockSpec(shape, lambda *_: zeros, pipeline_mode=pl.Buffered(1))


def _layer_spec(stacked_shape, layer):
    index = (layer,) + (0,) * (len(stacked_shape) - 1)
    return pl.BlockSpec((None,) + tuple(stacked_shape[1:]), lambda *_: index,
                        pipeline_mode=pl.Buffered(1))


def _params(n_axes, vmem_limit=VMEM_LIMIT_V7X):
    return pltpu.CompilerParams(dimension_semantics=("arbitrary",) * n_axes,
                                vmem_limit_bytes=vmem_limit)


def _rms(x, g):
    return x * lax.rsqrt(jnp.mean(x * x, axis=-1, keepdims=True) + EPS) * g


def _silu_of_twice(h):
    return h + h * jnp.tanh(h)


def _silu(x):
    return _silu_of_twice(0.5 * x)


def _mm(a, b):
    return jnp.dot(a.astype(BF16), b.astype(BF16), preferred_element_type=F32)


def _mm_nt(a, b):
    return lax.dot_general(a.astype(BF16), b.astype(BF16), (((1,), (1,)), ((), ())),
                           preferred_element_type=F32)


def _pool_body(x_ref, ng_ref, win_ref, wgrp_ref, sc_ref, wout_ref, fg_ref, o_ref, ubuf_ref,
               sa_ref, sb_ref, *, ts, nsub, width, final):
    j = pl.program_id(1)
    grp = width // len(POOL_WINDOWS)

    def window_sum(w, c0):
        levels = w.bit_length() - 1
        src, cols, shift = ubuf_ref, slice(c0, c0 + grp), 1
        for lv in range(levels):
            r0 = POOL_HALO - SUBLANES * (levels - 1 - lv)
            n = POOL_HALO + ts - r0
            val = src[r0:r0 + n, cols] + src[r0 - shift:r0 - shift + n, cols]
            if lv == levels - 1:
                return val
            dst = (sa_ref, sb_ref)[lv % 2]
            dst[r0:r0 + n, :] = val
            src, cols, shift = dst, slice(0, grp), 2 * shift

    @pl.when(j == 0)
    def _():
        ubuf_ref[0:POOL_HALO, :] = jnp.zeros((POOL_HALO, width), F32)

    for st in range(nsub):
        rows = slice(st * ts, (st + 1) * ts)
        x = x_ref[0, rows, :]
        h = _rms(x, ng_ref[...]).astype(BF16)
        u = jnp.dot(h, win_ref[:, :width], preferred_element_type=F32)
        gate = jnp.dot(h, win_ref[:, width:], preferred_element_type=F32)
        ubuf_ref[POOL_HALO:POOL_HALO + ts, :] = u
        t = (j * nsub + st) * ts + lax.broadcasted_iota(jnp.int32, (ts, 1), 0)
        parts = []
        for gi, w in enumerate(POOL_WINDOWS):
            c0 = gi * grp
            cnt = jnp.minimum(t + 1, w).astype(F32)
            p = window_sum(w, c0) / cnt - u[:, c0:c0 + grp]
            parts.append(jnp.dot(p.astype(BF16), wgrp_ref[gi], preferred_element_type=F32))
        pg = jnp.concatenate(parts, axis=1)
        y = pg * sc_ref[...] * _silu(gate)
        out = x + jnp.dot(y.astype(BF16), wout_ref[...], preferred_element_type=F32)
        ubuf_ref[0:POOL_HALO, :] = u[ts - POOL_HALO:, :]
        if final:
            out = _rms(out, fg_ref[...])
        o_ref[0, rows, :] = out


def _pool_layer(x, ng, w_in, w_grp, scale, w_out, layer, final_g, *, ts=512, nsub=2):
    B, S, D = x.shape
    width = w_out.shape[1]
    final = final_g is not None
    fg = final_g if final else jnp.ones((D,), F32)
    body = functools.partial(_pool_body, ts=ts, nsub=nsub, width=width, final=final)
    tb = ts * nsub
    return pl.pallas_call(
        body,
        grid=(B, S // tb),
        in_specs=[
            pl.BlockSpec((1, tb, D), lambda b, j: (b, j, 0)),
            _const_spec((1, D)),
            _layer_spec(w_in.shape, layer),
            _layer_spec(w_grp.shape, layer),
            _const_spec((1, width)),
            _layer_spec(w_out.shape, layer),
            _const_spec((1, D)),
        ],
        out_specs=pl.BlockSpec((1, tb, D), lambda b, j: (b, j, 0)),
        out_shape=jax.ShapeDtypeStruct((B, S, D), F32),
        scratch_shapes=[pltpu.VMEM((POOL_HALO + ts, width), F32),
                        pltpu.VMEM((POOL_HALO + ts, width // len(POOL_WINDOWS)), F32),
                        pltpu.VMEM((POOL_HALO + ts, width // len(POOL_WINDOWS)), F32)],
        compiler_params=_params(2),
        name="pool_layer",
    )(x, ng.reshape(1, D), w_in, w_grp, scale.reshape(1, width), w_out, fg.reshape(1, D))


def _gdn_in_body(x_ref, ng_ref, win_ref, wba_ref, conv_ref, alog_ref, dtb_ref,
                 q_ref, k_ref, v_ref, gate_ref, gb_ref, cbuf_ref, *, ts, nsub):
    j = pl.program_id(1)
    nch = 2 * GDN_QK + GDN_V

    @pl.when(j == 0)
    def _():
        cbuf_ref[0:CONV_HALO, :] = jnp.zeros((CONV_HALO, nch), F32)

    lane = lax.broadcasted_iota(jnp.int32, (1, LANES), 1)
    r = lax.broadcasted_iota(jnp.int32, (ts, 1), 0)
    c = lax.broadcasted_iota(jnp.int32, (1, ts), 1)
    tri = jnp.where(((r // GDN_CHUNK) == (c // GDN_CHUNK)) & (c <= r), 1.0, 0.0).astype(BF16)

    for st in range(nsub):
        rows = slice(st * ts, (st + 1) * ts)
        x = x_ref[0, rows, :]
        xn = _rms(x, ng_ref[...])
        h = xn.astype(BF16)
        h_half = (0.5 * xn).astype(BF16)

        for c0 in range(0, nch, MXU_COLS):
            cols = slice(c0, c0 + MXU_COLS)
            pre = jnp.dot(h_half, win_ref[:, cols], preferred_element_type=F32)
            cbuf_ref[CONV_HALO:CONV_HALO + ts, cols] = pre
            acc = pre * conv_ref[GDN_CONV - 1:GDN_CONV, cols]
            for kk in range(GDN_CONV - 1):
                r0 = CONV_HALO - (GDN_CONV - 1) + kk
                acc = acc + cbuf_ref[r0:r0 + ts, cols] * conv_ref[kk:kk + 1, cols]
            cbuf_ref[0:CONV_HALO, cols] = pre[ts - CONV_HALO:, :]
            a = _silu_of_twice(acc)
            if c0 < 2 * GDN_QK:
                is_q = c0 < GDN_QK
                out_ref, base = (q_ref, c0) if is_q else (k_ref, c0 - GDN_QK)
                for d0 in range(0, MXU_COLS, GDN_DK):
                    ah = a[:, d0:d0 + GDN_DK]
                    inv = lax.rsqrt(jnp.sum(ah * ah, axis=-1, keepdims=True) + EPS)
                    if is_q:
                        inv = inv * (GDN_DK ** -0.5)
                    out_ref[0, rows, base + d0:base + d0 + GDN_DK] = (ah * inv).astype(BF16)
            else:
                v0 = c0 - 2 * GDN_QK
                v_ref[0, rows, v0:v0 + MXU_COLS] = a.astype(BF16)
        for c0 in range(0, GDN_V, MXU_COLS):
            gate = jnp.dot(h_half, win_ref[:, nch + c0:nch + c0 + MXU_COLS],
                           preferred_element_type=F32)
            gate_ref[0, rows, c0:c0 + MXU_COLS] = _silu_of_twice(gate).astype(BF16)

        ba = jnp.dot(h, wba_ref[...], preferred_element_type=F32)
        z = ba + dtb_ref[...]
        softplus = jnp.maximum(z, 0.0) + jnp.log1p(jnp.exp(-jnp.abs(z)))
        g = jnp.where((lane >= GDN_HEADS) & (lane < 2 * GDN_HEADS),
                      -jnp.exp(alog_ref[...]) * softplus, 0.0)
        g_hi = g.astype(BF16)
        r1 = g - g_hi.astype(F32)
        g_mid = r1.astype(BF16)
        g_lo = (r1 - g_mid.astype(F32)).astype(BF16)
        gc = (jnp.dot(tri, g_hi, preferred_element_type=F32)
              + jnp.dot(tri, g_mid, preferred_element_type=F32)
              + jnp.dot(tri, g_lo, preferred_element_type=F32))
        gb_ref[0, rows, :] = jnp.where(lane < GDN_HEADS, jax.nn.sigmoid(ba), gc)


def _gdn_core_body(q_ref, k_ref, v_ref, gb_ref, gbt_ref, gate_ref, ng_ref, o_ref, s_ref):
    j = pl.program_id(1)

    @pl.when(j == 0)
    def _():
        s_ref[...] = jnp.zeros_like(s_ref)

    n = GDN_UNIT
    r = lax.broadcasted_iota(jnp.int32, (n, 1), 0)
    c = lax.broadcasted_iota(jnp.int32, (1, n), 1)
    same = (r // GDN_CHUNK) == (c // GDN_CHUNK)
    causal = same & (c <= r)
    strict = same & (c < r)
    eye = jnp.where(r == c, 1.0, 0.0)
    zeros_c = jnp.zeros((GDN_CHUNK, GDN_DV), F32)
    nb = q_ref.shape[0]
    hs = range(nb * GDN_HEADS)
    bi = [ci // GDN_HEADS for ci in hs]
    hi = [ci % GDN_HEADS for ci in hs]
    gb = [gb_ref[b] for b in range(nb)]
    gbt = [gbt_ref[b] for b in range(nb)]
    beta_c = [gb[bi[h]][:, hi[h]:hi[h] + 1] for h in hs]
    gc_c = [gb[bi[h]][:, GDN_HEADS + hi[h]:GDN_HEADS + hi[h] + 1] for h in hs]
    gc_r = [gbt[bi[h]][GDN_HEADS + hi[h]:GDN_HEADS + hi[h] + 1, :] for h in hs]
    q = [q_ref[bi[h], :, hi[h] * GDN_DK:(hi[h] + 1) * GDN_DK] for h in hs]
    k = [k_ref[bi[h], :, hi[h] * GDN_DK:(hi[h] + 1) * GDN_DK] for h in hs]
    kf = [k[h].astype(F32) for h in hs]
    kb = [kf[h] * beta_c[h] for h in hs]
    decay = [jnp.exp(jnp.where(causal, gc_c[h] - gc_r[h], NEG)) for h in hs]
    kkqk = [_mm_nt(jnp.concatenate([kb[h].astype(BF16), q[h]], axis=0), k[h]) for h in hs]
    lmat = [jnp.where(strict, kkqk[h][:n] * decay[h], 0.0) for h in hs]
    attn = [kkqk[h][n:] * decay[h] for h in hs]
    ch = GDN_CHUNK
    eye_s = eye[:ch] + eye[ch:]

    def block_diag(side):
        return jnp.where(same, jnp.concatenate([side, side], axis=0), 0.0)

    l_s = [lmat[h][:ch] + lmat[h][ch:] for h in hs]
    xs = [eye_s - l_s[h] for h in hs]
    ps = [_mm(l_s[h], lmat[h]) for h in hs]
    for _ in range(4):
        res = [_mm(jnp.concatenate([xs[h], ps[h]], axis=0), block_diag(ps[h])) for h in hs]
        xs = [xs[h] + res[h][:ch] for h in hs]
        ps = [res[h][ch:] for h in hs]
    res = [_mm(xs[h], block_diag(ps[h])) for h in hs]
    xm = [block_diag(xs[h] + res[h]) for h in hs]
    eg = [jnp.exp(gc_c[h]) for h in hs]
    rhs = [jnp.concatenate(
        [v_ref[bi[h], :, hi[h] * GDN_DV:(hi[h] + 1) * GDN_DV].astype(F32) * beta_c[h],
         kb[h] * eg[h]], axis=1) for h in hs]
    sol = [rhs[h] + _mm(xm[h] - eye, rhs[h]) for h in hs]
    qg = [q[h].astype(F32) * eg[h] for h in hs]
    k_t = [kf[h].T for h in hs]
    state = [s_ref[h] for h in hs]
    for cc in range(2):
        rows = slice(cc * GDN_CHUNK, (cc + 1) * GDN_CHUNK)
        in_chunk = (c // GDN_CHUNK) == cc
        g_last = [gc_r[h][:, (cc + 1) * GDN_CHUNK - 1:(cc + 1) * GDN_CHUNK] for h in hs]
        res = [_mm(jnp.concatenate([sol[h][rows, GDN_DV:], qg[h][rows]], axis=0), state[h])
               for h in hs]
        v_new = [sol[h][rows, :GDN_DV] - res[h][:GDN_CHUNK] for h in hs]
        v_all = [jnp.concatenate([v_new[h], zeros_c] if cc == 0 else [zeros_c, v_new[h]], axis=0)
                 for h in hs]
        k_dec_t = [k_t[h] * jnp.exp(jnp.where(in_chunk, g_last[h] - gc_r[h], NEG)) for h in hs]
        upd = [_mm(jnp.concatenate([attn[h][rows], k_dec_t[h]], axis=0), v_all[h]) for h in hs]
        o_c = [res[h][GDN_CHUNK:] + upd[h][:GDN_CHUNK] for h in hs]
        state = [state[h] * jnp.exp(g_last[h]) + upd[h][GDN_CHUNK:] for h in hs]
        for h in hs:
            cols = slice(hi[h] * GDN_DV, (hi[h] + 1) * GDN_DV)
            gated = _rms(o_c[h], ng_ref[...]) * gate_ref[bi[h], rows, cols].astype(F32)
            o_ref[bi[h], rows, cols] = gated.astype(o_ref.dtype)
    for h in hs:
        s_ref[h] = state[h]


def _proj_out_body(x_ref, o_ref, wout_ref, y_ref):
    y_ref[0] = x_ref[0] + jnp.dot(o_ref[0], wout_ref[...], preferred_element_type=F32)


def _proj_out(x, o, w_out, *, ts, name):
    B, S, D = x.shape
    tok = lambda w: pl.BlockSpec((1, ts, w), lambda b, j: (b, j, 0))
    return pl.pallas_call(
        _proj_out_body,
        grid=(B, S // ts),
        in_specs=[tok(D), tok(o.shape[-1]), _const_spec(w_out.shape)],
        out_specs=tok(D),
        out_shape=jax.ShapeDtypeStruct((B, S, D), F32),
        compiler_params=_params(2),
        name=name,
    )(x, o, w_out.astype(BF16))


def _gdn_layer(x, ng, w_in, conv_w, a_log, dt_bias, norm_g, w_out, *, ts=256, nsub=1, ts_out=1024):
    B, S, D = x.shape
    nch = 2 * GDN_QK + GDN_V
    w_all = w_in.astype(BF16)
    w_ba = jnp.pad(w_in[:, nch + GDN_V:], ((0, 0), (0, LANES - 2 * GDN_HEADS))).astype(BF16)
    pad_a = (GDN_HEADS, LANES - 2 * GDN_HEADS)
    alog = jnp.pad(a_log.astype(F32), pad_a).reshape(1, LANES)
    dtb = jnp.pad(dt_bias.astype(F32), pad_a).reshape(1, LANES)

    tb = ts * nsub
    tok = lambda w: pl.BlockSpec((1, tb, w), lambda b, j: (b, j, 0))
    q, k, v, gate, gb = pl.pallas_call(
        functools.partial(_gdn_in_body, ts=ts, nsub=nsub),
        grid=(B, S // tb),
        in_specs=[tok(D), _const_spec((1, D)), _const_spec(w_all.shape), _const_spec(w_ba.shape),
                  _const_spec(conv_w.shape), _const_spec((1, LANES)), _const_spec((1, LANES))],
        out_specs=[tok(GDN_QK), tok(GDN_QK), tok(GDN_V), tok(GDN_V), tok(LANES)],
        out_shape=[jax.ShapeDtypeStruct((B, S, GDN_QK), BF16),
                   jax.ShapeDtypeStruct((B, S, GDN_QK), BF16),
                   jax.ShapeDtypeStruct((B, S, GDN_V), BF16),
                   jax.ShapeDtypeStruct((B, S, GDN_V), BF16),
                   jax.ShapeDtypeStruct((B, S, LANES), F32)],
        scratch_shapes=[pltpu.VMEM((CONV_HALO + ts, nch), F32)],
        compiler_params=_params(2),
        name="gdn_in",
    )(x, ng.reshape(1, D), w_all, w_ba, conv_w, alog, dtb)

    gbt = jnp.transpose(gb[:, :, :2 * GDN_HEADS], (0, 2, 1))
    n = GDN_UNIT
    nb = 2 if B % 2 == 0 else 1
    utok = lambda w: pl.BlockSpec((nb, n, w), lambda b, j: (b, j, 0))
    o = pl.pallas_call(
        _gdn_core_body,
        grid=(B // nb, S // n),
        in_specs=[utok(GDN_QK), utok(GDN_QK), utok(GDN_V), utok(LANES),
                  pl.BlockSpec((nb, 2 * GDN_HEADS, n), lambda b, j: (b, 0, j)),
                  utok(GDN_V), _const_spec((1, GDN_DV))],
        out_specs=utok(GDN_V),
        out_shape=jax.ShapeDtypeStruct((B, S, GDN_V), BF16),
        scratch_shapes=[pltpu.VMEM((nb * GDN_HEADS, GDN_DK, GDN_DV), F32)],
        compiler_params=_params(2),
        name="gdn_core",
    )(q, k, v, gb, gbt, gate, norm_g.reshape(1, GDN_DV))

    return _proj_out(x, o, w_out, ts=ts_out, name="gdn_out")


def _mla_in_body(x_ref, pos_ref, ng_ref, win_ref, qg_ref, wuq_ref, kvg_ref, wukv_ref, inv_ref,
                 q_ref, k_ref, v_ref, gate_ref):
    c1 = MLA_Q_LORA
    c2 = c1 + MLA_KV_LORA
    c3 = c2 + LANES
    x = x_ref[0]
    h = _rms(x, ng_ref[...]).astype(BF16)
    cq = jnp.dot(h, win_ref[:, :c1], preferred_element_type=F32)
    ckv = jnp.dot(h, win_ref[:, c1:c2], preferred_element_type=F32)
    kr = jnp.dot(h, win_ref[:, c2:c3], preferred_element_type=F32)
    gate = jnp.dot(h, win_ref[:, c3:], preferred_element_type=F32)
    gate_ref[0] = _silu(gate).astype(BF16)

    half = MLA_ROPE // 2
    ang = pos_ref[0].astype(F32) * inv_ref[...]
    cos = jnp.cos(ang)
    sin = jnp.sin(ang)
    lane = lax.broadcasted_iota(jnp.int32, (1, LANES), 1)
    is_x1 = (lane // half) % 2 == 0
    sin_s = sin * jnp.where(is_x1, -1.0, 1.0)

    def rope(t):
        partner = jnp.where(is_x1, pltpu.roll(t, LANES - half, 1), pltpu.roll(t, half, 1))
        return t * cos + partner * sin_s

    k_rope_t = rope(kr).T.astype(BF16)
    qf = jnp.dot(_rms(cq, qg_ref[...]).astype(BF16), wuq_ref[...],
                 preferred_element_type=F32) * (MLA_QK ** -0.5 * LOG2E)
    kv = jnp.dot(_rms(ckv, kvg_ref[...]).astype(BF16), wukv_ref[...], preferred_element_type=F32)
    nope_w = MLA_HEADS * MLA_NOPE
    q_pad = jnp.zeros((qf.shape[0], MLA_QPAD - MLA_QK), F32)
    for hh in range(MLA_HEADS):
        if hh % 2 == 0:
            r0 = nope_w + (hh // 2) * LANES
            roped = rope(qf[:, r0:r0 + LANES])
        e = (hh % 2) * MLA_ROPE
        q_ref[0, hh] = jnp.concatenate(
            [qf[:, hh * MLA_NOPE:(hh + 1) * MLA_NOPE], roped[:, e:e + MLA_ROPE], q_pad],
            axis=1).astype(BF16)
        k0 = hh * (MLA_NOPE + MLA_V)
        k_ref[0, hh, 0:MLA_NOPE, :] = kv[:, k0:k0 + MLA_NOPE].T.astype(BF16)
        k_ref[0, hh, MLA_NOPE:, :] = k_rope_t
        v_ref[0, hh, :, :MLA_V] = kv[:, k0 + MLA_NOPE:k0 + MLA_NOPE + MLA_V].astype(BF16)
        v_ref[0, hh, :, MLA_V:] = jnp.ones((kv.shape[0], LANES), BF16)


def _attn_body(q_ref, k_ref, v_ref, gate_ref, o_ref, s_ref, p_ref, m_ref, a_ref, acc_ref, *, tq, tk):
    i = pl.program_id(2)
    nh = q_ref.shape[1]
    m_ref[...] = jnp.full(m_ref.shape, NEG, F32)
    acc_ref[...] = jnp.zeros(acc_ref.shape, F32)

    def step(c0, parts, masked):
        chains = [(h,) + part for part in parts for h in range(nh)]

        def scores(h, lo, hi, width):
            s_ref[h, lo:hi, :width] = jnp.dot(q_ref[0, h, lo:hi, :],
                                              k_ref[0, h, :, pl.ds(c0, width)],
                                              preferred_element_type=F32)

        def softmax(h, lo, hi, width):
            for r0 in range(lo, hi, ATTN_STRIP):
                rs = slice(r0, r0 + ATTN_STRIP)
                pieces = []
                for c in range(0, width, LANES):
                    if masked and c >= r0 + ATTN_STRIP:
                        pieces.append(None)
                        continue
                    sc = s_ref[h, rs, c:c + LANES]
                    if masked and c + LANES - 1 > r0:
                        sub = lax.broadcasted_iota(jnp.int32, (ATTN_STRIP, 1), 0)
                        ln = lax.broadcasted_iota(jnp.int32, (1, LANES), 1)
                        sc = jnp.where(c + ln <= r0 + sub, sc, NEG)
                    pieces.append(sc)
                visible = [sc for sc in pieces if sc is not None]
                mx = visible[0]
                for sc in visible[1:]:
                    mx = jnp.maximum(mx, sc)
                m_old = m_ref[h, rs, :]
                m_new = jnp.maximum(m_old, jnp.max(mx, axis=-1, keepdims=True))
                a_ref[h, rs, :] = jnp.exp2(m_old - m_new)
                m_ref[h, rs, :] = m_new
                zero = jnp.zeros((ATTN_STRIP, LANES), F32)
                p_ref[h, rs, :width] = jnp.concatenate(
                    [zero if sc is None else jnp.exp2(sc - m_new) for sc in pieces],
                    axis=1).astype(BF16)

        def values(h, lo, hi, width):
            pv = jnp.dot(p_ref[h, lo:hi, :width], v_ref[0, h, pl.ds(c0, width), :],
                         preferred_element_type=F32)
            a = a_ref[h, lo:hi, :]
            acc_ref[h, lo:hi, :MLA_V] = a * acc_ref[h, lo:hi, :MLA_V] + pv[:, :MLA_V]
            acc_ref[h, lo:hi, MLA_V:] = a * acc_ref[h, lo:hi, MLA_V:] + pv[:, MLA_V:]

        for stage in (scores, softmax, values):
            for chain in chains:
                stage(*chain)

    n_wide = (i * tq) // tk

    def wide_step(kb, carry):
        step(pl.multiple_of(kb * tk, tk), [(0, tq, tk)], False)
        return carry

    lax.fori_loop(0, n_wide, wide_step, 0)
    for d in range(tk // tq - 1):
        @pl.when(n_wide * tk + d * tq < i * tq)
        def _():
            step(pl.multiple_of(n_wide * tk + d * tq, tq), [(0, tq, tq)], False)
    step(pl.multiple_of(i * tq, tq), [(0, tq // 2, tq // 2), (tq // 2, tq, tq)], True)
    for h in range(nh):
        cols = slice(h * MLA_V, (h + 1) * MLA_V)
        o = acc_ref[h, :, :MLA_V] / acc_ref[h, :, MLA_V:]
        o_ref[0, :, cols] = (o * gate_ref[0, :, cols].astype(F32)).astype(o_ref.dtype)


def _mla_layer(x, pos, ng, w_in, q_norm_g, w_uq, kv_norm_g, w_ukv, w_out, *, ts=256, ts_out=1024, tq=512, tk=1024,
               nh=4):
    B, S, D = x.shape
    H = MLA_HEADS
    half = MLA_ROPE // 2
    c1 = MLA_Q_LORA
    c2 = c1 + MLA_KV_LORA
    c3 = c2 + MLA_ROPE
    w_in_p = jnp.concatenate(
        [w_in[:, :c3], jnp.zeros((D, LANES - MLA_ROPE), w_in.dtype), w_in[:, c3:]],
        axis=1).astype(BF16)
    wq = w_uq.reshape(MLA_Q_LORA, H, MLA_QK)
    wq_p = jnp.concatenate([wq[:, :, :MLA_NOPE].reshape(MLA_Q_LORA, H * MLA_NOPE),
                            wq[:, :, MLA_NOPE:].reshape(MLA_Q_LORA, H * MLA_ROPE)],
                           axis=1).astype(BF16)
    inv = ROPE_THETA ** (-jnp.arange(half, dtype=F32) / half)
    inv = jnp.tile(inv, LANES // half).reshape(1, LANES)
    width = H * MLA_V

    tok = lambda w: pl.BlockSpec((1, ts, w), lambda b, j: (b, j, 0))
    head = lambda w: pl.BlockSpec((1, H, ts, w), lambda b, j: (b, 0, j, 0))
    q, k, v, gate = pl.pallas_call(
        _mla_in_body,
        grid=(B, S // ts),
        in_specs=[tok(D), tok(1), _const_spec((1, D)), _const_spec(w_in_p.shape),
                  _const_spec((1, MLA_Q_LORA)), _const_spec(wq_p.shape),
                  _const_spec((1, MLA_KV_LORA)), _const_spec(w_ukv.shape), _const_spec((1, LANES))],
        out_specs=[head(MLA_QPAD), pl.BlockSpec((1, H, MLA_QPAD, ts), lambda b, j: (b, 0, 0, j)),
                   head(MLA_V + LANES), tok(width)],
        out_shape=[jax.ShapeDtypeStruct((B, H, S, MLA_QPAD), BF16),
                   jax.ShapeDtypeStruct((B, H, MLA_QPAD, S), BF16),
                   jax.ShapeDtypeStruct((B, H, S, MLA_V + LANES), BF16),
                   jax.ShapeDtypeStruct((B, S, width), BF16)],
        compiler_params=_params(2),
        name="mla_in",
    )(x, pos.reshape(B, S, 1), ng.reshape(1, D), w_in_p, q_norm_g.reshape(1, MLA_Q_LORA), wq_p,
      kv_norm_g.reshape(1, MLA_KV_LORA), w_ukv.astype(BF16), inv)

    o = pl.pallas_call(
        functools.partial(_attn_body, tq=tq, tk=tk),
        grid=(B, H // nh, S // tq),
        in_specs=[pl.BlockSpec((1, nh, tq, MLA_QPAD), lambda b, h, i: (b, h, i, 0)),
                  pl.BlockSpec((1, nh, MLA_QPAD, S), lambda b, h, i: (b, h, 0, 0)),
                  pl.BlockSpec((1, nh, S, MLA_V + LANES), lambda b, h, i: (b, h, 0, 0)),
                  pl.BlockSpec((1, tq, nh * MLA_V), lambda b, h, i: (b, i, h))],
        out_specs=pl.BlockSpec((1, tq, nh * MLA_V), lambda b, h, i: (b, i, h)),
        out_shape=jax.ShapeDtypeStruct((B, S, width), BF16),
        scratch_shapes=[pltpu.VMEM((nh, tq, tk), F32), pltpu.VMEM((nh, tq, tk), BF16),
                        pltpu.VMEM((nh, tq, LANES), F32), pltpu.VMEM((nh, tq, LANES), F32),
                        pltpu.VMEM((nh, tq, MLA_V + LANES), F32)],
        compiler_params=_params(3, ATTN_VMEM_LIMIT_V7X),
        name="mla_attn",
    )(q, k, v, gate)

    return _proj_out(x, o, w_out, ts=ts_out, name="mla_out")


def kernel(x, positions, norm_g, pool_w_in, pool_w_grp, pool_scale, pool_w_out, gdn_w_in, gdn_conv, gdn_a_log, gdn_dt_bias, gdn_norm_g, gdn_w_out, mla_w_in, mla_q_norm_g, mla_w_uq, mla_kv_norm_g, mla_w_ukv, mla_w_out, final_g):
    depth = norm_g.shape[0]
    pool_w = (pool_w_in.astype(BF16), pool_w_grp.astype(BF16), pool_w_out.astype(BF16))
    for i in range(depth):
        kind, j = i % N_MIXERS, i // N_MIXERS
        last = i == depth - 1
        if kind == 0:
            x = _pool_layer(x, norm_g[i], pool_w[0], pool_w[1], pool_scale[j], pool_w[2], j,
                            final_g if last else None)
        elif kind == 1:
            x = _gdn_layer(x, norm_g[i], gdn_w_in[j], gdn_conv[j], gdn_a_log[j], gdn_dt_bias[j],
                           gdn_norm_g[j], gdn_w_out[j])
        else:
            x = _mla_layer(x, positions, norm_g[i], mla_w_in[j], mla_q_norm_g[j], mla_w_uq[j],
                           mla_kv_norm_g[j], mla_w_ukv[j], mla_w_out[j])
        if last and kind != 0:
            x = _final_norm(x, final_g)
    return x


def _final_norm_body(x_ref, g_ref, o_ref):
    o_ref[0] = _rms(x_ref[0], g_ref[...])


def _final_norm(x, g, *, ts=512):
    B, S, D = x.shape
    tok = pl.BlockSpec((1, ts, D), lambda b, j: (b, j, 0))
    return pl.pallas_call(
        _final_norm_body, grid=(B, S // ts), in_specs=[tok, _const_spec((1, D))], out_specs=tok,
        out_shape=jax.ShapeDtypeStruct((B, S, D), F32), compiler_params=_params(2),
        name="final_norm",
    )(x, g.reshape(1, D))
```

```python
import functools

import jax
import jax.numpy as jnp
from jax import lax
from jax.experimental import pallas as pl
from jax.experimental.pallas import tpu as pltpu

F32 = jnp.float32
BF16 = jnp.bfloat16
EPS = 1e-6
NEG = -1e30
VMEM_LIMIT_V7X = 56 * 2 ** 20
ATTN_VMEM_LIMIT_V7X = 60 * 2 ** 20
LANES = 128
SUBLANES = 8
MXU_COLS = 256

N_MIXERS = 3

POOL_WINDOWS = (2, 4, 8, 16)
POOL_HALO = SUBLANES * (max(POOL_WINDOWS).bit_length() - 1)

GDN_HEADS = 8
GDN_DK = 128
GDN_DV = 256
GDN_CONV = 4
GDN_CHUNK = 64
GDN_QK = GDN_HEADS * GDN_DK
GDN_V = GDN_HEADS * GDN_DV
GDN_UNIT = 2 * GDN_CHUNK
CONV_HALO = 8

MLA_HEADS = 16
MLA_NOPE = 128
MLA_ROPE = 64
MLA_V = 128
MLA_Q_LORA = 768
MLA_KV_LORA = 512
MLA_QK = MLA_NOPE + MLA_ROPE
MLA_QPAD = 256
ROPE_THETA = 10000.0
ATTN_STRIP = 32
LOG2E = 1.4426950408889634


def _const_spec(shape):
    zeros = (0,) * len(shape)
    return pl.BlockSpec(shape, lambda *_: zeros, pipeline_mode=pl.Buffered(1))


def _layer_spec(stacked_shape, layer):
    index = (layer,) + (0,) * (len(stacked_shape) - 1)
    return pl.BlockSpec((None,) + tuple(stacked_shape[1:]), lambda *_: index,
                        pipeline_mode=pl.Buffered(1))


def _stream_spec(rows, width):
    return pl.BlockSpec((1, rows, width), lambda b, j: (b, j, 0))


def _params(n_axes, vmem_limit=VMEM_LIMIT_V7X):
    return pltpu.CompilerParams(dimension_semantics=("arbitrary",) * n_axes,
                                vmem_limit_bytes=vmem_limit)


def _rms(x, g):
    return x * lax.rsqrt(jnp.mean(x * x, axis=-1, keepdims=True) + EPS) * g


def _silu_of_twice(h):
    return h + h * jnp.tanh(h)


def _silu(x):
    return _silu_of_twice(0.5 * x)


def _mm(a, b):
    return jnp.dot(a.astype(BF16), b.astype(BF16), preferred_element_type=F32)


def _mm_nt(a, b):
    return lax.dot_general(a.astype(BF16), b.astype(BF16), (((1,), (1,)), ((), ())),
                           preferred_element_type=F32)


def _pool_body(x_ref, ng_ref, win_ref, wgrp_ref, sc_ref, wout_ref, fg_ref, o_ref, ubuf_ref,
               sa_ref, sb_ref, *, ts, nsub, width, final):
    j = pl.program_id(1)
    grp = width // len(POOL_WINDOWS)

    def window_sum(w, c0):
        levels = w.bit_length() - 1
        src, cols, shift = ubuf_ref, slice(c0, c0 + grp), 1
        for lv in range(levels):
            r0 = POOL_HALO - SUBLANES * (levels - 1 - lv)
            n = POOL_HALO + ts - r0
            val = src[r0:r0 + n, cols] + src[r0 - shift:r0 - shift + n, cols]
            if lv == levels - 1:
                return val
            dst = (sa_ref, sb_ref)[lv % 2]
            dst[r0:r0 + n, :] = val
            src, cols, shift = dst, slice(0, grp), 2 * shift

    @pl.when(j == 0)
    def _():
        ubuf_ref[0:POOL_HALO, :] = jnp.zeros((POOL_HALO, width), F32)

    for st in range(nsub):
        rows = slice(st * ts, (st + 1) * ts)
        x = x_ref[0, rows, :]
        h = _rms(x, ng_ref[...]).astype(BF16)
        u = jnp.dot(h, win_ref[:, :width], preferred_element_type=F32)
        gate = jnp.dot(h, win_ref[:, width:], preferred_element_type=F32)
        ubuf_ref[POOL_HALO:POOL_HALO + ts, :] = u
        t = (j * nsub + st) * ts + lax.broadcasted_iota(jnp.int32, (ts, 1), 0)
        parts = []
        for gi, w in enumerate(POOL_WINDOWS):
            c0 = gi * grp
            cnt = jnp.minimum(t + 1, w).astype(F32)
            p = window_sum(w, c0) / cnt - u[:, c0:c0 + grp]
            parts.append(jnp.dot(p.astype(BF16), wgrp_ref[gi], preferred_element_type=F32))
        pg = jnp.concatenate(parts, axis=1)
        y = pg * sc_ref[...] * _silu(gate)
        out = x + jnp.dot(y.astype(BF16), wout_ref[...], preferred_element_type=F32)
        ubuf_ref[0:POOL_HALO, :] = u[ts - POOL_HALO:, :]
        if final:
            out = _rms(out, fg_ref[...])
        o_ref[0, rows, :] = out


def _pool_layer(x, ng, w_in, w_grp, scale, w_out, layer, final_g, *, ts=512, nsub=2):
    B, S, D = x.shape
    width = w_out.shape[1]
    final = final_g is not None
    fg = final_g if final else jnp.ones((D,), F32)
    body = functools.partial(_pool_body, ts=ts, nsub=nsub, width=width, final=final)
    tb = ts * nsub
    return pl.pallas_call(
        body,
        grid=(B, S // tb),
        in_specs=[
            _stream_spec(tb, D),
            _const_spec((1, D)),
            _layer_spec(w_in.shape, layer),
            _layer_spec(w_grp.shape, layer),
            _const_spec((1, width)),
            _layer_spec(w_out.shape, layer),
            _const_spec((1, D)),
        ],
        out_specs=pl.BlockSpec((1, tb, D), lambda b, j: (b, j, 0)),
        out_shape=jax.ShapeDtypeStruct((B, S, D), F32),
        scratch_shapes=[pltpu.VMEM((POOL_HALO + ts, width), F32),
                        pltpu.VMEM((POOL_HALO + ts, width // len(POOL_WINDOWS)), F32),
                        pltpu.VMEM((POOL_HALO + ts, width // len(POOL_WINDOWS)), F32)],
        compiler_params=_params(2),
        name="pool_layer",
    )(x, ng.reshape(1, D), w_in, w_grp, scale.reshape(1, width), w_out, fg.reshape(1, D))


def _gdn_in_body(x_ref, ng_ref, win_ref, wba_ref, conv_ref, alog_ref, dtb_ref,
                 q_ref, k_ref, v_ref, gate_ref, gb_ref, cbuf_ref, *, ts, nsub):
    j = pl.program_id(1)
    nch = 2 * GDN_QK + GDN_V

    @pl.when(j == 0)
    def _():
        cbuf_ref[0:CONV_HALO, :] = jnp.zeros((CONV_HALO, nch), F32)

    lane = lax.broadcasted_iota(jnp.int32, (1, LANES), 1)
    r = lax.broadcasted_iota(jnp.int32, (ts, 1), 0)
    c = lax.broadcasted_iota(jnp.int32, (1, ts), 1)
    tri = jnp.where(((r // GDN_CHUNK) == (c // GDN_CHUNK)) & (c <= r), 1.0, 0.0).astype(BF16)

    for st in range(nsub):
        rows = slice(st * ts, (st + 1) * ts)
        x = x_ref[0, rows, :]
        xn = _rms(x, ng_ref[...])
        h = xn.astype(BF16)
        h_half = (0.5 * xn).astype(BF16)

        for c0 in range(0, nch, MXU_COLS):
            cols = slice(c0, c0 + MXU_COLS)
            pre = jnp.dot(h_half, win_ref[:, cols], preferred_element_type=F32)
            cbuf_ref[CONV_HALO:CONV_HALO + ts, cols] = pre
            acc = pre * conv_ref[GDN_CONV - 1:GDN_CONV, cols]
            for kk in range(GDN_CONV - 1):
                r0 = CONV_HALO - (GDN_CONV - 1) + kk
                acc = acc + cbuf_ref[r0:r0 + ts, cols] * conv_ref[kk:kk + 1, cols]
            cbuf_ref[0:CONV_HALO, cols] = pre[ts - CONV_HALO:, :]
            a = _silu_of_twice(acc)
            if c0 < 2 * GDN_QK:
                is_q = c0 < GDN_QK
                out_ref, base = (q_ref, c0) if is_q else (k_ref, c0 - GDN_QK)
                for d0 in range(0, MXU_COLS, GDN_DK):
                    ah = a[:, d0:d0 + GDN_DK]
                    inv = lax.rsqrt(jnp.sum(ah * ah, axis=-1, keepdims=True) + EPS)
                    if is_q:
                        inv = inv * (GDN_DK ** -0.5)
                    out_ref[0, rows, base + d0:base + d0 + GDN_DK] = (ah * inv).astype(BF16)
            else:
                v0 = c0 - 2 * GDN_QK
                v_ref[0, rows, v0:v0 + MXU_COLS] = a.astype(BF16)
        for c0 in range(0, GDN_V, MXU_COLS):
            gate = jnp.dot(h_half, win_ref[:, nch + c0:nch + c0 + MXU_COLS],
                           preferred_element_type=F32)
            gate_ref[0, rows, c0:c0 + MXU_COLS] = _silu_of_twice(gate).astype(BF16)

        ba = jnp.dot(h, wba_ref[...], preferred_element_type=F32)
        z = ba + dtb_ref[...]
        softplus = jnp.maximum(z, 0.0) + jnp.log1p(jnp.exp(-jnp.abs(z)))
        g = jnp.where((lane >= GDN_HEADS) & (lane < 2 * GDN_HEADS),
                      -jnp.exp(alog_ref[...]) * softplus, 0.0)
        g_hi = g.astype(BF16)
        r1 = g - g_hi.astype(F32)
        g_mid = r1.astype(BF16)
        g_lo = (r1 - g_mid.astype(F32)).astype(BF16)
        gc = (jnp.dot(tri, g_hi, preferred_element_type=F32)
              + jnp.dot(tri, g_mid, preferred_element_type=F32)
              + jnp.dot(tri, g_lo, preferred_element_type=F32))
        gb_ref[0, rows, :] = jnp.where(lane < GDN_HEADS, jax.nn.sigmoid(ba), gc)


def _gdn_core_body(q_ref, k_ref, v_ref, gb_ref, gbt_ref, gate_ref, ng_ref, o_ref, s_ref):
    j = pl.program_id(1)

    @pl.when(j == 0)
    def _():
        s_ref[...] = jnp.zeros_like(s_ref)

    n = GDN_UNIT
    r = lax.broadcasted_iota(jnp.int32, (n, 1), 0)
    c = lax.broadcasted_iota(jnp.int32, (1, n), 1)
    same = (r // GDN_CHUNK) == (c // GDN_CHUNK)
    causal = same & (c <= r)
    strict = same & (c < r)
    eye = jnp.where(r == c, 1.0, 0.0)
    zeros_c = jnp.zeros((GDN_CHUNK, GDN_DV), F32)
    nb = q_ref.shape[0]
    hs = range(nb * GDN_HEADS)
    bi = [ci // GDN_HEADS for ci in hs]
    hi = [ci % GDN_HEADS for ci in hs]
    gb = [gb_ref[b] for b in range(nb)]
    gbt = [gbt_ref[b] for b in range(nb)]
    beta_c = [gb[bi[h]][:, hi[h]:hi[h] + 1] for h in hs]
    gc_c = [gb[bi[h]][:, GDN_HEADS + hi[h]:GDN_HEADS + hi[h] + 1] for h in hs]
    gc_r = [gbt[bi[h]][GDN_HEADS + hi[h]:GDN_HEADS + hi[h] + 1, :] for h in hs]
    q = [q_ref[bi[h], :, hi[h] * GDN_DK:(hi[h] + 1) * GDN_DK] for h in hs]
    k = [k_ref[bi[h], :, hi[h] * GDN_DK:(hi[h] + 1) * GDN_DK] for h in hs]
    kf = [k[h].astype(F32) for h in hs]
    kb = [kf[h] * beta_c[h] for h in hs]
    decay = [jnp.exp(jnp.where(causal, gc_c[h] - gc_r[h], NEG)) for h in hs]
    kkqk = [_mm_nt(jnp.concatenate([kb[h].astype(BF16), q[h]], axis=0), k[h]) for h in hs]
    lmat = [jnp.where(strict, kkqk[h][:n] * decay[h], 0.0) for h in hs]
    attn = [kkqk[h][n:] * decay[h] for h in hs]
    ch = GDN_CHUNK
    eye_s = eye[:ch] + eye[ch:]

    def block_diag(side):
        return jnp.where(same, jnp.concatenate([side, side], axis=0), 0.0)

    l_s = [lmat[h][:ch] + lmat[h][ch:] for h in hs]
    xs = [eye_s - l_s[h] for h in hs]
    ps = [_mm(l_s[h], lmat[h]) for h in hs]
    for _ in range(4):
        res = [_mm(jnp.concatenate([xs[h], ps[h]], axis=0), block_diag(ps[h])) for h in hs]
        xs = [xs[h] + res[h][:ch] for h in hs]
        ps = [res[h][ch:] for h in hs]
    res = [_mm(xs[h], block_diag(ps[h])) for h in hs]
    xm = [block_diag(xs[h] + res[h]) for h in hs]
    eg = [jnp.exp(gc_c[h]) for h in hs]
    rhs = [jnp.concatenate(
        [v_ref[bi[h], :, hi[h] * GDN_DV:(hi[h] + 1) * GDN_DV].astype(F32) * beta_c[h],
         kb[h] * eg[h]], axis=1) for h in hs]
    sol = [rhs[h] + _mm(xm[h] - eye, rhs[h]) for h in hs]
    qg = [q[h].astype(F32) * eg[h] for h in hs]
    k_t = [kf[h].T for h in hs]
    state = [s_ref[h] for h in hs]
    for cc in range(2):
        rows = slice(cc * GDN_CHUNK, (cc + 1) * GDN_CHUNK)
        in_chunk = (c // GDN_CHUNK) == cc
        g_last = [gc_r[h][:, (cc + 1) * GDN_CHUNK - 1:(cc + 1) * GDN_CHUNK] for h in hs]
        res = [_mm(jnp.concatenate([sol[h][rows, GDN_DV:], qg[h][rows]], axis=0), state[h])
               for h in hs]
        v_new = [sol[h][rows, :GDN_DV] - res[h][:GDN_CHUNK] for h in hs]
        v_all = [jnp.concatenate([v_new[h], zeros_c] if cc == 0 else [zeros_c, v_new[h]], axis=0)
                 for h in hs]
        k_dec_t = [k_t[h] * jnp.exp(jnp.where(in_chunk, g_last[h] - gc_r[h], NEG)) for h in hs]
        upd = [_mm(jnp.concatenate([attn[h][rows], k_dec_t[h]], axis=0), v_all[h]) for h in hs]
        o_c = [res[h][GDN_CHUNK:] + upd[h][:GDN_CHUNK] for h in hs]
        state = [state[h] * jnp.exp(g_last[h]) + upd[h][GDN_CHUNK:] for h in hs]
        for h in hs:
            cols = slice(hi[h] * GDN_DV, (hi[h] + 1) * GDN_DV)
            gated = _rms(o_c[h], ng_ref[...]) * gate_ref[bi[h], rows, cols].astype(F32)
            o_ref[bi[h], rows, cols] = gated.astype(o_ref.dtype)
    for h in hs:
        s_ref[h] = state[h]


def _proj_out_body(x_ref, o_ref, wout_ref, y_ref):
    y_ref[0] = x_ref[0] + jnp.dot(o_ref[0], wout_ref[...], preferred_element_type=F32)


def _proj_out(x, o, w_out, *, ts, name):
    B, S, D = x.shape
    tok = lambda w: pl.BlockSpec((1, ts, w), lambda b, j: (b, j, 0))
    return pl.pallas_call(
        _proj_out_body,
        grid=(B, S // ts),
        in_specs=[_stream_spec(ts, D), _stream_spec(ts, o.shape[-1]), _const_spec(w_out.shape)],
        out_specs=tok(D),
        out_shape=jax.ShapeDtypeStruct((B, S, D), F32),
        compiler_params=_params(2),
        name=name,
    )(x, o, w_out.astype(BF16))


def _gdn_layer(x, ng, w_in, conv_w, a_log, dt_bias, norm_g, w_out, *, ts=256, nsub=1, ts_out=1024):
    B, S, D = x.shape
    nch = 2 * GDN_QK + GDN_V
    w_all = w_in.astype(BF16)
    w_ba = jnp.pad(w_in[:, nch + GDN_V:], ((0, 0), (0, LANES - 2 * GDN_HEADS))).astype(BF16)
    pad_a = (GDN_HEADS, LANES - 2 * GDN_HEADS)
    alog = jnp.pad(a_log.astype(F32), pad_a).reshape(1, LANES)
    dtb = jnp.pad(dt_bias.astype(F32), pad_a).reshape(1, LANES)

    tb = ts * nsub
    tok = lambda w: pl.BlockSpec((1, tb, w), lambda b, j: (b, j, 0))
    q, k, v, gate, gb = pl.pallas_call(
        functools.partial(_gdn_in_body, ts=ts, nsub=nsub),
        grid=(B, S // tb),
        in_specs=[_stream_spec(tb, D), _const_spec((1, D)), _const_spec(w_all.shape),
                  _const_spec(w_ba.shape),
                  _const_spec(conv_w.shape), _const_spec((1, LANES)), _const_spec((1, LANES))],
        out_specs=[tok(GDN_QK), tok(GDN_QK), tok(GDN_V), tok(GDN_V), tok(LANES)],
        out_shape=[jax.ShapeDtypeStruct((B, S, GDN_QK), BF16),
                   jax.ShapeDtypeStruct((B, S, GDN_QK), BF16),
                   jax.ShapeDtypeStruct((B, S, GDN_V), BF16),
                   jax.ShapeDtypeStruct((B, S, GDN_V), BF16),
                   jax.ShapeDtypeStruct((B, S, LANES), F32)],
        scratch_shapes=[pltpu.VMEM((CONV_HALO + ts, nch), F32)],
        compiler_params=_params(2),
        name="gdn_in",
    )(x, ng.reshape(1, D), w_all, w_ba, conv_w, alog, dtb)

    gbt = jnp.transpose(gb[:, :, :2 * GDN_HEADS], (0, 2, 1))
    n = GDN_UNIT
    nb = 2 if B % 2 == 0 else 1
    utok = lambda w: pl.BlockSpec((nb, n, w), lambda b, j: (b, j, 0))
    o = pl.pallas_call(
        _gdn_core_body,
        grid=(B // nb, S // n),
        in_specs=[utok(GDN_QK), utok(GDN_QK), utok(GDN_V), utok(LANES),
                  pl.BlockSpec((nb, 2 * GDN_HEADS, n), lambda b, j: (b, 0, j)),
                  utok(GDN_V), _const_spec((1, GDN_DV))],
        out_specs=utok(GDN_V),
        out_shape=jax.ShapeDtypeStruct((B, S, GDN_V), BF16),
        scratch_shapes=[pltpu.VMEM((nb * GDN_HEADS, GDN_DK, GDN_DV), F32)],
        compiler_params=_params(2),
        name="gdn_core",
    )(q, k, v, gb, gbt, gate, norm_g.reshape(1, GDN_DV))

    return _proj_out(x, o, w_out, ts=ts_out, name="gdn_out")


def _mla_in_body(x_ref, pos_ref, ng_ref, win_ref, qg_ref, wuq_ref, kvg_ref, wukv_ref, inv_ref,
                 q_ref, k_ref, v_ref, gate_ref):
    c1 = MLA_Q_LORA
    c2 = c1 + MLA_KV_LORA
    c3 = c2 + LANES
    x = x_ref[0]
    h = _rms(x, ng_ref[...]).astype(BF16)
    cq = jnp.dot(h, win_ref[:, :c1], preferred_element_type=F32)
    ckv = jnp.dot(h, win_ref[:, c1:c2], preferred_element_type=F32)
    kr = jnp.dot(h, win_ref[:, c2:c3], preferred_element_type=F32)
    gate = jnp.dot(h, win_ref[:, c3:], preferred_element_type=F32)
    gate_ref[0] = _silu(gate).astype(BF16)

    half = MLA_ROPE // 2
    ang = pos_ref[0].astype(F32) * inv_ref[...]
    cos = jnp.cos(ang)
    sin = jnp.sin(ang)
    lane = lax.broadcasted_iota(jnp.int32, (1, LANES), 1)
    is_x1 = (lane // half) % 2 == 0
    sin_s = sin * jnp.where(is_x1, -1.0, 1.0)

    def rope(t):
        partner = jnp.where(is_x1, pltpu.roll(t, LANES - half, 1), pltpu.roll(t, half, 1))
        return t * cos + partner * sin_s

    k_rope_t = rope(kr).T.astype(BF16)
    qf = jnp.dot(_rms(cq, qg_ref[...]).astype(BF16), wuq_ref[...],
                 preferred_element_type=F32) * (MLA_QK ** -0.5 * LOG2E)
    kv = jnp.dot(_rms(ckv, kvg_ref[...]).astype(BF16), wukv_ref[...], preferred_element_type=F32)
    nope_w = MLA_HEADS * MLA_NOPE
    q_pad = jnp.zeros((qf.shape[0], MLA_QPAD - MLA_QK), F32)
    for hh in range(MLA_HEADS):
        if hh % 2 == 0:
            r0 = nope_w + (hh // 2) * LANES
            roped = rope(qf[:, r0:r0 + LANES])
        e = (hh % 2) * MLA_ROPE
        q_ref[0, hh] = jnp.concatenate(
            [qf[:, hh * MLA_NOPE:(hh + 1) * MLA_NOPE], roped[:, e:e + MLA_ROPE], q_pad],
            axis=1).astype(BF16)
        k0 = hh * (MLA_NOPE + MLA_V)
        k_ref[0, hh, 0:MLA_NOPE, :] = kv[:, k0:k0 + MLA_NOPE].T.astype(BF16)
        k_ref[0, hh, MLA_NOPE:, :] = k_rope_t
        v_ref[0, hh, :, :MLA_V] = kv[:, k0 + MLA_NOPE:k0 + MLA_NOPE + MLA_V].astype(BF16)
        v_ref[0, hh, :, MLA_V:] = jnp.ones((kv.shape[0], LANES), BF16)


def _attn_body(q_ref, k_ref, v_ref, gate_ref, o_ref, s_ref, p_ref, m_ref, a_ref, acc_ref, *, tq, tk):
    i = pl.program_id(2)
    nh = q_ref.shape[1]
    m_ref[...] = jnp.full(m_ref.shape, NEG, F32)
    acc_ref[...] = jnp.zeros(acc_ref.shape, F32)

    def step(c0, parts, masked):
        chains = [(h,) + part for part in parts for h in range(nh)]

        def scores(h, lo, hi, width):
            s_ref[h, lo:hi, :width] = jnp.dot(q_ref[0, h, lo:hi, :],
                                              k_ref[0, h, :, pl.ds(c0, width)],
                                              preferred_element_type=F32)

        def softmax(h, lo, hi, width):
            for r0 in range(lo, hi, ATTN_STRIP):
                rs = slice(r0, r0 + ATTN_STRIP)
                pieces = []
                for c in range(0, width, LANES):
                    if masked and c >= r0 + ATTN_STRIP:
                        pieces.append(None)
                        continue
                    sc = s_ref[h, rs, c:c + LANES]
                    if masked and c + LANES - 1 > r0:
                        sub = lax.broadcasted_iota(jnp.int32, (ATTN_STRIP, 1), 0)
                        ln = lax.broadcasted_iota(jnp.int32, (1, LANES), 1)
                        sc = jnp.where(c + ln <= r0 + sub, sc, NEG)
                    pieces.append(sc)
                visible = [sc for sc in pieces if sc is not None]
                mx = visible[0]
                for sc in visible[1:]:
                    mx = jnp.maximum(mx, sc)
                m_old = m_ref[h, rs, :]
                m_new = jnp.maximum(m_old, jnp.max(mx, axis=-1, keepdims=True))
                a_ref[h, rs, :] = jnp.exp2(m_old - m_new)
                m_ref[h, rs, :] = m_new
                zero = jnp.zeros((ATTN_STRIP, LANES), F32)
                p_ref[h, rs, :width] = jnp.concatenate(
                    [zero if sc is None else jnp.exp2(sc - m_new) for sc in pieces],
                    axis=1).astype(BF16)

        def values(h, lo, hi, width):
            pv = jnp.dot(p_ref[h, lo:hi, :width], v_ref[0, h, pl.ds(c0, width), :],
                         preferred_element_type=F32)
            a = a_ref[h, lo:hi, :]
            acc_ref[h, lo:hi, :MLA_V] = a * acc_ref[h, lo:hi, :MLA_V] + pv[:, :MLA_V]
            acc_ref[h, lo:hi, MLA_V:] = a * acc_ref[h, lo:hi, MLA_V:] + pv[:, MLA_V:]

        for stage in (scores, softmax, values):
            for chain in chains:
                stage(*chain)

    n_wide = (i * tq) // tk

    def wide_step(kb, carry):
        step(pl.multiple_of(kb * tk, tk), [(0, tq // 2, tk), (tq // 2, tq, tk)], False)
        return carry

    lax.fori_loop(0, n_wide, wide_step, 0)
    for d in range(tk // tq - 1):
        @pl.when(n_wide * tk + d * tq < i * tq)
        def _():
            step(pl.multiple_of(n_wide * tk + d * tq, tq), [(0, tq, tq)], False)
    step(pl.multiple_of(i * tq, tq), [(0, tq // 2, tq // 2), (tq // 2, tq, tq)], True)
    for h in range(nh):
        cols = slice(h * MLA_V, (h + 1) * MLA_V)
        o = acc_ref[h, :, :MLA_V] / acc_ref[h, :, MLA_V:]
        o_ref[0, :, cols] = (o * gate_ref[0, :, cols].astype(F32)).astype(o_ref.dtype)


def _mla_layer(x, pos, ng, w_in, q_norm_g, w_uq, kv_norm_g, w_ukv, w_out, *, ts=256, ts_out=1024, tq=512, tk=1024,
               nh=4):
    B, S, D = x.shape
    H = MLA_HEADS
    half = MLA_ROPE // 2
    c1 = MLA_Q_LORA
    c2 = c1 + MLA_KV_LORA
    c3 = c2 + MLA_ROPE
    w_in_p = jnp.concatenate(
        [w_in[:, :c3], jnp.zeros((D, LANES - MLA_ROPE), w_in.dtype), w_in[:, c3:]],
        axis=1).astype(BF16)
    wq = w_uq.reshape(MLA_Q_LORA, H, MLA_QK)
    wq_p = jnp.concatenate([wq[:, :, :MLA_NOPE].reshape(MLA_Q_LORA, H * MLA_NOPE),
                            wq[:, :, MLA_NOPE:].reshape(MLA_Q_LORA, H * MLA_ROPE)],
                           axis=1).astype(BF16)
    inv = ROPE_THETA ** (-jnp.arange(half, dtype=F32) / half)
    inv = jnp.tile(inv, LANES // half).reshape(1, LANES)
    width = H * MLA_V

    tok = lambda w: pl.BlockSpec((1, ts, w), lambda b, j: (b, j, 0))
    head = lambda w: pl.BlockSpec((1, H, ts, w), lambda b, j: (b, 0, j, 0))
    q, k, v, gate = pl.pallas_call(
        _mla_in_body,
        grid=(B, S // ts),
        in_specs=[_stream_spec(ts, D), tok(1), _const_spec((1, D)), _const_spec(w_in_p.shape),
                  _const_spec((1, MLA_Q_LORA)), _const_spec(wq_p.shape),
                  _const_spec((1, MLA_KV_LORA)), _const_spec(w_ukv.shape), _const_spec((1, LANES))],
        out_specs=[head(MLA_QPAD), pl.BlockSpec((1, H, MLA_QPAD, ts), lambda b, j: (b, 0, 0, j)),
                   head(MLA_V + LANES), tok(width)],
        out_shape=[jax.ShapeDtypeStruct((B, H, S, MLA_QPAD), BF16),
                   jax.ShapeDtypeStruct((B, H, MLA_QPAD, S), BF16),
                   jax.ShapeDtypeStruct((B, H, S, MLA_V + LANES), BF16),
                   jax.ShapeDtypeStruct((B, S, width), BF16)],
        compiler_params=_params(2),
        name="mla_in",
    )(x, pos.reshape(B, S, 1), ng.reshape(1, D), w_in_p, q_norm_g.reshape(1, MLA_Q_LORA), wq_p,
      kv_norm_g.reshape(1, MLA_KV_LORA), w_ukv.astype(BF16), inv)

    o = pl.pallas_call(
        functools.partial(_attn_body, tq=tq, tk=tk),
        grid=(B, H // nh, S // tq),
        in_specs=[pl.BlockSpec((1, nh, tq, MLA_QPAD), lambda b, h, i: (b, h, i, 0)),
                  pl.BlockSpec((1, nh, MLA_QPAD, S), lambda b, h, i: (b, h, 0, 0)),
                  pl.BlockSpec((1, nh, S, MLA_V + LANES), lambda b, h, i: (b, h, 0, 0)),
                  pl.BlockSpec((1, tq, nh * MLA_V), lambda b, h, i: (b, i, h))],
        out_specs=pl.BlockSpec((1, tq, nh * MLA_V), lambda b, h, i: (b, i, h)),
        out_shape=jax.ShapeDtypeStruct((B, S, width), BF16),
        scratch_shapes=[pltpu.VMEM((nh, tq, tk), F32), pltpu.VMEM((nh, tq, tk), BF16),
                        pltpu.VMEM((nh, tq, LANES), F32), pltpu.VMEM((nh, tq, LANES), F32),
                        pltpu.VMEM((nh, tq, MLA_V + LANES), F32)],
        compiler_params=_params(3, ATTN_VMEM_LIMIT_V7X),
        name="mla_attn",
    )(q, k, v, gate)

    return _proj_out(x, o, w_out, ts=ts_out, name="mla_out")


def kernel(x, positions, norm_g, pool_w_in, pool_w_grp, pool_scale, pool_w_out, gdn_w_in, gdn_conv, gdn_a_log, gdn_dt_bias, gdn_norm_g, gdn_w_out, mla_w_in, mla_q_norm_g, mla_w_uq, mla_kv_norm_g, mla_w_ukv, mla_w_out, final_g):
    depth = norm_g.shape[0]
    pool_w = (pool_w_in.astype(BF16), pool_w_grp.astype(BF16), pool_w_out.astype(BF16))
    for i in range(depth):
        kind, j = i % N_MIXERS, i // N_MIXERS
        last = i == depth - 1
        if kind == 0:
            x = _pool_layer(x, norm_g[i], pool_w[0], pool_w[1], pool_scale[j], pool_w[2], j,
                            final_g if last else None)
        elif kind == 1:
            x = _gdn_layer(x, norm_g[i], gdn_w_in[j], gdn_conv[j], gdn_a_log[j], gdn_dt_bias[j],
                           gdn_norm_g[j], gdn_w_out[j])
        else:
            x = _mla_layer(x, positions, norm_g[i], mla_w_in[j], mla_q_norm_g[j], mla_w_uq[j],
                           mla_kv_norm_g[j], mla_w_ukv[j], mla_w_out[j])
        if last and kind != 0:
            x = _final_norm(x, final_g)
    return x


def _final_norm_body(x_ref, g_ref, o_ref):
    o_ref[0] = _rms(x_ref[0], g_ref[...])


def _final_norm(x, g, *, ts=512):
    B, S, D = x.shape
    tok = pl.BlockSpec((1, ts, D), lambda b, j: (b, j, 0))
    return pl.pallas_call(
        _final_norm_body, grid=(B, S // ts), in_specs=[tok, _const_spec((1, D))], out_specs=tok,
        out_shape=jax.ShapeDtypeStruct((B, S, D), F32), compiler_params=_params(2),
        name="final_norm",
    )(x, g.reshape(1, D))
```

```python
import functools

import jax
import jax.numpy as jnp
from jax import lax
from jax.experimental import pallas as pl
from jax.experimental.pallas import tpu as pltpu

F32 = jnp.float32
BF16 = jnp.bfloat16
EPS = 1e-6
NEG = -1e30
VMEM_LIMIT_V7X = 56 * 2 ** 20
ATTN_VMEM_LIMIT_V7X = 60 * 2 ** 20
LANES = 128
SUBLANES = 8
MXU_COLS = 256

N_MIXERS = 3

POOL_WINDOWS = (2, 4, 8, 16)
POOL_HALO = SUBLANES * (max(POOL_WINDOWS).bit_length() - 1)

GDN_HEADS = 8
GDN_DK = 128
GDN_DV = 256
GDN_CONV = 4
GDN_CHUNK = 64
GDN_QK = GDN_HEADS * GDN_DK
GDN_V = GDN_HEADS * GDN_DV
GDN_UNIT = 2 * GDN_CHUNK
CONV_HALO = 8

MLA_HEADS = 16
MLA_NOPE = 128
MLA_ROPE = 64
MLA_V = 128
MLA_Q_LORA = 768
MLA_KV_LORA = 512
MLA_QK = MLA_NOPE + MLA_ROPE
MLA_QPAD = 256
ROPE_THETA = 10000.0
ATTN_STRIP = 32
LOG2E = 1.4426950408889634


def _const_spec(shape):
    zeros = (0,) * len(shape)
    return pl.BlockSpec(shape, lambda *_: zeros, pipeline_mode=pl.Buffered(1))


def _layer_spec(stacked_shape, layer):
    index = (layer,) + (0,) * (len(stacked_shape) - 1)
    return pl.BlockSpec((None,) + tuple(stacked_shape[1:]), lambda *_: index,
                        pipeline_mode=pl.Buffered(1))


def _params(n_axes, vmem_limit=VMEM_LIMIT_V7X):
    return pltpu.CompilerParams(dimension_semantics=("arbitrary",) * n_axes,
                                vmem_limit_bytes=vmem_limit)


def _rms(x, g):
    return x * lax.rsqrt(jnp.mean(x * x, axis=-1, keepdims=True) + EPS) * g


def _silu_of_twice(h):
    return h + h * jnp.tanh(h)


def _silu(x):
    return _silu_of_twice(0.5 * x)


def _mm(a, b):
    return jnp.dot(a.astype(BF16), b.astype(BF16), preferred_element_type=F32)


def _mm_nt(a, b):
    return lax.dot_general(a.astype(BF16), b.astype(BF16), (((1,), (1,)), ((), ())),
                           preferred_element_type=F32)


def _pool_body(x_ref, ng_ref, win_ref, wgrp_ref, sc_ref, wout_ref, fg_ref, o_ref, ubuf_ref,
               sa_ref, sb_ref, *, ts, nsub, width, final):
    j = pl.program_id(1)
    grp = width // len(POOL_WINDOWS)

    def window_sum(w, c0):
        levels = w.bit_length() - 1
        src, cols, shift = ubuf_ref, slice(c0, c0 + grp), 1
        for lv in range(levels):
            r0 = POOL_HALO - SUBLANES * (levels - 1 - lv)
            n = POOL_HALO + ts - r0
            val = src[r0:r0 + n, cols] + src[r0 - shift:r0 - shift + n, cols]
            if lv == levels - 1:
                return val
            dst = (sa_ref, sb_ref)[lv % 2]
            dst[r0:r0 + n, :] = val
            src, cols, shift = dst, slice(0, grp), 2 * shift

    @pl.when(j == 0)
    def _():
        ubuf_ref[0:POOL_HALO, :] = jnp.zeros((POOL_HALO, width), F32)

    for st in range(nsub):
        rows = slice(st * ts, (st + 1) * ts)
        x = x_ref[0, rows, :]
        h = _rms(x, ng_ref[...]).astype(BF16)
        u = jnp.dot(h, win_ref[:, :width], preferred_element_type=F32)
        gate = jnp.dot(h, win_ref[:, width:], preferred_element_type=F32)
        ubuf_ref[POOL_HALO:POOL_HALO + ts, :] = u
        t = (j * nsub + st) * ts + lax.broadcasted_iota(jnp.int32, (ts, 1), 0)
        parts = []
        for gi, w in enumerate(POOL_WINDOWS):
            c0 = gi * grp
            cnt = jnp.minimum(t + 1, w).astype(F32)
            p = window_sum(w, c0) / cnt - u[:, c0:c0 + grp]
            parts.append(jnp.dot(p.astype(BF16), wgrp_ref[gi], preferred_element_type=F32))
        pg = jnp.concatenate(parts, axis=1)
        y = pg * sc_ref[...] * _silu(gate)
        out = x + jnp.dot(y.astype(BF16), wout_ref[...], preferred_element_type=F32)
        ubuf_ref[0:POOL_HALO, :] = u[ts - POOL_HALO:, :]
        if final:
            out = _rms(out, fg_ref[...])
        o_ref[0, rows, :] = out


def _pool_layer(x, ng, w_in, w_grp, scale, w_out, layer, final_g, *, ts=512, nsub=2):
    B, S, D = x.shape
    width = w_out.shape[1]
    final = final_g is not None
    fg = final_g if final else jnp.ones((D,), F32)
    body = functools.partial(_pool_body, ts=ts, nsub=nsub, width=width, final=final)
    tb = ts * nsub
    return pl.pallas_call(
        body,
        grid=(B, S // tb),
        in_specs=[
            pl.BlockSpec((1, tb, D), lambda b, j: (b, j, 0)),
            _const_spec((1, D)),
            _layer_spec(w_in.shape, layer),
            _layer_spec(w_grp.shape, layer),
            _const_spec((1, width)),
            _layer_spec(w_out.shape, layer),
            _const_spec((1, D)),
        ],
        out_specs=pl.BlockSpec((1, tb, D), lambda b, j: (b, j, 0)),
        out_shape=jax.ShapeDtypeStruct((B, S, D), F32),
        scratch_shapes=[pltpu.VMEM((POOL_HALO + ts, width), F32),
                        pltpu.VMEM((POOL_HALO + ts, width // len(POOL_WINDOWS)), F32),
                        pltpu.VMEM((POOL_HALO + ts, width // len(POOL_WINDOWS)), F32)],
        compiler_params=_params(2),
        name="pool_layer",
    )(x, ng.reshape(1, D), w_in, w_grp, scale.reshape(1, width), w_out, fg.reshape(1, D))


def _gdn_in_body(x_ref, ng_ref, win_ref, wba_ref, conv_ref, alog_ref, dtb_ref,
                 q_ref, k_ref, v_ref, gate_ref, gb_ref, cbuf_ref, *, ts, nsub):
    j = pl.program_id(1)
    nch = 2 * GDN_QK + GDN_V

    @pl.when(j == 0)
    def _():
        cbuf_ref[0:CONV_HALO, :] = jnp.zeros((CONV_HALO, nch), F32)

    lane = lax.broadcasted_iota(jnp.int32, (1, LANES), 1)
    r = lax.broadcasted_iota(jnp.int32, (ts, 1), 0)
    c = lax.broadcasted_iota(jnp.int32, (1, ts), 1)
    tri = jnp.where(((r // GDN_CHUNK) == (c // GDN_CHUNK)) & (c <= r), 1.0, 0.0).astype(BF16)

    for st in range(nsub):
        rows = slice(st * ts, (st + 1) * ts)
        x = x_ref[0, rows, :]
        xn = _rms(x, ng_ref[...])
        h = xn.astype(BF16)
        h_half = (0.5 * xn).astype(BF16)

        for c0 in range(0, nch, MXU_COLS):
            cols = slice(c0, c0 + MXU_COLS)
            pre = jnp.dot(h_half, win_ref[:, cols], preferred_element_type=F32)
            cbuf_ref[CONV_HALO:CONV_HALO + ts, cols] = pre
            acc = pre * conv_ref[GDN_CONV - 1:GDN_CONV, cols]
            for kk in range(GDN_CONV - 1):
                r0 = CONV_HALO - (GDN_CONV - 1) + kk
                acc = acc + cbuf_ref[r0:r0 + ts, cols] * conv_ref[kk:kk + 1, cols]
            cbuf_ref[0:CONV_HALO, cols] = pre[ts - CONV_HALO:, :]
            a = _silu_of_twice(acc)
            if c0 < 2 * GDN_QK:
                is_q = c0 < GDN_QK
                out_ref, base = (q_ref, c0) if is_q else (k_ref, c0 - GDN_QK)
                for d0 in range(0, MXU_COLS, GDN_DK):
                    ah = a[:, d0:d0 + GDN_DK]
                    an = ah * lax.rsqrt(jnp.sum(ah * ah, axis=-1, keepdims=True) + EPS)
                    if is_q:
                        an = an * (GDN_DK ** -0.5)
                    out_ref[0, rows, base + d0:base + d0 + GDN_DK] = an.astype(BF16)
            else:
                v0 = c0 - 2 * GDN_QK
                v_ref[0, rows, v0:v0 + MXU_COLS] = a.astype(BF16)
        for c0 in range(0, GDN_V, MXU_COLS):
            gate = jnp.dot(h_half, win_ref[:, nch + c0:nch + c0 + MXU_COLS],
                           preferred_element_type=F32)
            gate_ref[0, rows, c0:c0 + MXU_COLS] = _silu_of_twice(gate).astype(BF16)

        ba = jnp.dot(h, wba_ref[...], preferred_element_type=F32)
        z = ba + dtb_ref[...]
        softplus = jnp.maximum(z, 0.0) + jnp.log1p(jnp.exp(-jnp.abs(z)))
        g = jnp.where((lane >= GDN_HEADS) & (lane < 2 * GDN_HEADS),
                      -jnp.exp(alog_ref[...]) * softplus, 0.0)
        g_hi = g.astype(BF16)
        r1 = g - g_hi.astype(F32)
        g_mid = r1.astype(BF16)
        g_lo = (r1 - g_mid.astype(F32)).astype(BF16)
        gc = (jnp.dot(tri, g_hi, preferred_element_type=F32)
              + jnp.dot(tri, g_mid, preferred_element_type=F32)
              + jnp.dot(tri, g_lo, preferred_element_type=F32))
        gb_ref[0, rows, :] = jnp.where(lane < GDN_HEADS, jax.nn.sigmoid(ba), gc)


def _gdn_core_body(q_ref, k_ref, v_ref, gb_ref, gbt_ref, gate_ref, ng_ref, o_ref, s_ref):
    j = pl.program_id(1)

    @pl.when(j == 0)
    def _():
        s_ref[...] = jnp.zeros_like(s_ref)

    n = GDN_UNIT
    r = lax.broadcasted_iota(jnp.int32, (n, 1), 0)
    c = lax.broadcasted_iota(jnp.int32, (1, n), 1)
    same = (r // GDN_CHUNK) == (c // GDN_CHUNK)
    causal = same & (c <= r)
    strict = same & (c < r)
    eye = jnp.where(r == c, 1.0, 0.0)
    zeros_c = jnp.zeros((GDN_CHUNK, GDN_DV), F32)
    nb = q_ref.shape[0]
    hs = range(nb * GDN_HEADS)
    bi = [ci // GDN_HEADS for ci in hs]
    hi = [ci % GDN_HEADS for ci in hs]
    gb = [gb_ref[b] for b in range(nb)]
    gbt = [gbt_ref[b] for b in range(nb)]
    beta_c = [gb[bi[h]][:, hi[h]:hi[h] + 1] for h in hs]
    gc_c = [gb[bi[h]][:, GDN_HEADS + hi[h]:GDN_HEADS + hi[h] + 1] for h in hs]
    gc_r = [gbt[bi[h]][GDN_HEADS + hi[h]:GDN_HEADS + hi[h] + 1, :] for h in hs]
    q = [q_ref[bi[h], :, hi[h] * GDN_DK:(hi[h] + 1) * GDN_DK] for h in hs]
    k = [k_ref[bi[h], :, hi[h] * GDN_DK:(hi[h] + 1) * GDN_DK] for h in hs]
    kf = [k[h].astype(F32) for h in hs]
    kb = [kf[h] * beta_c[h] for h in hs]
    decay = [jnp.exp(jnp.where(causal, gc_c[h] - gc_r[h], NEG)) for h in hs]
    kkqk = [_mm_nt(jnp.concatenate([kb[h].astype(BF16), q[h]], axis=0), k[h]) for h in hs]
    lmat = [jnp.where(strict, kkqk[h][:n] * decay[h], 0.0) for h in hs]
    attn = [kkqk[h][n:] * decay[h] for h in hs]
    ch = GDN_CHUNK
    eye_s = eye[:ch] + eye[ch:]

    def block_diag(side):
        return jnp.where(same, jnp.concatenate([side, side], axis=0), 0.0)

    l_s = [lmat[h][:ch] + lmat[h][ch:] for h in hs]
    xs = [eye_s - l_s[h] for h in hs]
    ps = [_mm(l_s[h], lmat[h]) for h in hs]
    for _ in range(4):
        res = [_mm(jnp.concatenate([xs[h], ps[h]], axis=0), block_diag(ps[h])) for h in hs]
        xs = [xs[h] + res[h][:ch] for h in hs]
        ps = [res[h][ch:] for h in hs]
    res = [_mm(xs[h], block_diag(ps[h])) for h in hs]
    xm = [block_diag(xs[h] + res[h]) for h in hs]
    eg = [jnp.exp(gc_c[h]) for h in hs]
    rhs = [jnp.concatenate(
        [v_ref[bi[h], :, hi[h] * GDN_DV:(hi[h] + 1) * GDN_DV].astype(F32) * beta_c[h],
         kb[h] * eg[h]], axis=1) for h in hs]
    sol = [rhs[h] + _mm(xm[h] - eye, rhs[h]) for h in hs]
    qg = [q[h].astype(F32) * eg[h] for h in hs]
    k_t = [kf[h].T for h in hs]
    state = [s_ref[h] for h in hs]
    for cc in range(2):
        rows = slice(cc * GDN_CHUNK, (cc + 1) * GDN_CHUNK)
        in_chunk = (c // GDN_CHUNK) == cc
        g_last = [gc_r[h][:, (cc + 1) * GDN_CHUNK - 1:(cc + 1) * GDN_CHUNK] for h in hs]
        res = [_mm(jnp.concatenate([sol[h][rows, GDN_DV:], qg[h][rows]], axis=0), state[h])
               for h in hs]
        v_new = [sol[h][rows, :GDN_DV] - res[h][:GDN_CHUNK] for h in hs]
        v_all = [jnp.concatenate([v_new[h], zeros_c] if cc == 0 else [zeros_c, v_new[h]], axis=0)
                 for h in hs]
        k_dec_t = [k_t[h] * jnp.exp(jnp.where(in_chunk, g_last[h] - gc_r[h], NEG)) for h in hs]
        upd = [_mm(jnp.concatenate([attn[h][rows], k_dec_t[h]], axis=0), v_all[h]) for h in hs]
        o_c = [res[h][GDN_CHUNK:] + upd[h][:GDN_CHUNK] for h in hs]
        state = [state[h] * jnp.exp(g_last[h]) + upd[h][GDN_CHUNK:] for h in hs]
        for h in hs:
            cols = slice(hi[h] * GDN_DV, (hi[h] + 1) * GDN_DV)
            gated = _rms(o_c[h], ng_ref[...]) * gate_ref[bi[h], rows, cols].astype(F32)
            o_ref[bi[h], rows, cols] = gated.astype(o_ref.dtype)
    for h in hs:
        s_ref[h] = state[h]


def _proj_out_body(x_ref, o_ref, wout_ref, y_ref):
    y_ref[0] = x_ref[0] + jnp.dot(o_ref[0], wout_ref[...], preferred_element_type=F32)


def _proj_out(x, o, w_out, *, ts, name):
    B, S, D = x.shape
    tok = lambda w: pl.BlockSpec((1, ts, w), lambda b, j: (b, j, 0))
    return pl.pallas_call(
        _proj_out_body,
        grid=(B, S // ts),
        in_specs=[tok(D), tok(o.shape[-1]), _const_spec(w_out.shape)],
        out_specs=tok(D),
        out_shape=jax.ShapeDtypeStruct((B, S, D), F32),
        compiler_params=_params(2),
        name=name,
    )(x, o, w_out.astype(BF16))


def _gdn_layer(x, ng, w_in, conv_w, a_log, dt_bias, norm_g, w_out, *, ts=256, nsub=1, ts_out=1024):
    B, S, D = x.shape
    nch = 2 * GDN_QK + GDN_V
    w_all = w_in.astype(BF16)
    w_ba = jnp.pad(w_in[:, nch + GDN_V:], ((0, 0), (0, LANES - 2 * GDN_HEADS))).astype(BF16)
    pad_a = (GDN_HEADS, LANES - 2 * GDN_HEADS)
    alog = jnp.pad(a_log.astype(F32), pad_a).reshape(1, LANES)
    dtb = jnp.pad(dt_bias.astype(F32), pad_a).reshape(1, LANES)

    tb = ts * nsub
    tok = lambda w: pl.BlockSpec((1, tb, w), lambda b, j: (b, j, 0))
    q, k, v, gate, gb = pl.pallas_call(
        functools.partial(_gdn_in_body, ts=ts, nsub=nsub),
        grid=(B, S // tb),
        in_specs=[tok(D), _const_spec((1, D)), _const_spec(w_all.shape), _const_spec(w_ba.shape),
                  _const_spec(conv_w.shape), _const_spec((1, LANES)), _const_spec((1, LANES))],
        out_specs=[tok(GDN_QK), tok(GDN_QK), tok(GDN_V), tok(GDN_V), tok(LANES)],
        out_shape=[jax.ShapeDtypeStruct((B, S, GDN_QK), BF16),
                   jax.ShapeDtypeStruct((B, S, GDN_QK), BF16),
                   jax.ShapeDtypeStruct((B, S, GDN_V), BF16),
                   jax.ShapeDtypeStruct((B, S, GDN_V), BF16),
                   jax.ShapeDtypeStruct((B, S, LANES), F32)],
        scratch_shapes=[pltpu.VMEM((CONV_HALO + ts, nch), F32)],
        compiler_params=_params(2),
        name="gdn_in",
    )(x, ng.reshape(1, D), w_all, w_ba, conv_w, alog, dtb)

    gbt = jnp.transpose(gb[:, :, :2 * GDN_HEADS], (0, 2, 1))
    n = GDN_UNIT
    nb = 4 if B % 4 == 0 else 1
    utok = lambda w: pl.BlockSpec((nb, n, w), lambda b, j: (b, j, 0))
    o = pl.pallas_call(
        _gdn_core_body,
        grid=(B // nb, S // n),
        in_specs=[utok(GDN_QK), utok(GDN_QK), utok(GDN_V), utok(LANES),
                  pl.BlockSpec((nb, 2 * GDN_HEADS, n), lambda b, j: (b, 0, j)),
                  utok(GDN_V), _const_spec((1, GDN_DV))],
        out_specs=utok(GDN_V),
        out_shape=jax.ShapeDtypeStruct((B, S, GDN_V), BF16),
        scratch_shapes=[pltpu.VMEM((nb * GDN_HEADS, GDN_DK, GDN_DV), F32)],
        compiler_params=_params(2),
        name="gdn_core",
    )(q, k, v, gb, gbt, gate, norm_g.reshape(1, GDN_DV))

    return _proj_out(x, o, w_out, ts=ts_out, name="gdn_out")


def _mla_in_body(x_ref, pos_ref, ng_ref, win_ref, qg_ref, wuq_ref, kvg_ref, wukv_ref, inv_ref,
                 q_ref, k_ref, v_ref, gate_ref):
    c1 = MLA_Q_LORA
    c2 = c1 + MLA_KV_LORA
    c3 = c2 + LANES
    x = x_ref[0]
    h = _rms(x, ng_ref[...]).astype(BF16)
    cq = jnp.dot(h, win_ref[:, :c1], preferred_element_type=F32)
    ckv = jnp.dot(h, win_ref[:, c1:c2], preferred_element_type=F32)
    kr = jnp.dot(h, win_ref[:, c2:c3], preferred_element_type=F32)
    gate = jnp.dot(h, win_ref[:, c3:], preferred_element_type=F32)
    gate_ref[0] = _silu(gate).astype(BF16)

    half = MLA_ROPE // 2
    ang = pos_ref[0].astype(F32) * inv_ref[...]
    cos = jnp.cos(ang)
    sin = jnp.sin(ang)
    lane = lax.broadcasted_iota(jnp.int32, (1, LANES), 1)
    is_x1 = (lane // half) % 2 == 0
    sin_s = sin * jnp.where(is_x1, -1.0, 1.0)

    def rope(t):
        partner = jnp.where(is_x1, pltpu.roll(t, LANES - half, 1), pltpu.roll(t, half, 1))
        return t * cos + partner * sin_s

    k_rope_t = rope(kr).T.astype(BF16)
    qf = jnp.dot(_rms(cq, qg_ref[...]).astype(BF16), wuq_ref[...],
                 preferred_element_type=F32) * (MLA_QK ** -0.5 * LOG2E)
    kv = jnp.dot(_rms(ckv, kvg_ref[...]).astype(BF16), wukv_ref[...], preferred_element_type=F32)
    nope_w = MLA_HEADS * MLA_NOPE
    q_pad = jnp.zeros((qf.shape[0], MLA_QPAD - MLA_QK), F32)
    for hh in range(MLA_HEADS):
        if hh % 2 == 0:
            r0 = nope_w + (hh // 2) * LANES
            roped = rope(qf[:, r0:r0 + LANES])
        e = (hh % 2) * MLA_ROPE
        q_ref[0, hh] = jnp.concatenate(
            [qf[:, hh * MLA_NOPE:(hh + 1) * MLA_NOPE], roped[:, e:e + MLA_ROPE], q_pad],
            axis=1).astype(BF16)
        k0 = hh * (MLA_NOPE + MLA_V)
        k_ref[0, hh, 0:MLA_NOPE, :] = kv[:, k0:k0 + MLA_NOPE].T.astype(BF16)
        k_ref[0, hh, MLA_NOPE:, :] = k_rope_t
        v_ref[0, hh, :, :MLA_V] = kv[:, k0 + MLA_NOPE:k0 + MLA_NOPE + MLA_V].astype(BF16)
        v_ref[0, hh, :, MLA_V:] = jnp.ones((kv.shape[0], LANES), BF16)


def _attn_body(q_ref, k_ref, v_ref, gate_ref, o_ref, s_ref, p_ref, m_ref, a_ref, acc_ref, *, tq, tk):
    i = pl.program_id(2)
    nh = q_ref.shape[1]
    m_ref[...] = jnp.full(m_ref.shape, NEG, F32)
    acc_ref[...] = jnp.zeros(acc_ref.shape, F32)

    def step(c0, parts, masked):
        chains = [(h,) + part for part in parts for h in range(nh)]

        def scores(h, lo, hi, width):
            s_ref[h, lo:hi, :width] = jnp.dot(q_ref[0, h, lo:hi, :],
                                              k_ref[0, h, :, pl.ds(c0, width)],
                                              preferred_element_type=F32)

        def softmax(h, lo, hi, width):
            for r0 in range(lo, hi, ATTN_STRIP):
                rs = slice(r0, r0 + ATTN_STRIP)
                pieces = []
                for c in range(0, width, LANES):
                    if masked and c >= r0 + ATTN_STRIP:
                        pieces.append(None)
                        continue
                    sc = s_ref[h, rs, c:c + LANES]
                    if masked and c + LANES - 1 > r0:
                        sub = lax.broadcasted_iota(jnp.int32, (ATTN_STRIP, 1), 0)
                        ln = lax.broadcasted_iota(jnp.int32, (1, LANES), 1)
                        sc = jnp.where(c + ln <= r0 + sub, sc, NEG)
                    pieces.append(sc)
                visible = [sc for sc in pieces if sc is not None]
                mx = visible[0]
                for sc in visible[1:]:
                    mx = jnp.maximum(mx, sc)
                m_old = m_ref[h, rs, :]
                m_new = jnp.maximum(m_old, jnp.max(mx, axis=-1, keepdims=True))
                a_ref[h, rs, :] = jnp.exp2(m_old - m_new)
                m_ref[h, rs, :] = m_new
                zero = jnp.zeros((ATTN_STRIP, LANES), F32)
                p_ref[h, rs, :width] = jnp.concatenate(
                    [zero if sc is None else jnp.exp2(sc - m_new) for sc in pieces],
                    axis=1).astype(BF16)

        def values(h, lo, hi, width):
            pv = jnp.dot(p_ref[h, lo:hi, :width], v_ref[0, h, pl.ds(c0, width), :],
                         preferred_element_type=F32)
            a = a_ref[h, lo:hi, :]
            acc_ref[h, lo:hi, :MLA_V] = a * acc_ref[h, lo:hi, :MLA_V] + pv[:, :MLA_V]
            acc_ref[h, lo:hi, MLA_V:] = a * acc_ref[h, lo:hi, MLA_V:] + pv[:, MLA_V:]

        for stage in (scores, softmax, values):
            for chain in chains:
                stage(*chain)

    n_wide = (i * tq) // tk

    def wide_step(kb, carry):
        step(pl.multiple_of(kb * tk, tk), [(0, tq, tk)], False)
        return carry

    lax.fori_loop(0, n_wide, wide_step, 0)
    for d in range(tk // tq - 1):
        @pl.when(n_wide * tk + d * tq < i * tq)
        def _():
            step(pl.multiple_of(n_wide * tk + d * tq, tq), [(0, tq, tq)], False)
    step(pl.multiple_of(i * tq, tq), [(0, tq // 2, tq // 2), (tq // 2, tq, tq)], True)
    for h in range(nh):
        cols = slice(h * MLA_V, (h + 1) * MLA_V)
        o = acc_ref[h, :, :MLA_V] / acc_ref[h, :, MLA_V:]
        o_ref[0, :, cols] = (o * gate_ref[0, :, cols].astype(F32)).astype(o_ref.dtype)


def _mla_layer(x, pos, ng, w_in, q_norm_g, w_uq, kv_norm_g, w_ukv, w_out, *, ts=256, ts_out=1024, tq=512, tk=1024,
               nh=4):
    B, S, D = x.shape
    H = MLA_HEADS
    half = MLA_ROPE // 2
    c1 = MLA_Q_LORA
    c2 = c1 + MLA_KV_LORA
    c3 = c2 + MLA_ROPE
    w_in_p = jnp.concatenate(
        [w_in[:, :c3], jnp.zeros((D, LANES - MLA_ROPE), w_in.dtype), w_in[:, c3:]],
        axis=1).astype(BF16)
    wq = w_uq.reshape(MLA_Q_LORA, H, MLA_QK)
    wq_p = jnp.concatenate([wq[:, :, :MLA_NOPE].reshape(MLA_Q_LORA, H * MLA_NOPE),
                            wq[:, :, MLA_NOPE:].reshape(MLA_Q_LORA, H * MLA_ROPE)],
                           axis=1).astype(BF16)
    inv = ROPE_THETA ** (-jnp.arange(half, dtype=F32) / half)
    inv = jnp.tile(inv, LANES // half).reshape(1, LANES)
    width = H * MLA_V

    tok = lambda w: pl.BlockSpec((1, ts, w), lambda b, j: (b, j, 0))
    head = lambda w: pl.BlockSpec((1, H, ts, w), lambda b, j: (b, 0, j, 0))
    q, k, v, gate = pl.pallas_call(
        _mla_in_body,
        grid=(B, S // ts),
        in_specs=[tok(D), tok(1), _const_spec((1, D)), _const_spec(w_in_p.shape),
                  _const_spec((1, MLA_Q_LORA)), _const_spec(wq_p.shape),
                  _const_spec((1, MLA_KV_LORA)), _const_spec(w_ukv.shape), _const_spec((1, LANES))],
        out_specs=[head(MLA_QPAD), pl.BlockSpec((1, H, MLA_QPAD, ts), lambda b, j: (b, 0, 0, j)),
                   head(MLA_V + LANES), tok(width)],
        out_shape=[jax.ShapeDtypeStruct((B, H, S, MLA_QPAD), BF16),
                   jax.ShapeDtypeStruct((B, H, MLA_QPAD, S), BF16),
                   jax.ShapeDtypeStruct((B, H, S, MLA_V + LANES), BF16),
                   jax.ShapeDtypeStruct((B, S, width), BF16)],
        compiler_params=_params(2),
        name="mla_in",
    )(x, pos.reshape(B, S, 1), ng.reshape(1, D), w_in_p, q_norm_g.reshape(1, MLA_Q_LORA), wq_p,
      kv_norm_g.reshape(1, MLA_KV_LORA), w_ukv.astype(BF16), inv)

    o = pl.pallas_call(
        functools.partial(_attn_body, tq=tq, tk=tk),
        grid=(B, H // nh, S // tq),
        in_specs=[pl.BlockSpec((1, nh, tq, MLA_QPAD), lambda b, h, i: (b, h, i, 0)),
                  pl.BlockSpec((1, nh, MLA_QPAD, S), lambda b, h, i: (b, h, 0, 0)),
                  pl.BlockSpec((1, nh, S, MLA_V + LANES), lambda b, h, i: (b, h, 0, 0)),
                  pl.BlockSpec((1, tq, nh * MLA_V), lambda b, h, i: (b, i, h))],
        out_specs=pl.BlockSpec((1, tq, nh * MLA_V), lambda b, h, i: (b, i, h)),
        out_shape=jax.ShapeDtypeStruct((B, S, width), BF16),
        scratch_shapes=[pltpu.VMEM((nh, tq, tk), F32), pltpu.VMEM((nh, tq, tk), BF16),
                        pltpu.VMEM((nh, tq, LANES), F32), pltpu.VMEM((nh, tq, LANES), F32),
                        pltpu.VMEM((nh, tq, MLA_V + LANES), F32)],
        compiler_params=_params(3, ATTN_VMEM_LIMIT_V7X),
        name="mla_attn",
    )(q, k, v, gate)

    return _proj_out(x, o, w_out, ts=ts_out, name="mla_out")


def kernel(x, positions, norm_g, pool_w_in, pool_w_grp, pool_scale, pool_w_out, gdn_w_in, gdn_conv, gdn_a_log, gdn_dt_bias, gdn_norm_g, gdn_w_out, mla_w_in, mla_q_norm_g, mla_w_uq, mla_kv_norm_g, mla_w_ukv, mla_w_out, final_g):
    depth = norm_g.shape[0]
    pool_w = (pool_w_in.astype(BF16), pool_w_grp.astype(BF16), pool_w_out.astype(BF16))
    for i in range(depth):
        kind, j = i % N_MIXERS, i // N_MIXERS
        last = i == depth - 1
        if kind == 0:
            x = _pool_layer(x, norm_g[i], pool_w[0], pool_w[1], pool_scale[j], pool_w[2], j,
                            final_g if last else None)
        elif kind == 1:
            x = _gdn_layer(x, norm_g[i], gdn_w_in[j], gdn_conv[j], gdn_a_log[j], gdn_dt_bias[j],
                           gdn_norm_g[j], gdn_w_out[j])
        else:
            x = _mla_layer(x, positions, norm_g[i], mla_w_in[j], mla_q_norm_g[j], mla_w_uq[j],
                           mla_kv_norm_g[j], mla_w_ukv[j], mla_w_out[j])
        if last and kind != 0:
            x = _final_norm(x, final_g)
    return x


def _final_norm_body(x_ref, g_ref, o_ref):
    o_ref[0] = _rms(x_ref[0], g_ref[...])


def _final_norm(x, g, *, ts=512):
    B, S, D = x.shape
    tok = pl.BlockSpec((1, ts, D), lambda b, j: (b, j, 0))
    return pl.pallas_call(
        _final_norm_body, grid=(B, S // ts), in_specs=[tok, _const_spec((1, D))], out_specs=tok,
        out_shape=jax.ShapeDtypeStruct((B, S, D), F32), compiler_params=_params(2),
        name="final_norm",
    )(x, g.reshape(1, D))
```
